```python
import jax, jax.numpy as jnp
from jax import lax
import numpy as np

D_MODEL = 1024
BATCH = 4
SEQ = 4096
DEPTH = 2

HEAD_DIM = 64
N_HEADS = 8
N_KV_HEADS = 2
ROT_DIM = HEAD_DIM // 4
ROPE_THETA = 500000.0
IDX_HEADS = 8
IDX_DIM = 64
MAX_TOPK = 256
Q_BLOCK = 128
POOL_CH = D_MODEL // 2
POOL_WINDOWS = (2, 4, 8, 16)
POOL_GROUP = POOL_CH // 4
N_EXPERTS = 32
TOP_K = 4
D_FF = D_MODEL
SWIGLU_LIMIT = 7.0
SWIGLU_ALPHA = 1.702
EXPERT_BLOCK = 128
PLE_DIM = 256
LN_EPS = 1e-5
DN_ALPHA = (2 * DEPTH) ** 0.25
DN_BETA = (8 * DEPTH) ** -0.25
SPLIT_SIZES = (N_HEADS * HEAD_DIM, N_KV_HEADS * HEAD_DIM, N_KV_HEADS * HEAD_DIM,
               IDX_HEADS * IDX_DIM, IDX_DIM, IDX_HEADS, POOL_CH, D_MODEL, D_MODEL)
D_IN = sum(SPLIT_SIZES)

kernel_name = 'hybrid_dsa_pool_moe_deepnorm'


def _layer_norm(x, g, b):
    xf = x.astype(jnp.float32)
    mu = xf.mean(-1, keepdims=True)
    var = jnp.square(xf - mu).mean(-1, keepdims=True)
    y = (xf - mu) * lax.rsqrt(var + LN_EPS) * g.astype(jnp.float32) + b.astype(jnp.float32)
    return y.astype(x.dtype)


def _rope_tables(L):
    pos = jnp.arange(L, dtype=jnp.float32)
    inv = ROPE_THETA ** (-jnp.arange(0, ROT_DIM, 2, dtype=jnp.float32) / ROT_DIM)
    ang = pos[:, None] * inv[None, :]
    return jnp.cos(ang), jnp.sin(ang)


def _partial_rope(x, cos, sin):
    half = ROT_DIM // 2
    xf = x.astype(jnp.float32)
    x1, x2 = xf[..., :half], xf[..., half:ROT_DIM]
    c, s = cos[None, :, None, :], sin[None, :, None, :]
    out = jnp.concatenate([x1 * c - x2 * s, x2 * c + x1 * s, xf[..., ROT_DIM:]], axis=-1)
    return out.astype(x.dtype)


def _dsa_attention(q, k, v, qi, ki, wi):
    B, L, H, Dh = q.shape
    n_sel = min(MAX_TOPK, L // 4)
    n_blk = L // Q_BLOCK
    group = N_HEADS // N_KV_HEADS
    key_pos = jnp.arange(L)
    wi = wi * (IDX_HEADS ** -0.5)

    def to_blocks(a):
        return a.reshape((B, n_blk, Q_BLOCK) + a.shape[2:]).swapaxes(0, 1)

    def one_block(args):
        qb, qib, wib, start = args
        qpos = start + jnp.arange(Q_BLOCK)
        causal = key_pos[None, :] <= qpos[:, None]
        dots = jnp.einsum('bqhd,bsd->bqhs', qib.astype(jnp.float32),
                          ki.astype(jnp.float32)) * (IDX_DIM ** -0.5)
        score = jnp.einsum('bqhs,bqh->bqs', jax.nn.relu(dots), wib.astype(jnp.float32))
        score = jnp.where(causal[None], score, -jnp.inf)
        _, sel = lax.top_k(score, n_sel)
        kg = jax.vmap(lambda kb, ib: kb[ib])(k, sel)
        vg = jax.vmap(lambda vb, ib: vb[ib])(v, sel)
        qg = qb.reshape(B, Q_BLOCK, N_KV_HEADS, group, Dh)
        logits = jnp.einsum('bqcgd,bqncd->bqcgn', qg, kg).astype(jnp.float32) * (Dh ** -0.5)
        valid = sel <= qpos[None, :, None]
        logits = jnp.where(valid[:, :, None, None, :], logits, -jnp.inf)
        probs = jax.nn.softmax(logits, axis=-1).astype(v.dtype)
        o = jnp.einsum('bqcgn,bqncd->bqcgd', probs, vg)
        return o.reshape(B, Q_BLOCK, H * Dh)

    starts = jnp.arange(n_blk) * Q_BLOCK
    out = lax.map(one_block, (to_blocks(q), to_blocks(qi), to_blocks(wi), starts))
    return out.swapaxes(0, 1).reshape(B, L, H * Dh)


def _pool_mixer(u, pool_w, pool_scale):
    B, L, C = u.shape
    uf = u.astype(jnp.float32)
    csum = jnp.pad(jnp.cumsum(uf, axis=1), ((0, 0), (1, 0), (0, 0)))
    pos1 = jnp.arange(1, L + 1, dtype=jnp.float32)[None, :, None]
    parts = []
    for g, win in enumerate(POOL_WINDOWS):
        sl = slice(g * POOL_GROUP, (g + 1) * POOL_GROUP)
        cg = csum[..., sl]
        upper = cg[:, 1:]
        lower = jnp.pad(cg, ((0, 0), (win - 1, 0), (0, 0)))[:, :L]
        mean = (upper - lower) / jnp.minimum(pos1, float(win))
        parts.append(mean - uf[..., sl])
    d = jnp.stack(parts, axis=2).astype(u.dtype)
    y = jnp.einsum('blgc,gcd->blgd', d, pool_w).reshape(B, L, C)
    return y * pool_scale


def _mixer(h, cos, sin, w_in, pool_w, pool_scale, w_up_attn, w_up_pool, w_out):
    B, L, _ = h.shape
    z = h @ w_in
    offsets = [int(o) for o in np.cumsum(SPLIT_SIZES)[:-1]]
    q, k, v, qi, ki, wi, u, g_attn, g_pool = jnp.split(z, offsets, axis=-1)
    q = _partial_rope(q.reshape(B, L, N_HEADS, HEAD_DIM), cos, sin)
    k = _partial_rope(k.reshape(B, L, N_KV_HEADS, HEAD_DIM), cos, sin)
    v = v.reshape(B, L, N_KV_HEADS, HEAD_DIM)
    qi = _partial_rope(qi.reshape(B, L, IDX_HEADS, IDX_DIM), cos, sin)
    ki = _partial_rope(ki.reshape(B, L, 1, IDX_DIM), cos, sin)[:, :, 0]
    y_attn = _dsa_attention(q, k, v, qi, ki, wi)
    y_pool = _pool_mixer(u, pool_w, pool_scale)
    merged = jax.nn.sigmoid(g_attn) * (y_attn @ w_up_attn) + jax.nn.sigmoid(g_pool) * (y_pool @ w_up_pool)
    return merged @ w_out


def _moe(h, router_w, router_b, w_gu, b_gu, w_down, b_down):
    B, L, D = h.shape
    N = B * L
    hf = h.reshape(N, D)
    logits = (hf @ router_w + router_b).astype(jnp.float32)
    top_val, top_idx = lax.top_k(logits, TOP_K)
    gates = jax.nn.softmax(top_val, axis=-1)
    A = N * TOP_K
    e_flat = top_idx.reshape(A).astype(jnp.int32)
    tok_flat = jnp.repeat(jnp.arange(N, dtype=jnp.int32), TOP_K)
    g_flat = gates.reshape(A)
    order = jnp.argsort(e_flat)
    e_sorted = e_flat[order]
    counts = jnp.zeros((N_EXPERTS,), jnp.int32).at[e_flat].add(1)
    padded = (counts + EXPERT_BLOCK - 1) // EXPERT_BLOCK * EXPERT_BLOCK
    pend = jnp.cumsum(padded)
    pstart = pend - padded
    gstart = jnp.cumsum(counts) - counts
    dest = pstart[e_sorted] + (jnp.arange(A, dtype=jnp.int32) - gstart[e_sorted])
    n_blocks = -(-A // EXPERT_BLOCK) + N_EXPERTS
    R = n_blocks * EXPERT_BLOCK
    row_tok = jnp.full((R,), N, jnp.int32).at[dest].set(tok_flat[order])
    row_gate = jnp.zeros((R,), jnp.float32).at[dest].set(g_flat[order])
    blk_start = jnp.arange(n_blocks, dtype=jnp.int32) * EXPERT_BLOCK
    blk_exp = jnp.minimum(jnp.searchsorted(pend, blk_start, side='right'), N_EXPERTS - 1)
    h_pad = jnp.concatenate([hf, jnp.zeros((1, D), hf.dtype)], axis=0)
    xr = h_pad[row_tok].reshape(n_blocks, EXPERT_BLOCK, D)

    def expert_block(args):
        xb, e = args
        gu = xb @ w_gu[e] + b_gu[e]
        gt, up = gu[:, :D_FF], gu[:, D_FF:]
        gt = jnp.minimum(gt, SWIGLU_LIMIT)
        up = jnp.clip(up, -SWIGLU_LIMIT, SWIGLU_LIMIT)
        act = gt * jax.nn.sigmoid(SWIGLU_ALPHA * gt) * (up + 1.0)
        return act @ w_down[e] + b_down[e]

    yr = lax.map(expert_block, (xr, blk_exp)).reshape(R, D)
    y = jax.ops.segment_sum(yr.astype(jnp.float32) * row_gate[:, None], row_tok, num_segments=N + 1)[:N]
    return y.astype(h.dtype).reshape(B, L, D)


def setup_inputs(seed: int = 0) -> dict:
    key = jax.random.key(seed)
    ks = jax.random.split(key, 24)

    def nrm(k, shape, scale):
        return jax.random.normal(k, shape, jnp.float32) * scale

    Ld = DEPTH
    return {
        'x': nrm(ks[0], (BATCH, SEQ, D_MODEL), 1.0),
        'p': nrm(ks[1], (DEPTH, BATCH, SEQ, PLE_DIM), 1.0),
        'ln0_g': 1.0 + nrm(ks[2], (D_MODEL,), 0.1),
        'ln0_b': nrm(ks[3], (D_MODEL,), 0.01),
        'w_in': nrm(ks[4], (Ld, D_MODEL, D_IN), D_MODEL ** -0.5),
        'pool_w': nrm(ks[5], (Ld, 4, POOL_GROUP, POOL_GROUP), POOL_GROUP ** -0.5),
        'pool_scale': 1.0 + nrm(ks[6], (Ld, POOL_CH), 0.1),
        'w_up_attn': nrm(ks[7], (Ld, N_HEADS * HEAD_DIM, D_MODEL), (N_HEADS * HEAD_DIM) ** -0.5),
        'w_up_pool': nrm(ks[8], (Ld, POOL_CH, D_MODEL), POOL_CH ** -0.5),
        'w_out': nrm(ks[9], (Ld, D_MODEL, D_MODEL), DN_BETA * D_MODEL ** -0.5),
        'ln1_g': 1.0 + nrm(ks[10], (Ld, D_MODEL), 0.1),
        'ln1_b': nrm(ks[11], (Ld, D_MODEL), 0.01),
        'router_w': nrm(ks[12], (Ld, D_MODEL, N_EXPERTS), D_MODEL ** -0.5),
        'router_b': nrm(ks[13], (Ld, N_EXPERTS), 0.01),
        'exp_w_gu': nrm(ks[14], (Ld, N_EXPERTS, D_MODEL, 2 * D_FF), D_MODEL ** -0.5),
        'exp_b_gu': nrm(ks[15], (Ld, N_EXPERTS, 2 * D_FF), 0.01),
        'exp_w_down': nrm(ks[16], (Ld, N_EXPERTS, D_FF, D_MODEL), DN_BETA * D_FF ** -0.5),
        'exp_b_down': nrm(ks[17], (Ld, N_EXPERTS, D_MODEL), 0.01),
        'ple_w_gate': nrm(ks[18], (Ld, D_MODEL, D_MODEL), D_MODEL ** -0.5),
        'ple_w_proj': nrm(ks[19], (Ld, PLE_DIM, D_MODEL), DN_BETA * PLE_DIM ** -0.5),
        'ln2_g': 1.0 + nrm(ks[20], (Ld, D_MODEL), 0.1),
        'ln2_b': nrm(ks[21], (Ld, D_MODEL), 0.01),
    }


def reference(x, p, ln0_g, ln0_b, w_in, pool_w, pool_scale, w_up_attn, w_up_pool, w_out,
              ln1_g, ln1_b, router_w, router_b, exp_w_gu, exp_b_gu, exp_w_down, exp_b_down,
              ple_w_gate, ple_w_proj, ln2_g, ln2_b):
    L = x.shape[1]
    cos, sin = _rope_tables(L)
    x = _layer_norm(x, ln0_g, ln0_b)
    for i in range(DEPTH):
        mix = _mixer(x, cos, sin, w_in[i], pool_w[i], pool_scale[i], w_up_attn[i], w_up_pool[i], w_out[i])
        x = _layer_norm(DN_ALPHA * x + mix, ln1_g[i], ln1_b[i])
        ffn = _moe(x, router_w[i], router_b[i], exp_w_gu[i], exp_b_gu[i], exp_w_down[i], exp_b_down[i])
        ple = jax.nn.sigmoid(x @ ple_w_gate[i]) * (p[i].astype(x.dtype) @ ple_w_proj[i])
        x = _layer_norm(DN_ALPHA * x + ffn + ple, ln2_g[i], ln2_b[i])
    return x
```

```python
import functools

import jax
import jax.numpy as jnp
import numpy as np
from jax import lax
from jax.experimental import pallas as pl
from jax.experimental.pallas import tpu as pltpu

MXU_DTYPE = jnp.bfloat16

D_MODEL = 1024
HEAD_DIM = 64
N_HEADS = 8
N_KV_HEADS = 2
ROT_DIM = 16
ROPE_THETA = 500000.0
IDX_HEADS = 8
IDX_DIM = 64
MAX_TOPK = 256
POOL_CH = 512
POOL_WINDOWS = (2, 4, 8, 16)
POOL_GROUP = 128
POOL_HALO = 16
N_EXPERTS = 32
TOP_K = 4
D_FF = 1024
SWIGLU_LIMIT = 7.0
SWIGLU_ALPHA = 1.702
PLE_DIM = 256
LN_EPS = 1e-5
DEPTH = 2
DN_ALPHA = (2 * DEPTH) ** 0.25

LANES = 128
SUBLANES = 8
ROW_TILES = D_MODEL // LANES
assert ROW_TILES == SUBLANES

OFF_Q, OFF_K, OFF_V, OFF_QI, OFF_KIW, OFF_U, W1_COLS = 0, 512, 640, 768, 1280, 1408, 1920

VMEM_LIMIT = 56 * 1024 * 1024
NEG_INF = float("-inf")
INT_MIN = -2 ** 31


def _cparams(n_axes):
    return pltpu.CompilerParams(dimension_semantics=("arbitrary",) * n_axes,
                                vmem_limit_bytes=VMEM_LIMIT)


def _mm(a, b):
    return jnp.dot(a.astype(MXU_DTYPE), b.astype(MXU_DTYPE), preferred_element_type=jnp.float32)


def _mm_nt(a, b):
    return lax.dot_general(a.astype(MXU_DTYPE), b.astype(MXU_DTYPE), (((1,), (1,)), ((), ())),
                           preferred_element_type=jnp.float32)


def _layer_norm(h, g, b):
    mu = jnp.mean(h, axis=-1, keepdims=True)
    c = h - mu
    var = jnp.mean(c * c, axis=-1, keepdims=True)
    return c * lax.rsqrt(var + LN_EPS) * g + b


def _ln0_kernel(x_ref, g_ref, b_ref, o_ref):
    o_ref[...] = _layer_norm(x_ref[...], g_ref[...], b_ref[...])


def _ln0(x, g, b, tm=512):
    n = x.shape[0]
    return pl.pallas_call(
        _ln0_kernel,
        grid=(n // tm,),
        in_specs=[pl.BlockSpec((tm, D_MODEL), lambda i: (i, 0)),
                  pl.BlockSpec((1, D_MODEL), lambda i: (0, 0)),
                  pl.BlockSpec((1, D_MODEL), lambda i: (0, 0))],
        out_specs=pl.BlockSpec((tm, D_MODEL), lambda i: (i, 0)),
        out_shape=jax.ShapeDtypeStruct((n, D_MODEL), jnp.float32),
        compiler_params=_cparams(1),
        name="ln0",
    )(x, g.reshape(1, -1), b.reshape(1, -1))


def _rope_tables(seq):
    pos = jnp.arange(seq, dtype=jnp.float32)
    inv = ROPE_THETA ** (-jnp.arange(0, ROT_DIM, 2, dtype=jnp.float32) / ROT_DIM)
    ang = pos[:, None] * inv[None, :]
    cos, sin = jnp.cos(ang), jnp.sin(ang)
    half = ROT_DIM // 2
    one = jnp.ones((seq, HEAD_DIM - ROT_DIM), jnp.float32)
    zero = jnp.zeros((seq, HEAD_DIM - ROT_DIM), jnp.float32)
    zh = jnp.zeros((seq, half), jnp.float32)
    c64 = jnp.concatenate([cos, cos, one], axis=1)
    s1_64 = jnp.concatenate([-sin, zh, zero], axis=1)
    s2_64 = jnp.concatenate([zh, sin, zero], axis=1)
    ident_c = jnp.ones((seq, HEAD_DIM), jnp.float32)
    ident_s = jnp.zeros((seq, HEAD_DIM), jnp.float32)
    c = jnp.concatenate([c64, c64, c64, ident_c], axis=1)
    s1 = jnp.concatenate([s1_64, s1_64, s1_64, ident_s], axis=1)
    s2 = jnp.concatenate([s2_64, s2_64, s2_64, ident_s], axis=1)
    return c, s1, s2


def _rope_tile(x, c, s1, s2):
    half = ROT_DIM // 2
    return x * c + pltpu.roll(x, LANES - half, axis=1) * s1 + pltpu.roll(x, half, axis=1) * s2


def _inproj_kernel(x_ref, w_ref, c_ref, s1_ref, s2_ref,
                   q_ref, k_ref, v_ref, qi_ref, ki_ref, wi_ref, u_ref):
    xb = x_ref[...].astype(MXU_DTYPE)
    c, s1, s2 = c_ref[:, :LANES], s1_ref[:, :LANES], s2_ref[:, :LANES]
    ck, s1k, s2k = c_ref[:, LANES:], s1_ref[:, LANES:], s2_ref[:, LANES:]

    def seg(off, width):
        return jnp.dot(xb, w_ref[:, off:off + width], preferred_element_type=jnp.float32)

    def roped(z, scale, out_ref):
        for j in range(z.shape[1] // LANES):
            t = _rope_tile(z[:, j * LANES:(j + 1) * LANES], c, s1, s2)
            out_ref[:, j * LANES:(j + 1) * LANES] = (t * scale).astype(out_ref.dtype)

    roped(seg(OFF_Q, 512), HEAD_DIM ** -0.5, q_ref)
    roped(seg(OFF_K, 128), 1.0, k_ref)
    v_ref[...] = seg(OFF_V, 128).astype(v_ref.dtype)
    roped(seg(OFF_QI, 512), IDX_DIM ** -0.5, qi_ref)
    kiw = seg(OFF_KIW, 128)
    ki_ref[...] = _rope_tile(kiw, ck, s1k, s2k).astype(ki_ref.dtype)
    wi_ref[...] = kiw * (IDX_HEADS ** -0.5)
    u_ref[...] = seg(OFF_U, 512)


def _inproj(x, w1, tables, seq, tm=512):
    n = x.shape[0]
    nseq = seq // tm
    c, s1, s2 = tables
    tab_spec = pl.BlockSpec((tm, 2 * LANES), lambda i: (i % nseq, 0))

    def out(width, dtype):
        return pl.BlockSpec((tm, width), lambda i: (i, 0)), jax.ShapeDtypeStruct((n, width), dtype)

    outs = [out(512, MXU_DTYPE), out(128, MXU_DTYPE), out(128, MXU_DTYPE), out(512, MXU_DTYPE),
            out(128, MXU_DTYPE), out(128, jnp.float32), out(512, jnp.float32)]
    return pl.pallas_call(
        _inproj_kernel,
        grid=(n // tm,),
        in_specs=[pl.BlockSpec((tm, D_MODEL), lambda i: (i, 0)),
                  pl.BlockSpec((D_MODEL, W1_COLS), lambda i: (0, 0)),
                  tab_spec, tab_spec, tab_spec],
        out_specs=[o[0] for o in outs],
        out_shape=[o[1] for o in outs],
        compiler_params=_cparams(1),
        name="inproj",
    )(x, w1, c, s1, s2)


def _sortable_key(s):
    k = lax.bitcast_convert_type(s, jnp.int32)
    return k ^ ((k >> 31) & jnp.int32(0x7FFFFFFF))


def _attn_kernel(qi_ref, wi_ref, q_ref, ki_ref, k_ref, v_ref, tri_ref, o_ref,
                 key_ref, thr_ref, need_ref, m_ref, l_ref, acc_ref, *, tq, kc, rb, n_sel):
    j = pl.program_id(1)
    q0 = j * tq
    nch = (q0 + tq + kc - 1) // kc
    qpos = q0 + lax.broadcasted_iota(jnp.int32, (tq, kc), 0)
    kcol = lax.broadcasted_iota(jnp.int32, (tq, kc), 1)

    wi = wi_ref[...]

    def score_chunk(c, carry):
        ks = pl.multiple_of(c * kc, kc)
        kic = ki_ref[pl.ds(ks, kc), :][:, :IDX_DIM]
        acc = jnp.zeros((tq, kc), jnp.float32)
        for h in range(IDX_HEADS):
            d = _mm_nt(qi_ref[:, h * IDX_DIM:(h + 1) * IDX_DIM], kic)
            acc = acc + jnp.maximum(d, 0.0) * wi[:, IDX_DIM + h:IDX_DIM + h + 1]
        s = jnp.where(kcol + ks <= qpos, acc, NEG_INF)
        key_ref[c] = _sortable_key(s)
        return carry

    lax.fori_loop(0, nch, score_chunk, 0)

    def count_ge(r0, cand):
        def body(c, cnt):
            hit = jnp.where(key_ref[c, pl.ds(r0, rb), :] >= cand, 1, 0)
            part = hit[:, 0:LANES]
            for t in range(1, kc // LANES):
                part = part + hit[:, t * LANES:(t + 1) * LANES]
            return cnt + part
        cnt = lax.fori_loop(0, nch, body, jnp.zeros((rb, LANES), jnp.int32))
        return jnp.sum(cnt, axis=1, keepdims=True)

    for r in range(tq // rb):
        r0 = r * rb

        def bit_step(b, thr):
            cand = thr + lax.shift_left(jnp.int32(1), 31 - b)
            return jnp.where(count_ge(r0, cand) >= n_sel, cand, thr)

        thr = lax.fori_loop(0, 32, bit_step, jnp.full((rb, 1), INT_MIN, jnp.int32))
        n_gt = count_ge(r0, thr + 1)
        thr_ref[pl.ds(r0, rb), :] = thr
        need_ref[pl.ds(r0, rb), :] = (n_sel - n_gt).astype(jnp.float32)

    m_ref[...] = jnp.full(m_ref.shape, -1e30, jnp.float32)
    l_ref[...] = jnp.zeros(l_ref.shape, jnp.float32)
    acc_ref[...] = jnp.zeros(acc_ref.shape, jnp.float32)
    thr = thr_ref[...]
    need = need_ref[...]
    group = N_HEADS // N_KV_HEADS

    def attend_chunk(c, n_eq_before):
        ks = pl.multiple_of(c * kc, kc)
        key = key_ref[c]
        eq = key == thr
        rank = _mm(jnp.where(eq, 1.0, 0.0), tri_ref[...]) + n_eq_before
        sel = ((key > thr) | (eq & (rank <= need))) & (kcol + ks <= qpos)
        bias = jnp.where(sel, 0.0, NEG_INF)
        kch = k_ref[pl.ds(ks, kc), :]
        vch = v_ref[pl.ds(ks, kc), :]
        for h in range(N_HEADS):
            g = h // group
            s = _mm_nt(q_ref[:, h * HEAD_DIM:(h + 1) * HEAD_DIM],
                       kch[:, g * HEAD_DIM:(g + 1) * HEAD_DIM]) + bias
            m_old = m_ref[h]
            m_new = jnp.maximum(m_old, jnp.max(s, axis=1, keepdims=True))
            p = jnp.exp(s - m_new)
            alpha = jnp.exp(m_old - m_new)
            l_ref[h] = alpha * l_ref[h] + jnp.sum(p, axis=1, keepdims=True)
            acc_ref[h] = alpha * acc_ref[h] + _mm(p, vch[:, g * HEAD_DIM:(g + 1) * HEAD_DIM])
            m_ref[h] = m_new
        return rank[:, kc - 1:kc]

    lax.fori_loop(0, nch, attend_chunk, jnp.zeros((tq, 1), jnp.float32))

    for h in range(N_HEADS):
        o_ref[:, h * HEAD_DIM:(h + 1) * HEAD_DIM] = (acc_ref[h] / l_ref[h]).astype(o_ref.dtype)


def _attention(qi, wi, q, ki, k, v, batch, seq, tq=256, kc=512, rb=64):
    n = q.shape[0]
    n_sel = min(MAX_TOPK, seq // 4)
    nq = seq // tq
    assert seq % tq == 0 and seq % kc == 0 and kc >= n_sel and tq % rb == 0
    tri = (np.arange(kc)[:, None] <= np.arange(kc)[None, :]).astype(np.float32)
    tri = jnp.asarray(tri, MXU_DTYPE)
    qblk = lambda w: pl.BlockSpec((tq, w), lambda b, j: (b * nq + j, 0))
    seqblk = pl.BlockSpec((seq, LANES), lambda b, j: (b, 0))
    return pl.pallas_call(
        functools.partial(_attn_kernel, tq=tq, kc=kc, rb=rb, n_sel=n_sel),
        grid=(batch, nq),
        in_specs=[qblk(512), qblk(LANES), qblk(512), seqblk, seqblk, seqblk,
                  pl.BlockSpec((kc, kc), lambda b, j: (0, 0))],
        out_specs=qblk(512),
        out_shape=jax.ShapeDtypeStruct((n, N_HEADS * HEAD_DIM), MXU_DTYPE),
        scratch_shapes=[pltpu.VMEM((seq // kc, tq, kc), jnp.int32),
                        pltpu.VMEM((tq, 1), jnp.int32),
                        pltpu.VMEM((tq, 1), jnp.float32),
                        pltpu.VMEM((N_HEADS, tq, 1), jnp.float32),
                        pltpu.VMEM((N_HEADS, tq, 1), jnp.float32),
                        pltpu.VMEM((N_HEADS, tq, HEAD_DIM), jnp.float32)],
        compiler_params=_cparams(2),
        name="dsa_attention",
    )(qi, wi, q, ki, k, v, tri)


def _merge_kernel(x_ref, ya_ref, u_ref, uh_ref, wg_ref, wua_ref, pw_ref, ps_ref, wup_ref, wo_ref,
                  g_ref, b_ref, rw_ref, rb_ref,
                  x1_ref, idx_ref, gate_ref, ext_ref, *, tm, seq):
    i = pl.program_id(0)
    x = x_ref[...]
    xb = x.astype(MXU_DTYPE)

    first = (i % (seq // tm)) == 0
    ext_ref[0:POOL_HALO, :] = jnp.where(first, 0.0, uh_ref[...])
    ext_ref[POOL_HALO:, :] = u_ref[...]
    pos1 = ((i % (seq // tm)) * tm + 1 + lax.broadcasted_iota(jnp.int32, (tm, 1), 0)).astype(jnp.float32)
    parts = []
    for g, win in enumerate(POOL_WINDOWS):
        e = ext_ref[:, g * POOL_GROUP:(g + 1) * POOL_GROUP]
        w = 1
        while w < win:
            e = e + pltpu.roll(e, w, axis=0)
            w *= 2
        tok = e[POOL_HALO:, :]
        ug = u_ref[:, g * POOL_GROUP:(g + 1) * POOL_GROUP]
        d = tok / jnp.minimum(pos1, float(win)) - ug
        parts.append(_mm(d, pw_ref[g]))
    y_pool = jnp.concatenate(parts, axis=1) * ps_ref[...]

    gates = jnp.dot(xb, wg_ref[...], preferred_element_type=jnp.float32)
    merged = (jax.nn.sigmoid(gates[:, :D_MODEL]) * _mm(ya_ref[...], wua_ref[...])
              + jax.nn.sigmoid(gates[:, D_MODEL:]) * _mm(y_pool, wup_ref[...]))
    mix = _mm(merged, wo_ref[...])
    x1 = _layer_norm(DN_ALPHA * x + mix, g_ref[...], b_ref[...])
    x1_ref[...] = x1

    logits = _mm(x1, rw_ref[...]) + rb_ref[...]
    lane = lax.broadcasted_iota(jnp.int32, logits.shape, 1)
    work = jnp.where(lane < N_EXPERTS, logits, NEG_INF)
    vals, idxs = [], []
    for _ in range(TOP_K):
        m = jnp.max(work, axis=1, keepdims=True)
        ix = jnp.min(jnp.where(work == m, lane, LANES), axis=1, keepdims=True)
        vals.append(m)
        idxs.append(ix)
        work = jnp.where(lane == ix, NEG_INF, work)
    es = [jnp.exp(vv - vals[0]) for vv in vals]
    den = es[0] + es[1] + es[2] + es[3]
    idx_out = jnp.zeros(logits.shape, jnp.int32)
    gate_out = jnp.zeros(logits.shape, jnp.float32)
    for kk in range(TOP_K):
        idx_out = jnp.where(lane == kk, idxs[kk], idx_out)
        gate_out = jnp.where(lane == kk, es[kk] / den, gate_out)
    idx_ref[...] = idx_out
    gate_ref[...] = gate_out


def _merge(x, y_attn, u, wg, wua, pw, ps, wup, wo, g, b, rw, rb, seq, tm=256):
    n = x.shape[0]
    hb = tm // POOL_HALO
    full = lambda shape: pl.BlockSpec(shape, lambda i: (0,) * len(shape))
    row = lambda w: pl.BlockSpec((tm, w), lambda i: (i, 0))
    return pl.pallas_call(
        functools.partial(_merge_kernel, tm=tm, seq=seq),
        grid=(n // tm,),
        in_specs=[row(D_MODEL), row(512), row(POOL_CH),
                  pl.BlockSpec((POOL_HALO, POOL_CH), lambda i: (jnp.maximum(i * hb - 1, 0), 0)),
                  full((D_MODEL, 2 * D_MODEL)), full((512, D_MODEL)),
                  full((4, POOL_GROUP, POOL_GROUP)), full((1, POOL_CH)), full((POOL_CH, D_MODEL)),
                  full((D_MODEL, D_MODEL)), full((1, D_MODEL)), full((1, D_MODEL)),
                  full((D_MODEL, LANES)), full((1, LANES))],
        out_specs=[row(D_MODEL), row(LANES), row(LANES)],
        out_shape=[jax.ShapeDtypeStruct((n, D_MODEL), jnp.float32),
                   jax.ShapeDtypeStruct((n, LANES), jnp.int32),
                   jax.ShapeDtypeStruct((n, LANES), jnp.float32)],
        scratch_shapes=[pltpu.VMEM((tm + POOL_HALO, POOL_CH), jnp.float32)],
        compiler_params=_cparams(1),
        name="merge_ln1_router",
    )(x, y_attn, u, u, wg, wua, pw, ps, wup, wo, g, b, rw, rb)


def _gather_rows(idx_ref, n_rows, table_hbm, buf_ref, sem):
    def issue(r, carry):
        t = idx_ref[0, 0, r]
        pltpu.make_async_copy(table_hbm.at[pl.ds(pl.multiple_of(t * ROW_TILES, ROW_TILES), ROW_TILES), :],
                              buf_ref.at[pl.ds(pl.multiple_of(r * ROW_TILES, ROW_TILES), ROW_TILES), :],
                              sem).start()
        return carry
    lax.fori_loop(0, n_rows, issue, 0)


def _wait_rows(n_rows, table_hbm, buf_ref, sem):
    def wait(r, carry):
        pltpu.make_async_copy(table_hbm.at[pl.ds(0, ROW_TILES), :],
                              buf_ref.at[pl.ds(pl.multiple_of(r * ROW_TILES, ROW_TILES), ROW_TILES), :],
                              sem).wait()
        return carry
    lax.fori_loop(0, n_rows, wait, 0)


def _rows_as_matrix(buf_ref, r0, n_rows):
    return jnp.concatenate(
        [buf_ref[pl.ds(r0 * ROW_TILES + c, n_rows, stride=ROW_TILES), :] for c in range(ROW_TILES)], axis=1)


def _expert_kernel(be_ref, nu_ref, tok_ref, x_hbm, wgu_ref, bgu_ref, wd_ref, bd_ref, y_ref,
                   xbuf, wgu_bf, wd_bf, sem, *, tb):
    b = pl.program_id(0)
    used = b < nu_ref[0]

    @pl.when(used)
    def _():
        _gather_rows(tok_ref, tb, x_hbm, xbuf, sem)
        e = be_ref[b]
        e_prev = be_ref[jnp.maximum(b - 1, 0)]

        @pl.when((b == 0) | (e != e_prev))
        def _():
            wgu_bf[...] = wgu_ref[...].astype(MXU_DTYPE)
            wd_bf[...] = wd_ref[...].astype(MXU_DTYPE)

        _wait_rows(tb, x_hbm, xbuf, sem)
        xr = _rows_as_matrix(xbuf, 0, tb)
        gu = _mm(xr, wgu_bf[...]) + bgu_ref[...]
        gt = jnp.minimum(gu[:, :D_FF], SWIGLU_LIMIT)
        up = jnp.clip(gu[:, D_FF:], -SWIGLU_LIMIT, SWIGLU_LIMIT)
        act = gt * jax.nn.sigmoid(SWIGLU_ALPHA * gt) * (up + 1.0)
        y = _mm(act, wd_bf[...]) + bd_ref[...]
        for c in range(ROW_TILES):
            y_ref[pl.ds(c, tb, stride=ROW_TILES), :] = y[:, c * LANES:(c + 1) * LANES]

    @pl.when(jnp.logical_not(used))
    def _():
        y_ref[...] = jnp.zeros(y_ref.shape, y_ref.dtype)


def _experts(blk_exp, n_used, row_tok, x1, w_gu, b_gu, w_down, b_down, tb):
    n_blocks = blk_exp.shape[0]
    n = x1.shape[0]
    wspec = lambda r, c: pl.BlockSpec((None, r, c), lambda b, be, nu: (be[b], 0, 0))
    return pl.pallas_call(
        functools.partial(_expert_kernel, tb=tb),
        grid_spec=pltpu.PrefetchScalarGridSpec(
            num_scalar_prefetch=2,
            grid=(n_blocks,),
            in_specs=[pl.BlockSpec((1, 1, tb), lambda b, be, nu: (b, 0, 0), memory_space=pltpu.SMEM),
                      pl.BlockSpec(memory_space=pl.ANY),
                      wspec(D_MODEL, 2 * D_FF), wspec(1, 2 * D_FF), wspec(D_FF, D_MODEL), wspec(1, D_MODEL)],
            out_specs=pl.BlockSpec((tb * ROW_TILES, LANES), lambda b, be, nu: (b, 0)),
            scratch_shapes=[pltpu.VMEM((tb * ROW_TILES, LANES), jnp.float32),
                            pltpu.VMEM((D_MODEL, 2 * D_FF), MXU_DTYPE),
                            pltpu.VMEM((D_FF, D_MODEL), MXU_DTYPE),
                            pltpu.SemaphoreType.DMA(())]),
        out_shape=jax.ShapeDtypeStruct((n_blocks * tb * ROW_TILES, LANES), jnp.float32),
        compiler_params=_cparams(1),
        name="moe_experts",
    )(blk_exp, n_used, row_tok.reshape(n_blocks, 1, tb), x1.reshape(n * ROW_TILES, LANES),
      w_gu, b_gu.reshape(N_EXPERTS, 1, -1), w_down, b_down.reshape(N_EXPERTS, 1, -1))


def _combine_kernel(dest_ref, y_hbm, gate_ref, x1_ref, p_ref, wpg_ref, wpp_ref, g_ref, b_ref, o_ref,
                    ybuf, sem, *, tm):
    _gather_rows(dest_ref, TOP_K * tm, y_hbm, ybuf, sem)
    x1 = x1_ref[...]
    ple = jax.nn.sigmoid(_mm(x1, wpg_ref[...])) * _mm(p_ref[...], wpp_ref[...])
    h = DN_ALPHA * x1 + ple
    _wait_rows(TOP_K * tm, y_hbm, ybuf, sem)
    gate = gate_ref[...]
    ffn = jnp.zeros((tm, D_MODEL), jnp.float32)
    for kk in range(TOP_K):
        ffn = ffn + _rows_as_matrix(ybuf, kk * tm, tm) * gate[:, kk:kk + 1]
    o_ref[...] = _layer_norm(h + ffn, g_ref[...], b_ref[...])


def _combine(dest, yr, gates, x1, p, wpg, wpp, g, b, tm=256):
    n = x1.shape[0]
    full = lambda shape: pl.BlockSpec(shape, lambda i: (0,) * len(shape))
    row = lambda w: pl.BlockSpec((tm, w), lambda i: (i, 0))
    dest_t = dest.reshape(n // tm, tm, TOP_K).transpose(0, 2, 1).reshape(n // tm, 1, TOP_K * tm)
    return pl.pallas_call(
        functools.partial(_combine_kernel, tm=tm),
        grid=(n // tm,),
        in_specs=[pl.BlockSpec((1, 1, TOP_K * tm), lambda i: (i, 0, 0), memory_space=pltpu.SMEM),
                  pl.BlockSpec(memory_space=pl.ANY),
                  row(LANES), row(D_MODEL), row(PLE_DIM),
                  full((D_MODEL, D_MODEL)), full((PLE_DIM, D_MODEL)),
                  full((1, D_MODEL)), full((1, D_MODEL))],
        out_specs=row(D_MODEL),
        out_shape=jax.ShapeDtypeStruct((n, D_MODEL), jnp.float32),
        scratch_shapes=[pltpu.VMEM((TOP_K * tm * ROW_TILES, LANES), jnp.float32),
                        pltpu.SemaphoreType.DMA(())],
        compiler_params=_cparams(1),
        name="combine_ple_ln2",
    )(dest_t, yr, gates, x1, p, wpg, wpp, g, b)


def _route(top_idx, n_tokens, tb):
    a = n_tokens * TOP_K
    n_blocks = a // tb + N_EXPERTS
    onehot = (top_idx[:, :, None] == jnp.arange(N_EXPERTS, dtype=jnp.int32)[None, None, :]).astype(jnp.int32)
    member = onehot.sum(axis=1)
    rank = jnp.cumsum(member, axis=0) - member
    counts = member.sum(axis=0)
    padded = (counts + tb - 1) // tb * tb
    pend = jnp.cumsum(padded)
    pstart = pend - padded
    dest = jnp.take_along_axis(rank + pstart[None, :], top_idx, axis=1).astype(jnp.int32)
    tok = jnp.broadcast_to(jnp.arange(n_tokens, dtype=jnp.int32)[:, None], (n_tokens, TOP_K))
    row_tok = jnp.zeros((n_blocks * tb,), jnp.int32).at[dest.reshape(-1)].set(tok.reshape(-1))
    blk_start = jnp.arange(n_blocks, dtype=jnp.int32) * tb
    blk_exp = jnp.minimum(jnp.searchsorted(pend, blk_start, side="right"), N_EXPERTS - 1).astype(jnp.int32)
    n_used = (pend[-1] // tb).astype(jnp.int32).reshape(1)
    return dest, row_tok, blk_exp, n_used


def kernel(x, p, ln0_g, ln0_b, w_in, pool_w, pool_scale, w_up_attn, w_up_pool, w_out, ln1_g, ln1_b,
           router_w, router_b, exp_w_gu, exp_b_gu, exp_w_down, exp_b_down, ple_w_gate, ple_w_proj,
           ln2_g, ln2_b):
    batch, seq, d = x.shape
    assert d == D_MODEL
    n = batch * seq
    tb = 256
    tables = _rope_tables(seq)
    bf = lambda a: a.astype(MXU_DTYPE)
    vec = lambda a: a.reshape(1, -1)

    h = _ln0(x.reshape(n, d), ln0_g, ln0_b)
    for i in range(DEPTH):
        w = w_in[i]
        w1 = bf(jnp.concatenate([w[:, :1352], jnp.zeros((d, OFF_U - 1352), w.dtype), w[:, 1352:1864]], axis=1))
        wg = bf(w[:, 1864:])
        q, k, v, qi, ki, wi, u = _inproj(h, w1, tables, seq)
        y_attn = _attention(qi, wi, q, ki, k, v, batch, seq)
        rw = bf(jnp.pad(router_w[i], ((0, 0), (0, LANES - N_EXPERTS))))
        rb = jnp.pad(router_b[i], (0, LANES - N_EXPERTS)).reshape(1, -1)
        x1, idx_l, gate_l = _merge(h, y_attn, u, wg, bf(w_up_attn[i]), bf(pool_w[i]), vec(pool_scale[i]),
                                   bf(w_up_pool[i]), bf(w_out[i]), vec(ln1_g[i]), vec(ln1_b[i]), rw, rb, seq)
        dest, row_tok, blk_exp, n_used = _route(idx_l[:, :TOP_K], n, tb)
        yr = _experts(blk_exp, n_used, row_tok, x1, exp_w_gu[i], exp_b_gu[i], exp_w_down[i], exp_b_down[i], tb)
        h = _combine(dest, yr, gate_l, x1, p[i].reshape(n, PLE_DIM), bf(ple_w_gate[i]), bf(ple_w_proj[i]),
                     vec(ln2_g[i]), vec(ln2_b[i]))
    return h.reshape(batch, seq, d)
```

```python
import functools

import jax
import jax.numpy as jnp
import numpy as np
from jax import lax
from jax.experimental import pallas as pl
from jax.experimental.pallas import tpu as pltpu

MXU_DTYPE = jnp.bfloat16

D_MODEL = 1024
HEAD_DIM = 64
N_HEADS = 8
N_KV_HEADS = 2
ROT_DIM = 16
ROPE_THETA = 500000.0
IDX_HEADS = 8
IDX_DIM = 64
MAX_TOPK = 256
POOL_CH = 512
POOL_WINDOWS = (2, 4, 8, 16)
POOL_GROUP = 128
POOL_HALO = 16
N_EXPERTS = 32
TOP_K = 4
D_FF = 1024
SWIGLU_LIMIT = 7.0
SWIGLU_ALPHA = 1.702
PLE_DIM = 256
LN_EPS = 1e-5
DEPTH = 2
DN_ALPHA = (2 * DEPTH) ** 0.25

LANES = 128
SUBLANES = 8
ROW_TILES = D_MODEL // LANES
assert ROW_TILES == SUBLANES

OFF_Q, OFF_K, OFF_V, OFF_QI, OFF_KIW, OFF_U, W1_COLS = 0, 512, 640, 768, 1280, 1408, 1920

VMEM_LIMIT = 56 * 1024 * 1024
NEG_INF = float("-inf")
KEY_NEG_INF = -2139095041
SEARCH_CAP = 80
ACC_ROWS = 80

def _cparams(n_axes, flags=None):
    return pltpu.CompilerParams(dimension_semantics=("arbitrary",) * n_axes,
                                vmem_limit_bytes=VMEM_LIMIT, flags=flags)


def _mm(a, b):
    return jnp.dot(a.astype(MXU_DTYPE), b.astype(MXU_DTYPE), preferred_element_type=jnp.float32)


def _mm_nt(a, b):
    return lax.dot_general(a.astype(MXU_DTYPE), b.astype(MXU_DTYPE), (((1,), (1,)), ((), ())),
                           preferred_element_type=jnp.float32)


def _layer_norm(h, g, b):
    mu = jnp.mean(h, axis=-1, keepdims=True)
    c = h - mu
    var = jnp.mean(c * c, axis=-1, keepdims=True)
    return c * lax.rsqrt(var + LN_EPS) * g + b


def _ln0_kernel(x_ref, g_ref, b_ref, o_ref):
    o_ref[...] = _layer_norm(x_ref[...], g_ref[...], b_ref[...])


def _ln0(x, g, b, tm=512):
    n = x.shape[0]
    return pl.pallas_call(
        _ln0_kernel,
        grid=(n // tm,),
        in_specs=[pl.BlockSpec((tm, D_MODEL), lambda i: (i, 0)),
                  pl.BlockSpec((1, D_MODEL), lambda i: (0, 0)),
                  pl.BlockSpec((1, D_MODEL), lambda i: (0, 0))],
        out_specs=pl.BlockSpec((tm, D_MODEL), lambda i: (i, 0)),
        out_shape=jax.ShapeDtypeStruct((n, D_MODEL), jnp.float32),
        compiler_params=_cparams(1),
        name="ln0",
    )(x, g.reshape(1, -1), b.reshape(1, -1))


def _rope_tables(seq):
    pos = jnp.arange(seq, dtype=jnp.float32)
    inv = ROPE_THETA ** (-jnp.arange(0, ROT_DIM, 2, dtype=jnp.float32) / ROT_DIM)
    ang = pos[:, None] * inv[None, :]
    cos, sin = jnp.cos(ang), jnp.sin(ang)
    half = ROT_DIM // 2
    one = jnp.ones((seq, HEAD_DIM - ROT_DIM), jnp.float32)
    zero = jnp.zeros((seq, HEAD_DIM - ROT_DIM), jnp.float32)
    zh = jnp.zeros((seq, half), jnp.float32)
    c64 = jnp.concatenate([cos, cos, one], axis=1)
    s1_64 = jnp.concatenate([-sin, zh, zero], axis=1)
    s2_64 = jnp.concatenate([zh, sin, zero], axis=1)
    ident_c = jnp.ones((seq, HEAD_DIM), jnp.float32)
    ident_s = jnp.zeros((seq, HEAD_DIM), jnp.float32)
    c = jnp.concatenate([c64, c64, c64, ident_c], axis=1)
    s1 = jnp.concatenate([s1_64, s1_64, s1_64, ident_s], axis=1)
    s2 = jnp.concatenate([s2_64, s2_64, s2_64, ident_s], axis=1)
    return c, s1, s2


def _rope_tile(x, c, s1, s2):
    half = ROT_DIM // 2
    return x * c + pltpu.roll(x, LANES - half, axis=1) * s1 + pltpu.roll(x, half, axis=1) * s2


def _inproj_kernel(x_ref, w_ref, c_ref, s1_ref, s2_ref,
                   q_ref, k_ref, v_ref, qi_ref, ki_ref, wi_ref, u_ref):
    xb = x_ref[...].astype(MXU_DTYPE)
    c, s1, s2 = c_ref[:, :LANES], s1_ref[:, :LANES], s2_ref[:, :LANES]
    ck, s1k, s2k = c_ref[:, LANES:], s1_ref[:, LANES:], s2_ref[:, LANES:]

    def seg(off, width):
        return jnp.dot(xb, w_ref[:, off:off + width], preferred_element_type=jnp.float32)

    def roped(z, scale, out_ref):
        for j in range(z.shape[1] // LANES):
            t = _rope_tile(z[:, j * LANES:(j + 1) * LANES], c, s1, s2)
            out_ref[:, j * LANES:(j + 1) * LANES] = (t * scale).astype(out_ref.dtype)

    roped(seg(OFF_Q, 512), HEAD_DIM ** -0.5, q_ref)
    roped(seg(OFF_K, 128), 1.0, k_ref)
    v_ref[0] = seg(OFF_V, 128).T.astype(v_ref.dtype)
    roped(seg(OFF_QI, 512), IDX_DIM ** -0.5, qi_ref)
    kiw = seg(OFF_KIW, 128)
    ki_ref[...] = _rope_tile(kiw, ck, s1k, s2k).astype(ki_ref.dtype)
    wi_ref[...] = (kiw * (IDX_HEADS ** -0.5)).T
    u_ref[...] = seg(OFF_U, 512)


def _inproj(x, w1, tables, seq, tm):
    n = x.shape[0]
    nseq = seq // tm
    c, s1, s2 = tables
    tab_spec = pl.BlockSpec((tm, 2 * LANES), lambda i: (i % nseq, 0))

    def out(width, dtype):
        return pl.BlockSpec((tm, width), lambda i: (i, 0)), jax.ShapeDtypeStruct((n, width), dtype)

    v_t = (pl.BlockSpec((1, LANES, tm), lambda i: (i, 0, 0)), jax.ShapeDtypeStruct((n // tm, LANES, tm), MXU_DTYPE))
    wi_t = (pl.BlockSpec((LANES, tm), lambda i: (0, i)), jax.ShapeDtypeStruct((LANES, n), jnp.float32))
    outs = [out(512, MXU_DTYPE), out(128, MXU_DTYPE), v_t, out(512, MXU_DTYPE),
            out(128, MXU_DTYPE), wi_t, out(512, jnp.float32)]
    return pl.pallas_call(
        _inproj_kernel,
        grid=(n // tm,),
        in_specs=[pl.BlockSpec((tm, D_MODEL), lambda i: (i, 0)),
                  pl.BlockSpec((D_MODEL, W1_COLS), lambda i: (0, 0)),
                  tab_spec, tab_spec, tab_spec],
        out_specs=[o[0] for o in outs],
        out_shape=[o[1] for o in outs],
        compiler_params=_cparams(1),
        name="inproj",
    )(x, w1, c, s1, s2)


def _sortable_key(s):
    k = lax.bitcast_convert_type(s, jnp.int32)
    return k ^ ((k >> 31) & jnp.int32(0x7FFFFFFF))


def _key_value(k):
    return lax.bitcast_convert_type(k ^ ((k >> 31) & jnp.int32(0x7FFFFFFF)), jnp.float32)


def _attn_kernel(qi_ref, wi_ref, q_ref, ki_ref, k_ref, v_ref, tri_ref, o_ref,
                 key_ref, s_ref, acc_ref, *, tq, kc, n_sel):
    j = pl.program_id(1)
    q0 = j * tq
    nch = (q0 + tq + kc - 1) // kc
    qpos = q0 + lax.broadcasted_iota(jnp.int32, (kc, tq), 1)
    krow = lax.broadcasted_iota(jnp.int32, (kc, tq), 0)
    imax = jnp.iinfo(jnp.int32).max

    def score_chunk(c, carry):
        kmin, kmax = carry
        ks = pl.multiple_of(c * kc, kc)
        kic = ki_ref[pl.ds(ks, kc), :][:, :IDX_DIM]
        acc = jnp.zeros((kc, tq), jnp.float32)
        for h in range(IDX_HEADS):
            d = _mm_nt(kic, qi_ref[:, h * IDX_DIM:(h + 1) * IDX_DIM])
            acc = acc + jnp.maximum(d, 0.0) * wi_ref[IDX_DIM + h:IDX_DIM + h + 1, :]
        causal = krow + ks <= qpos
        key = _sortable_key(jnp.where(causal, acc, NEG_INF))
        key_ref[c] = key
        kmin = jnp.minimum(kmin, jnp.min(jnp.where(causal, key, imax), axis=0, keepdims=True))
        kmax = jnp.maximum(kmax, jnp.max(key, axis=0, keepdims=True))
        return kmin, kmax

    kmin, kmax = lax.fori_loop(0, nch, score_chunk,
                               (jnp.full((1, tq), imax, jnp.int32), jnp.full((1, tq), KEY_NEG_INF, jnp.int32)))

    def count_ge(cand):
        def body(c, cnt):
            hit = jnp.where(key_ref[c] >= cand, 1, 0)
            return cnt + jnp.sum(hit.reshape(kc // SUBLANES, SUBLANES, tq), axis=0)
        cnt = lax.fori_loop(0, nch, body, jnp.zeros((SUBLANES, tq), jnp.int32))
        return jnp.sum(cnt, axis=0, keepdims=True)

    n_valid = q0 + lax.broadcasted_iota(jnp.int32, (1, tq), 1) + 1
    all_selected = n_valid <= n_sel
    state0 = (jnp.int32(0),
              jnp.where(all_selected, KEY_NEG_INF, kmin),
              kmax + 1,
              n_valid, jnp.zeros((1, tq), jnp.int32),
              all_selected.astype(jnp.int32))

    def search_cond(st):
        return (st[0] < SEARCH_CAP) & (jnp.min(st[5]) == 0)

    def search_step(st):
        it, lo, hi, c_lo, c_hi, done = st
        mid = lo + ((hi >> 1) - (lo >> 1))
        frac = ((c_lo - n_sel).astype(jnp.float32) + 0.5) / (c_lo - c_hi).astype(jnp.float32)
        v_lo, v_hi = _key_value(lo), _key_value(hi)
        interp = _sortable_key(v_lo + frac * (v_hi - v_lo))
        cand = jnp.where(it % 2 == 0, interp, mid)
        cand = jnp.minimum(jnp.maximum(cand, lo + 1), hi - 1)
        active = done == 0
        cand = jnp.where(active, cand, lo)
        cnt = count_ge(cand)
        up = active & (cnt >= n_sel)
        dn = active & (cnt < n_sel)
        lo, c_lo = jnp.where(up, cand, lo), jnp.where(up, cnt, c_lo)
        hi, c_hi = jnp.where(dn, cand, hi), jnp.where(dn, cnt, c_hi)
        done = jnp.where((done != 0) | (c_lo == n_sel) | (hi - 1 <= lo), 1, 0)
        return it + 1, lo, hi, c_lo, c_hi, done

    _, thr, _, c_lo, c_hi, _ = lax.while_loop(search_cond, search_step, state0)
    need = jnp.where(c_lo == n_sel, float(2 ** 30), (n_sel - c_hi).astype(jnp.float32))

    def bias_chunk(with_ties, c, n_eq_before):
        ks = pl.multiple_of(c * kc, kc)
        key = key_ref[c]
        causal = krow + ks <= qpos
        if with_ties:
            eq = key == thr
            rank = _mm(tri_ref[...], jnp.where(eq, 1.0, 0.0)) + n_eq_before
            sel = ((key > thr) | (eq & (rank <= need))) & causal
            n_eq_before = rank[kc - 1:kc, :]
        else:
            sel = (key >= thr) & causal
        key_ref[c] = lax.bitcast_convert_type(jnp.where(sel, 0.0, NEG_INF), jnp.int32)
        return n_eq_before

    has_ties = jnp.max(c_lo) > n_sel

    @pl.when(has_ties)
    def _():
        lax.fori_loop(0, nch, functools.partial(bias_chunk, True), jnp.zeros((1, tq), jnp.float32))

    @pl.when(jnp.logical_not(has_ties))
    def _():
        lax.fori_loop(0, nch, functools.partial(bias_chunk, False), jnp.zeros((1, tq), jnp.float32))

    acc_ref[...] = jnp.zeros(acc_ref.shape, jnp.float32)
    group = N_HEADS // N_KV_HEADS
    ones_rows = jnp.ones((ACC_ROWS - HEAD_DIM, kc), MXU_DTYPE)

    def attend_chunk(c, ms):
        ks = pl.multiple_of(c * kc, kc)
        bias = lax.bitcast_convert_type(key_ref[c], jnp.float32)
        kch = k_ref[pl.ds(ks, kc), :]
        vch = v_ref[c]
        vext = [jnp.concatenate([vch[g * HEAD_DIM:(g + 1) * HEAD_DIM, :], ones_rows], axis=0)
                for g in range(N_KV_HEADS)]
        slot0 = jnp.minimum(c, 0)
        for h in range(N_HEADS):
            g = h // group
            s_ref[slot0 + h] = _mm_nt(kch[:, g * HEAD_DIM:(g + 1) * HEAD_DIM],
                                      q_ref[:, h * HEAD_DIM:(h + 1) * HEAD_DIM]) + bias
        new_ms = []
        for h in range(N_HEADS):
            m_new = jnp.maximum(ms[h], jnp.max(s_ref[slot0 + h], axis=0, keepdims=True))
            p = jnp.exp(s_ref[slot0 + h] - m_new)
            acc_ref[h] = jnp.exp(ms[h] - m_new) * acc_ref[h] + _mm(vext[h // group], p)
            new_ms.append(m_new)
        return tuple(new_ms)

    lax.fori_loop(0, nch, attend_chunk, tuple(jnp.full((1, tq), -1e30, jnp.float32) for _ in range(N_HEADS)))

    out_t = jnp.concatenate([acc_ref[h, 0:HEAD_DIM, :] / acc_ref[h, HEAD_DIM:HEAD_DIM + 1, :]
                             for h in range(N_HEADS)], axis=0)
    o_ref[...] = out_t.T.astype(o_ref.dtype)


def _attention(qi, wi_t, q, ki, k, v_t, batch, seq, tq, kc):
    n = q.shape[0]
    n_sel = min(MAX_TOPK, seq // 4)
    nq = seq // tq
    nck = seq // kc
    assert seq % tq == 0 and seq % kc == 0 and kc >= n_sel and v_t.shape == (n // kc, LANES, kc)
    tri = jnp.asarray((np.arange(kc)[None, :] <= np.arange(kc)[:, None]).astype(np.float32), MXU_DTYPE)
    qblk = lambda w: pl.BlockSpec((tq, w), lambda b, j: (b * nq + j, 0))
    seqblk = pl.BlockSpec((seq, LANES), lambda b, j: (b, 0))
    return pl.pallas_call(
        functools.partial(_attn_kernel, tq=tq, kc=kc, n_sel=n_sel),
        grid=(batch, nq),
        in_specs=[qblk(512), pl.BlockSpec((LANES, tq), lambda b, j: (0, b * nq + j)), qblk(512),
                  seqblk, seqblk, pl.BlockSpec((nck, LANES, kc), lambda b, j: (b, 0, 0)),
                  pl.BlockSpec((kc, kc), lambda b, j: (0, 0))],
        out_specs=qblk(512),
        out_shape=jax.ShapeDtypeStruct((n, N_HEADS * HEAD_DIM), MXU_DTYPE),
        scratch_shapes=[pltpu.VMEM((nck, kc, tq), jnp.int32),
                        pltpu.VMEM((N_HEADS, kc, tq), jnp.float32),
                        pltpu.VMEM((N_HEADS, ACC_ROWS, tq), jnp.float32)],
        compiler_params=_cparams(2),
        name="dsa_attention",
    )(qi, wi_t, q, ki, k, v_t, tri)


def _merge_kernel(x_ref, ya_ref, u_ref, uh_ref, wg_ref, wua_ref, pw_ref, ps_ref, wup_ref, wo_ref,
                  g_ref, b_ref, rw_ref, rb_ref,
                  x1_ref, idx_ref, gate_ref, ext_ref, *, tm, seq):
    i = pl.program_id(0)
    x = x_ref[...]
    xb = x.astype(MXU_DTYPE)

    first = (i % (seq // tm)) == 0
    ext_ref[0:POOL_HALO, :] = jnp.where(first, 0.0, uh_ref[...])
    ext_ref[POOL_HALO:, :] = u_ref[...]
    pos1 = ((i % (seq // tm)) * tm + 1 + lax.broadcasted_iota(jnp.int32, (tm, 1), 0)).astype(jnp.float32)
    parts = []
    for g, win in enumerate(POOL_WINDOWS):
        e = ext_ref[:, g * POOL_GROUP:(g + 1) * POOL_GROUP]
        w = 1
        while w < win:
            e = e + pltpu.roll(e, w, axis=0)
            w *= 2
        tok = e[POOL_HALO:, :]
        ug = u_ref[:, g * POOL_GROUP:(g + 1) * POOL_GROUP]
        d = tok / jnp.minimum(pos1, float(win)) - ug
        parts.append(_mm(d, pw_ref[g]))
    y_pool = jnp.concatenate(parts, axis=1) * ps_ref[...]

    gates = jnp.dot(xb, wg_ref[...], preferred_element_type=jnp.float32)
    merged = (jax.nn.sigmoid(gates[:, :D_MODEL]) * _mm(ya_ref[...], wua_ref[...])
              + jax.nn.sigmoid(gates[:, D_MODEL:]) * _mm(y_pool, wup_ref[...]))
    mix = _mm(merged, wo_ref[...])
    x1 = _layer_norm(DN_ALPHA * x + mix, g_ref[...], b_ref[...])
    x1_ref[...] = x1

    logits = _mm(x1, rw_ref[...]) + rb_ref[...]
    lane = lax.broadcasted_iota(jnp.int32, logits.shape, 1)
    work = jnp.where(lane < N_EXPERTS, logits, NEG_INF)
    vals, idxs = [], []
    for _ in range(TOP_K):
        m = jnp.max(work, axis=1, keepdims=True)
        ix = jnp.min(jnp.where(work == m, lane, LANES), axis=1, keepdims=True)
        vals.append(m)
        idxs.append(ix)
        work = jnp.where(lane == ix, NEG_INF, work)
    es = [jnp.exp(vv - vals[0]) for vv in vals]
    den = es[0] + es[1] + es[2] + es[3]
    idx_out = jnp.zeros(logits.shape, jnp.int32)
    gate_out = jnp.zeros(logits.shape, jnp.float32)
    for kk in range(TOP_K):
        idx_out = jnp.where(lane == kk, idxs[kk], idx_out)
        gate_out = jnp.where(lane == kk, es[kk] / den, gate_out)
    idx_ref[...] = idx_out
    gate_ref[...] = gate_out


def _merge(x, y_attn, u, wg, wua, pw, ps, wup, wo, g, b, rw, rb, seq, tm=256):
    n = x.shape[0]
    hb = tm // POOL_HALO
    full = lambda shape: pl.BlockSpec(shape, lambda i: (0,) * len(shape))
    row = lambda w: pl.BlockSpec((tm, w), lambda i: (i, 0))
    return pl.pallas_call(
        functools.partial(_merge_kernel, tm=tm, seq=seq),
        grid=(n // tm,),
        in_specs=[row(D_MODEL), row(512), row(POOL_CH),
                  pl.BlockSpec((POOL_HALO, POOL_CH), lambda i: (jnp.maximum(i * hb - 1, 0), 0)),
                  full((D_MODEL, 2 * D_MODEL)), full((512, D_MODEL)),
                  full((4, POOL_GROUP, POOL_GROUP)), full((1, POOL_CH)), full((POOL_CH, D_MODEL)),
                  full((D_MODEL, D_MODEL)), full((1, D_MODEL)), full((1, D_MODEL)),
                  full((D_MODEL, LANES)), full((1, LANES))],
        out_specs=[row(D_MODEL), row(LANES), row(LANES)],
        out_shape=[jax.ShapeDtypeStruct((n, D_MODEL), jnp.float32),
                   jax.ShapeDtypeStruct((n, LANES), jnp.int32),
                   jax.ShapeDtypeStruct((n, LANES), jnp.float32)],
        scratch_shapes=[pltpu.VMEM((tm + POOL_HALO, POOL_CH), jnp.float32)],
        compiler_params=_cparams(1),
        name="merge_ln1_router",
    )(x, y_attn, u, u, wg, wua, pw, ps, wup, wo, g, b, rw, rb)


def _gather_rows(idx_ref, n_rows, table_hbm, buf_ref, sem):
    def issue(r, carry):
        t = idx_ref[0, 0, r]
        pltpu.make_async_copy(table_hbm.at[pl.ds(pl.multiple_of(t * ROW_TILES, ROW_TILES), ROW_TILES), :],
                              buf_ref.at[pl.ds(pl.multiple_of(r * ROW_TILES, ROW_TILES), ROW_TILES), :],
                              sem).start()
        return carry
    lax.fori_loop(0, n_rows, issue, 0)


def _wait_rows(n_rows, table_hbm, buf_ref, sem):
    def wait(r, carry):
        pltpu.make_async_copy(table_hbm.at[pl.ds(0, ROW_TILES), :],
                              buf_ref.at[pl.ds(pl.multiple_of(r * ROW_TILES, ROW_TILES), ROW_TILES), :],
                              sem).wait()
        return carry
    lax.fori_loop(0, n_rows, wait, 0)


def _rows_as_matrix(buf_ref, r0, n_rows):
    return jnp.concatenate(
        [buf_ref[pl.ds(r0 * ROW_TILES + c, n_rows, stride=ROW_TILES), :] for c in range(ROW_TILES)], axis=1)


def _expert_kernel(be_ref, nu_ref, tok_ref, x_hbm, wgu_ref, bgu_ref, wd_ref, bd_ref, y_ref,
                   xbuf, wgu_bf, wd_bf, sem, *, tb):
    b = pl.program_id(0)
    used = b < nu_ref[0]

    @pl.when(used)
    def _():
        _gather_rows(tok_ref, tb, x_hbm, xbuf, sem)
        e = be_ref[b]
        e_prev = be_ref[jnp.maximum(b - 1, 0)]

        @pl.when((b == 0) | (e != e_prev))
        def _():
            wgu_bf[...] = wgu_ref[...].astype(MXU_DTYPE)
            wd_bf[...] = wd_ref[...].astype(MXU_DTYPE)

        _wait_rows(tb, x_hbm, xbuf, sem)
        xr = _rows_as_matrix(xbuf, 0, tb)
        gu = _mm(xr, wgu_bf[...]) + bgu_ref[...]
        gt = jnp.minimum(gu[:, :D_FF], SWIGLU_LIMIT)
        up = jnp.clip(gu[:, D_FF:], -SWIGLU_LIMIT, SWIGLU_LIMIT)
        act = gt * jax.nn.sigmoid(SWIGLU_ALPHA * gt) * (up + 1.0)
        y = _mm(act, wd_bf[...]) + bd_ref[...]
        for c in range(ROW_TILES):
            y_ref[pl.ds(c, tb, stride=ROW_TILES), :] = y[:, c * LANES:(c + 1) * LANES]

    @pl.when(jnp.logical_not(used))
    def _():
        y_ref[...] = jnp.zeros(y_ref.shape, y_ref.dtype)


def _experts(blk_exp, n_used, row_tok, x1, w_gu, b_gu, w_down, b_down, tb):
    n_blocks = blk_exp.shape[0]
    n = x1.shape[0]
    wspec = lambda r, c: pl.BlockSpec((None, r, c), lambda b, be, nu: (be[b], 0, 0))
    return pl.pallas_call(
        functools.partial(_expert_kernel, tb=tb),
        grid_spec=pltpu.PrefetchScalarGridSpec(
            num_scalar_prefetch=2,
            grid=(n_blocks,),
            in_specs=[pl.BlockSpec((1, 1, tb), lambda b, be, nu: (b, 0, 0), memory_space=pltpu.SMEM),
                      pl.BlockSpec(memory_space=pl.ANY),
                      wspec(D_MODEL, 2 * D_FF), wspec(1, 2 * D_FF), wspec(D_FF, D_MODEL), wspec(1, D_MODEL)],
            out_specs=pl.BlockSpec((tb * ROW_TILES, LANES), lambda b, be, nu: (b, 0)),
            scratch_shapes=[pltpu.VMEM((tb * ROW_TILES, LANES), jnp.float32),
                            pltpu.VMEM((D_MODEL, 2 * D_FF), MXU_DTYPE),
                            pltpu.VMEM((D_FF, D_MODEL), MXU_DTYPE),
                            pltpu.SemaphoreType.DMA(())]),
        out_shape=jax.ShapeDtypeStruct((n_blocks * tb * ROW_TILES, LANES), jnp.float32),
        compiler_params=_cparams(1),
        name="moe_experts",
    )(blk_exp, n_used, row_tok.reshape(n_blocks, 1, tb), x1.reshape(n * ROW_TILES, LANES),
      w_gu, b_gu, w_down, b_down)


def _combine_kernel(dest_ref, y_hbm, gate_ref, x1_ref, p_ref, wpg_ref, wpp_ref, g_ref, b_ref, o_ref,
                    ybuf, sem, *, tm):
    _gather_rows(dest_ref, TOP_K * tm, y_hbm, ybuf, sem)
    x1 = x1_ref[...]
    ple = jax.nn.sigmoid(_mm(x1, wpg_ref[...])) * _mm(p_ref[...], wpp_ref[...])
    h = DN_ALPHA * x1 + ple
    _wait_rows(TOP_K * tm, y_hbm, ybuf, sem)
    gate = gate_ref[...]
    ffn = jnp.zeros((tm, D_MODEL), jnp.float32)
    for kk in range(TOP_K):
        ffn = ffn + _rows_as_matrix(ybuf, kk * tm, tm) * gate[:, kk:kk + 1]
    o_ref[...] = _layer_norm(h + ffn, g_ref[...], b_ref[...])


def _combine(dest, yr, gates, x1, p, wpg, wpp, g, b, tm=256):
    n = x1.shape[0]
    full = lambda shape: pl.BlockSpec(shape, lambda i: (0,) * len(shape))
    row = lambda w: pl.BlockSpec((tm, w), lambda i: (i, 0))
    dest_t = dest.reshape(n // tm, tm, TOP_K).transpose(0, 2, 1).reshape(n // tm, 1, TOP_K * tm)
    return pl.pallas_call(
        functools.partial(_combine_kernel, tm=tm),
        grid=(n // tm,),
        in_specs=[pl.BlockSpec((1, 1, TOP_K * tm), lambda i: (i, 0, 0), memory_space=pltpu.SMEM),
                  pl.BlockSpec(memory_space=pl.ANY),
                  row(LANES), row(D_MODEL), row(PLE_DIM),
                  full((D_MODEL, D_MODEL)), full((PLE_DIM, D_MODEL)),
                  full((1, D_MODEL)), full((1, D_MODEL))],
        out_specs=row(D_MODEL),
        out_shape=jax.ShapeDtypeStruct((n, D_MODEL), jnp.float32),
        scratch_shapes=[pltpu.VMEM((TOP_K * tm * ROW_TILES, LANES), jnp.float32),
                        pltpu.SemaphoreType.DMA(())],
        compiler_params=_cparams(1),
        name="combine_ple_ln2",
    )(dest_t, yr, gates, x1, p, wpg, wpp, g, b)


def _route(top_idx, n_tokens, tb):
    a = n_tokens * TOP_K
    n_blocks = a // tb + N_EXPERTS
    onehot = (top_idx[:, :, None] == jnp.arange(N_EXPERTS, dtype=jnp.int32)[None, None, :]).astype(jnp.int32)
    member = onehot.sum(axis=1)
    rank = jnp.cumsum(member, axis=0) - member
    counts = member.sum(axis=0)
    padded = (counts + tb - 1) // tb * tb
    pend = jnp.cumsum(padded)
    pstart = pend - padded
    dest = jnp.take_along_axis(rank + pstart[None, :], top_idx, axis=1).astype(jnp.int32)
    tok = jnp.broadcast_to(jnp.arange(n_tokens, dtype=jnp.int32)[:, None], (n_tokens, TOP_K))
    row_tok = jnp.zeros((n_blocks * tb,), jnp.int32).at[dest.reshape(-1)].set(tok.reshape(-1))
    blk_start = jnp.arange(n_blocks, dtype=jnp.int32) * tb
    blk_exp = jnp.minimum((blk_start[:, None] >= pend[None, :]).astype(jnp.int32).sum(axis=1), N_EXPERTS - 1)
    n_used = (pend[-1] // tb).astype(jnp.int32).reshape(1)
    return dest, row_tok, blk_exp, n_used


def kernel(x, p, ln0_g, ln0_b, w_in, pool_w, pool_scale, w_up_attn, w_up_pool, w_out, ln1_g, ln1_b,
           router_w, router_b, exp_w_gu, exp_b_gu, exp_w_down, exp_b_down, ple_w_gate, ple_w_proj,
           ln2_g, ln2_b):
    batch, seq, d = x.shape
    assert d == D_MODEL
    n = batch * seq
    tb = 256
    kc = 512
    tables = _rope_tables(seq)
    bf = lambda a: a.astype(MXU_DTYPE)
    vec = lambda a: a.reshape(1, -1)
    n_le = DEPTH * N_EXPERTS
    w_gu_all = exp_w_gu.reshape(n_le, D_MODEL, 2 * D_FF)
    b_gu_all = exp_b_gu.reshape(n_le, 1, 2 * D_FF)
    w_down_all = exp_w_down.reshape(n_le, D_FF, D_MODEL)
    b_down_all = exp_b_down.reshape(n_le, 1, D_MODEL)

    h = _ln0(x.reshape(n, d), ln0_g, ln0_b)
    for i in range(DEPTH):
        w = w_in[i]
        w1 = bf(jnp.concatenate([w[:, :1352], jnp.zeros((d, OFF_U - 1352), w.dtype), w[:, 1352:1864]], axis=1))
        wg = bf(w[:, 1864:])
        q, k, v_t, qi, ki, wi_t, u = _inproj(h, w1, tables, seq, kc)
        y_attn = _attention(qi, wi_t, q, ki, k, v_t, batch, seq, 256, kc)
        rw = bf(jnp.pad(router_w[i], ((0, 0), (0, LANES - N_EXPERTS))))
        rb = jnp.pad(router_b[i], (0, LANES - N_EXPERTS)).reshape(1, -1)
        x1, idx_l, gate_l = _merge(h, y_attn, u, wg, bf(w_up_attn[i]), bf(pool_w[i]), vec(pool_scale[i]),
                                   bf(w_up_pool[i]), bf(w_out[i]), vec(ln1_g[i]), vec(ln1_b[i]), rw, rb, seq)
        dest, row_tok, blk_exp, n_used = _route(idx_l[:, :TOP_K], n, tb)
        yr = _experts(blk_exp + i * N_EXPERTS, n_used, row_tok, x1, w_gu_all, b_gu_all, w_down_all, b_down_all, tb)
        h = _combine(dest, yr, gate_l, x1, p[i].reshape(n, PLE_DIM), bf(ple_w_gate[i]), bf(ple_w_proj[i]),
                     vec(ln2_g[i]), vec(ln2_b[i]))
    return h.reshape(batch, seq, d)
```

```python
import functools

import jax
import jax.numpy as jnp
import numpy as np
from jax import lax
from jax.experimental import pallas as pl
from jax.experimental.pallas import tpu as pltpu

MXU_DTYPE = jnp.bfloat16

D_MODEL = 1024
HEAD_DIM = 64
N_HEADS = 8
N_KV_HEADS = 2
ROT_DIM = 16
ROPE_THETA = 500000.0
IDX_HEADS = 8
IDX_DIM = 64
MAX_TOPK = 256
POOL_CH = 512
POOL_WINDOWS = (2, 4, 8, 16)
POOL_GROUP = 128
POOL_HALO = 16
N_EXPERTS = 32
TOP_K = 4
D_FF = 1024
SWIGLU_LIMIT = 7.0
SWIGLU_ALPHA = 1.702
PLE_DIM = 256
LN_EPS = 1e-5
DEPTH = 2
DN_ALPHA = (2 * DEPTH) ** 0.25

LANES = 128
SUBLANES = 8
ROW_TILES = D_MODEL // LANES
assert ROW_TILES == SUBLANES

OFF_Q, OFF_K, OFF_V, OFF_QI, OFF_KIW, OFF_U, W1_COLS = 0, 512, 640, 768, 1280, 1408, 1920

VMEM_LIMIT = 56 * 1024 * 1024
NEG_INF = float("-inf")
KEY_NEG_INF = -2139095041
SEARCH_CAP = 80
ACC_ROWS = 80
GATHER_UNROLL = 8

def _cparams(n_axes, flags=None):
    return pltpu.CompilerParams(dimension_semantics=("arbitrary",) * n_axes,
                                vmem_limit_bytes=VMEM_LIMIT, flags=flags)


def _mm(a, b):
    return jnp.dot(a.astype(MXU_DTYPE), b.astype(MXU_DTYPE), preferred_element_type=jnp.float32)


def _mm_nt(a, b):
    return lax.dot_general(a.astype(MXU_DTYPE), b.astype(MXU_DTYPE), (((1,), (1,)), ((), ())),
                           preferred_element_type=jnp.float32)


def _layer_norm(h, g, b):
    mu = jnp.mean(h, axis=-1, keepdims=True)
    c = h - mu
    var = jnp.mean(c * c, axis=-1, keepdims=True)
    return c * lax.rsqrt(var + LN_EPS) * g + b


def _ln0_kernel(x_ref, g_ref, b_ref, o_ref):
    o_ref[...] = _layer_norm(x_ref[...], g_ref[...], b_ref[...])


def _ln0(x, g, b, tm=512):
    n = x.shape[0]
    return pl.pallas_call(
        _ln0_kernel,
        grid=(n // tm,),
        in_specs=[pl.BlockSpec((tm, D_MODEL), lambda i: (i, 0)),
                  pl.BlockSpec((1, D_MODEL), lambda i: (0, 0)),
                  pl.BlockSpec((1, D_MODEL), lambda i: (0, 0))],
        out_specs=pl.BlockSpec((tm, D_MODEL), lambda i: (i, 0)),
        out_shape=jax.ShapeDtypeStruct((n, D_MODEL), jnp.float32),
        compiler_params=_cparams(1),
        name="ln0",
    )(x, g.reshape(1, -1), b.reshape(1, -1))


def _rope_tables(seq):
    pos = jnp.arange(seq, dtype=jnp.float32)
    inv = ROPE_THETA ** (-jnp.arange(0, ROT_DIM, 2, dtype=jnp.float32) / ROT_DIM)
    ang = pos[:, None] * inv[None, :]
    cos, sin = jnp.cos(ang), jnp.sin(ang)
    half = ROT_DIM // 2
    one = jnp.ones((seq, HEAD_DIM - ROT_DIM), jnp.float32)
    zero = jnp.zeros((seq, HEAD_DIM - ROT_DIM), jnp.float32)
    zh = jnp.zeros((seq, half), jnp.float32)
    c64 = jnp.concatenate([cos, cos, one], axis=1)
    s1_64 = jnp.concatenate([-sin, zh, zero], axis=1)
    s2_64 = jnp.concatenate([zh, sin, zero], axis=1)
    ident_c = jnp.ones((seq, HEAD_DIM), jnp.float32)
    ident_s = jnp.zeros((seq, HEAD_DIM), jnp.float32)
    c = jnp.concatenate([c64, c64, c64, ident_c], axis=1)
    s1 = jnp.concatenate([s1_64, s1_64, s1_64, ident_s], axis=1)
    s2 = jnp.concatenate([s2_64, s2_64, s2_64, ident_s], axis=1)
    return c, s1, s2


def _rope_tile(x, c, s1, s2):
    half = ROT_DIM // 2
    return x * c + pltpu.roll(x, LANES - half, axis=1) * s1 + pltpu.roll(x, half, axis=1) * s2


def _inproj_kernel(x_ref, w_ref, c_ref, s1_ref, s2_ref,
                   q_ref, k_ref, v_ref, qi_ref, ki_ref, wi_ref, u_ref):
    xb = x_ref[...].astype(MXU_DTYPE)
    c, s1, s2 = c_ref[:, :LANES], s1_ref[:, :LANES], s2_ref[:, :LANES]
    ck, s1k, s2k = c_ref[:, LANES:], s1_ref[:, LANES:], s2_ref[:, LANES:]

    def seg(off, width):
        return jnp.dot(xb, w_ref[:, off:off + width], preferred_element_type=jnp.float32)

    def roped(z, scale, out_ref):
        for j in range(z.shape[1] // LANES):
            t = _rope_tile(z[:, j * LANES:(j + 1) * LANES], c, s1, s2)
            out_ref[:, j * LANES:(j + 1) * LANES] = (t * scale).astype(out_ref.dtype)

    roped(seg(OFF_Q, 512), HEAD_DIM ** -0.5, q_ref)
    roped(seg(OFF_K, 128), 1.0, k_ref)
    v_ref[0] = seg(OFF_V, 128).T.astype(v_ref.dtype)
    roped(seg(OFF_QI, 512), IDX_DIM ** -0.5, qi_ref)
    kiw = seg(OFF_KIW, 128)
    ki_ref[...] = _rope_tile(kiw, ck, s1k, s2k).astype(ki_ref.dtype)
    wi_ref[...] = (kiw * (IDX_HEADS ** -0.5)).T
    u_ref[...] = seg(OFF_U, 512)


def _inproj(x, w1, tables, seq, tm):
    n = x.shape[0]
    nseq = seq // tm
    c, s1, s2 = tables
    tab_spec = pl.BlockSpec((tm, 2 * LANES), lambda i: (i % nseq, 0))

    def out(width, dtype):
        return pl.BlockSpec((tm, width), lambda i: (i, 0)), jax.ShapeDtypeStruct((n, width), dtype)

    v_t = (pl.BlockSpec((1, LANES, tm), lambda i: (i, 0, 0)), jax.ShapeDtypeStruct((n // tm, LANES, tm), MXU_DTYPE))
    wi_t = (pl.BlockSpec((LANES, tm), lambda i: (0, i)), jax.ShapeDtypeStruct((LANES, n), jnp.float32))
    outs = [out(512, MXU_DTYPE), out(128, MXU_DTYPE), v_t, out(512, MXU_DTYPE),
            out(128, MXU_DTYPE), wi_t, out(512, jnp.float32)]
    return pl.pallas_call(
        _inproj_kernel,
        grid=(n // tm,),
        in_specs=[pl.BlockSpec((tm, D_MODEL), lambda i: (i, 0)),
                  pl.BlockSpec((D_MODEL, W1_COLS), lambda i: (0, 0)),
                  tab_spec, tab_spec, tab_spec],
        out_specs=[o[0] for o in outs],
        out_shape=[o[1] for o in outs],
        compiler_params=_cparams(1),
        name="inproj",
    )(x, w1, c, s1, s2)


def _sortable_key(s):
    k = lax.bitcast_convert_type(s, jnp.int32)
    return k ^ ((k >> 31) & jnp.int32(0x7FFFFFFF))


def _key_value(k):
    return lax.bitcast_convert_type(k ^ ((k >> 31) & jnp.int32(0x7FFFFFFF)), jnp.float32)


def _attn_kernel(qi_ref, wi_ref, q_ref, ki_ref, k_ref, v_ref, tri_ref, o_ref,
                 key_ref, s_ref, acc_ref, *, tq, kc, n_sel):
    j = pl.program_id(1)
    q0 = j * tq
    nch = (q0 + tq + kc - 1) // kc
    qpos = q0 + lax.broadcasted_iota(jnp.int32, (kc, tq), 1)
    krow = lax.broadcasted_iota(jnp.int32, (kc, tq), 0)
    imax = jnp.iinfo(jnp.int32).max

    def score_chunk(c, carry):
        kmin, kmax = carry
        ks = pl.multiple_of(c * kc, kc)
        kic = ki_ref[pl.ds(ks, kc), :][:, :IDX_DIM]
        acc = jnp.zeros((kc, tq), jnp.float32)
        for h in range(IDX_HEADS):
            d = _mm_nt(kic, qi_ref[:, h * IDX_DIM:(h + 1) * IDX_DIM])
            acc = acc + jnp.maximum(d, 0.0) * wi_ref[IDX_DIM + h:IDX_DIM + h + 1, :]
        causal = krow + ks <= qpos
        key = _sortable_key(jnp.where(causal, acc, NEG_INF))
        key_ref[c] = key
        kmin = jnp.minimum(kmin, jnp.min(jnp.where(causal, key, imax), axis=0, keepdims=True))
        kmax = jnp.maximum(kmax, jnp.max(key, axis=0, keepdims=True))
        return kmin, kmax

    kmin, kmax = lax.fori_loop(0, nch, score_chunk,
                               (jnp.full((1, tq), imax, jnp.int32), jnp.full((1, tq), KEY_NEG_INF, jnp.int32)))

    def count_ge(cand):
        def body(c, cnt):
            hit = jnp.where(key_ref[c] >= cand, 1, 0)
            return cnt + jnp.sum(hit.reshape(kc // SUBLANES, SUBLANES, tq), axis=0)
        cnt = lax.fori_loop(0, nch, body, jnp.zeros((SUBLANES, tq), jnp.int32))
        return jnp.sum(cnt, axis=0, keepdims=True)

    n_valid = q0 + lax.broadcasted_iota(jnp.int32, (1, tq), 1) + 1
    all_selected = n_valid <= n_sel
    state0 = (jnp.int32(0),
              jnp.where(all_selected, KEY_NEG_INF, kmin),
              kmax + 1,
              n_valid, jnp.zeros((1, tq), jnp.int32),
              all_selected.astype(jnp.int32))

    def search_cond(st):
        return (st[0] < SEARCH_CAP) & (jnp.min(st[5]) == 0)

    def search_step(st):
        it, lo, hi, c_lo, c_hi, done = st
        mid = lo + ((hi >> 1) - (lo >> 1))
        frac = ((c_lo - n_sel).astype(jnp.float32) + 0.5) / (c_lo - c_hi).astype(jnp.float32)
        v_lo, v_hi = _key_value(lo), _key_value(hi)
        interp = _sortable_key(v_lo + frac * (v_hi - v_lo))
        cand = jnp.where(it % 2 == 0, interp, mid)
        cand = jnp.minimum(jnp.maximum(cand, lo + 1), hi - 1)
        active = done == 0
        cand = jnp.where(active, cand, lo)
        cnt = count_ge(cand)
        up = active & (cnt >= n_sel)
        dn = active & (cnt < n_sel)
        lo, c_lo = jnp.where(up, cand, lo), jnp.where(up, cnt, c_lo)
        hi, c_hi = jnp.where(dn, cand, hi), jnp.where(dn, cnt, c_hi)
        done = jnp.where((done != 0) | (c_lo == n_sel) | (hi - 1 <= lo), 1, 0)
        return it + 1, lo, hi, c_lo, c_hi, done

    _, thr, _, c_lo, c_hi, _ = lax.while_loop(search_cond, search_step, state0)
    need = jnp.where(c_lo == n_sel, float(2 ** 30), (n_sel - c_hi).astype(jnp.float32))

    def bias_chunk(with_ties, c, n_eq_before):
        ks = pl.multiple_of(c * kc, kc)
        key = key_ref[c]
        causal = krow + ks <= qpos
        if with_ties:
            eq = key == thr
            rank = _mm(tri_ref[...], jnp.where(eq, 1.0, 0.0)) + n_eq_before
            sel = ((key > thr) | (eq & (rank <= need))) & causal
            n_eq_before = rank[kc - 1:kc, :]
        else:
            sel = (key >= thr) & causal
        key_ref[c] = lax.bitcast_convert_type(jnp.where(sel, 0.0, NEG_INF), jnp.int32)
        return n_eq_before

    has_ties = jnp.max(c_lo) > n_sel

    @pl.when(has_ties)
    def _():
        lax.fori_loop(0, nch, functools.partial(bias_chunk, True), jnp.zeros((1, tq), jnp.float32))

    @pl.when(jnp.logical_not(has_ties))
    def _():
        lax.fori_loop(0, nch, functools.partial(bias_chunk, False), jnp.zeros((1, tq), jnp.float32))

    acc_ref[...] = jnp.zeros(acc_ref.shape, jnp.float32)
    group = N_HEADS // N_KV_HEADS
    ones_rows = jnp.ones((ACC_ROWS - HEAD_DIM, kc), MXU_DTYPE)

    def attend_chunk(c, ms):
        ks = pl.multiple_of(c * kc, kc)
        bias = lax.bitcast_convert_type(key_ref[c], jnp.float32)
        kch = k_ref[pl.ds(ks, kc), :]
        vch = v_ref[c]
        vext = [jnp.concatenate([vch[g * HEAD_DIM:(g + 1) * HEAD_DIM, :], ones_rows], axis=0)
                for g in range(N_KV_HEADS)]
        slot0 = jnp.minimum(c, 0)
        for h in range(N_HEADS):
            g = h // group
            s_ref[slot0 + h] = _mm_nt(kch[:, g * HEAD_DIM:(g + 1) * HEAD_DIM],
                                      q_ref[:, h * HEAD_DIM:(h + 1) * HEAD_DIM]) + bias
        new_ms = []
        for h in range(N_HEADS):
            m_new = jnp.maximum(ms[h], jnp.max(s_ref[slot0 + h], axis=0, keepdims=True))
            p = jnp.exp(s_ref[slot0 + h] - m_new)
            acc_ref[h] = jnp.exp(ms[h] - m_new) * acc_ref[h] + _mm(vext[h // group], p)
            new_ms.append(m_new)
        return tuple(new_ms)

    lax.fori_loop(0, nch, attend_chunk, tuple(jnp.full((1, tq), -1e30, jnp.float32) for _ in range(N_HEADS)))

    out_t = jnp.concatenate([acc_ref[h, 0:HEAD_DIM, :] / acc_ref[h, HEAD_DIM:HEAD_DIM + 1, :]
                             for h in range(N_HEADS)], axis=0)
    o_ref[...] = out_t.T.astype(o_ref.dtype)


def _attention(qi, wi_t, q, ki, k, v_t, batch, seq, tq, kc):
    n = q.shape[0]
    n_sel = min(MAX_TOPK, seq // 4)
    nq = seq // tq
    nck = seq // kc
    assert seq % tq == 0 and seq % kc == 0 and kc >= n_sel and v_t.shape == (n // kc, LANES, kc)
    tri = jnp.asarray((np.arange(kc)[None, :] <= np.arange(kc)[:, None]).astype(np.float32), MXU_DTYPE)
    qblk = lambda w: pl.BlockSpec((tq, w), lambda b, j: (b * nq + j, 0))
    seqblk = pl.BlockSpec((seq, LANES), lambda b, j: (b, 0))
    return pl.pallas_call(
        functools.partial(_attn_kernel, tq=tq, kc=kc, n_sel=n_sel),
        grid=(batch, nq),
        in_specs=[qblk(512), pl.BlockSpec((LANES, tq), lambda b, j: (0, b * nq + j)), qblk(512),
                  seqblk, seqblk, pl.BlockSpec((nck, LANES, kc), lambda b, j: (b, 0, 0)),
                  pl.BlockSpec((kc, kc), lambda b, j: (0, 0))],
        out_specs=qblk(512),
        out_shape=jax.ShapeDtypeStruct((n, N_HEADS * HEAD_DIM), MXU_DTYPE),
        scratch_shapes=[pltpu.VMEM((nck, kc, tq), jnp.int32),
                        pltpu.VMEM((N_HEADS, kc, tq), jnp.float32),
                        pltpu.VMEM((N_HEADS, ACC_ROWS, tq), jnp.float32)],
        compiler_params=_cparams(2),
        name="dsa_attention",
    )(qi, wi_t, q, ki, k, v_t, tri)


def _merge_kernel(x_ref, ya_ref, u_ref, uh_ref, wg_ref, wua_ref, pw_ref, ps_ref, wup_ref, wo_ref,
                  g_ref, b_ref, rw_ref, rb_ref,
                  x1_ref, idx_ref, gate_ref, ext_ref, *, tm, seq):
    i = pl.program_id(0)
    x = x_ref[...]
    xb = x.astype(MXU_DTYPE)

    first = (i % (seq // tm)) == 0
    ext_ref[0:POOL_HALO, :] = jnp.where(first, 0.0, uh_ref[...])
    ext_ref[POOL_HALO:, :] = u_ref[...]
    pos1 = ((i % (seq // tm)) * tm + 1 + lax.broadcasted_iota(jnp.int32, (tm, 1), 0)).astype(jnp.float32)
    parts = []
    for g, win in enumerate(POOL_WINDOWS):
        e = ext_ref[:, g * POOL_GROUP:(g + 1) * POOL_GROUP]
        w = 1
        while w < win:
            e = e + pltpu.roll(e, w, axis=0)
            w *= 2
        tok = e[POOL_HALO:, :]
        ug = u_ref[:, g * POOL_GROUP:(g + 1) * POOL_GROUP]
        d = tok / jnp.minimum(pos1, float(win)) - ug
        parts.append(_mm(d, pw_ref[g]))
    y_pool = jnp.concatenate(parts, axis=1) * ps_ref[...]

    gates = jnp.dot(xb, wg_ref[...], preferred_element_type=jnp.float32)
    merged = (jax.nn.sigmoid(gates[:, :D_MODEL]) * _mm(ya_ref[...], wua_ref[...])
              + jax.nn.sigmoid(gates[:, D_MODEL:]) * _mm(y_pool, wup_ref[...]))
    mix = _mm(merged, wo_ref[...])
    x1 = _layer_norm(DN_ALPHA * x + mix, g_ref[...], b_ref[...])
    _store_rows(x1_ref, x1)

    logits = _mm(x1, rw_ref[...]) + rb_ref[...]
    lane = lax.broadcasted_iota(jnp.int32, logits.shape, 1)
    work = jnp.where(lane < N_EXPERTS, logits, NEG_INF)
    vals, idxs = [], []
    for _ in range(TOP_K):
        m = jnp.max(work, axis=1, keepdims=True)
        ix = jnp.min(jnp.where(work == m, lane, LANES), axis=1, keepdims=True)
        vals.append(m)
        idxs.append(ix)
        work = jnp.where(lane == ix, NEG_INF, work)
    es = [jnp.exp(vv - vals[0]) for vv in vals]
    den = es[0] + es[1] + es[2] + es[3]
    idx_out = jnp.zeros(logits.shape, jnp.int32)
    gate_out = jnp.zeros(logits.shape, jnp.float32)
    for kk in range(TOP_K):
        idx_out = jnp.where(lane == kk, idxs[kk], idx_out)
        gate_out = jnp.where(lane == kk, es[kk] / den, gate_out)
    idx_ref[...] = idx_out
    gate_ref[...] = gate_out


def _merge(x, y_attn, u, wg, wua, pw, ps, wup, wo, g, b, rw, rb, seq, tm=256):
    n = x.shape[0]
    hb = tm // POOL_HALO
    full = lambda shape: pl.BlockSpec(shape, lambda i: (0,) * len(shape))
    row = lambda w: pl.BlockSpec((tm, w), lambda i: (i, 0))
    return pl.pallas_call(
        functools.partial(_merge_kernel, tm=tm, seq=seq),
        grid=(n // tm,),
        in_specs=[row(D_MODEL), row(512), row(POOL_CH),
                  pl.BlockSpec((POOL_HALO, POOL_CH), lambda i: (jnp.maximum(i * hb - 1, 0), 0)),
                  full((D_MODEL, 2 * D_MODEL)), full((512, D_MODEL)),
                  full((4, POOL_GROUP, POOL_GROUP)), full((1, POOL_CH)), full((POOL_CH, D_MODEL)),
                  full((D_MODEL, D_MODEL)), full((1, D_MODEL)), full((1, D_MODEL)),
                  full((D_MODEL, LANES)), full((1, LANES))],
        out_specs=[pl.BlockSpec((tm * ROW_TILES, LANES), lambda i: (i, 0)), row(LANES), row(LANES)],
        out_shape=[jax.ShapeDtypeStruct((n * ROW_TILES, LANES), jnp.float32),
                   jax.ShapeDtypeStruct((n, LANES), jnp.int32),
                   jax.ShapeDtypeStruct((n, LANES), jnp.float32)],
        scratch_shapes=[pltpu.VMEM((tm + POOL_HALO, POOL_CH), jnp.float32)],
        compiler_params=_cparams(1),
        name="merge_ln1_router",
    )(x, y_attn, u, u, wg, wua, pw, ps, wup, wo, g, b, rw, rb)


def _gather_rows(idx_ref, n_rows, table_hbm, buf_ref, sem):
    def issue(i, carry):
        for j in range(GATHER_UNROLL):
            r = i * GATHER_UNROLL + j
            t = idx_ref[0, 0, r]
            pltpu.make_async_copy(table_hbm.at[pl.ds(pl.multiple_of(t * ROW_TILES, ROW_TILES), ROW_TILES), :],
                                  buf_ref.at[pl.ds(pl.multiple_of(r * ROW_TILES, ROW_TILES), ROW_TILES), :],
                                  sem).start()
        return carry
    assert n_rows % GATHER_UNROLL == 0
    lax.fori_loop(0, n_rows // GATHER_UNROLL, issue, 0)


def _wait_rows(table_hbm, buf_ref, sem):
    pltpu.make_async_copy(table_hbm.at[pl.ds(0, buf_ref.shape[0]), :], buf_ref, sem).wait()


def _rows_as_matrix(buf_ref, r0, n_rows):
    return jnp.concatenate(
        [buf_ref[pl.ds(r0 * ROW_TILES + c, n_rows, stride=ROW_TILES), :] for c in range(ROW_TILES)], axis=1)


def _store_rows(out_ref, y):
    for c in range(ROW_TILES):
        out_ref[pl.ds(c, y.shape[0], stride=ROW_TILES), :] = y[:, c * LANES:(c + 1) * LANES]


def _expert_kernel(be_ref, nu_ref, tok_ref, tok_next_ref, x_hbm, wgu_ref, bgu_ref, wd_ref, bd_ref, y_ref,
                   xbuf0, xbuf1, wgu_bf, wd_bf, sem, *, tb):
    b = pl.program_id(0)
    n_used = nu_ref[0]
    bufs = (xbuf0, xbuf1)

    @pl.when(b == 0)
    def _():
        _gather_rows(tok_ref, tb, x_hbm, xbuf0, sem.at[0])

    for slot in range(2):
        @pl.when((b + 1 < n_used) & ((b + 1) % 2 == slot))
        def _():
            _gather_rows(tok_next_ref, tb, x_hbm, bufs[slot], sem.at[slot])

    @pl.when(b < n_used)
    def _():
        e = be_ref[b]
        e_prev = be_ref[jnp.maximum(b - 1, 0)]

        @pl.when((b == 0) | (e != e_prev))
        def _():
            wgu_bf[...] = wgu_ref[...].astype(MXU_DTYPE)
            wd_bf[...] = wd_ref[...].astype(MXU_DTYPE)

        def compute(xbuf, slot):
            _wait_rows(x_hbm, xbuf, sem.at[slot])
            xr = _rows_as_matrix(xbuf, 0, tb)
            gu = _mm(xr, wgu_bf[...]) + bgu_ref[...]
            gt = jnp.minimum(gu[:, :D_FF], SWIGLU_LIMIT)
            up = jnp.clip(gu[:, D_FF:], -SWIGLU_LIMIT, SWIGLU_LIMIT)
            act = gt * jax.nn.sigmoid(SWIGLU_ALPHA * gt) * (up + 1.0)
            _store_rows(y_ref, _mm(act, wd_bf[...]) + bd_ref[...])

        for slot in range(2):
            pl.when(b % 2 == slot)(functools.partial(compute, bufs[slot], slot))

    @pl.when(b >= n_used)
    def _():
        y_ref[...] = jnp.zeros(y_ref.shape, y_ref.dtype)


def _experts(blk_exp, n_used, row_tok, x1_rows, w_gu, b_gu, w_down, b_down, tb):
    n_blocks = blk_exp.shape[0]
    wspec = lambda r, c: pl.BlockSpec((None, r, c), lambda b, be, nu: (be[b], 0, 0))
    tok = row_tok.reshape(n_blocks, 1, tb)
    return pl.pallas_call(
        functools.partial(_expert_kernel, tb=tb),
        grid_spec=pltpu.PrefetchScalarGridSpec(
            num_scalar_prefetch=2,
            grid=(n_blocks,),
            in_specs=[pl.BlockSpec((1, 1, tb), lambda b, be, nu: (b, 0, 0), memory_space=pltpu.SMEM),
                      pl.BlockSpec((1, 1, tb), lambda b, be, nu: (jnp.minimum(b + 1, n_blocks - 1), 0, 0),
                                   memory_space=pltpu.SMEM),
                      pl.BlockSpec(memory_space=pl.ANY),
                      wspec(D_MODEL, 2 * D_FF), wspec(1, 2 * D_FF), wspec(D_FF, D_MODEL), wspec(1, D_MODEL)],
            out_specs=pl.BlockSpec((tb * ROW_TILES, LANES), lambda b, be, nu: (b, 0)),
            scratch_shapes=[pltpu.VMEM((tb * ROW_TILES, LANES), jnp.float32),
                            pltpu.VMEM((tb * ROW_TILES, LANES), jnp.float32),
                            pltpu.VMEM((D_MODEL, 2 * D_FF), MXU_DTYPE),
                            pltpu.VMEM((D_FF, D_MODEL), MXU_DTYPE),
                            pltpu.SemaphoreType.DMA((2,))]),
        out_shape=jax.ShapeDtypeStruct((n_blocks * tb * ROW_TILES, LANES), jnp.float32),
        compiler_params=_cparams(1),
        name="moe_experts",
    )(blk_exp, n_used, tok, tok, x1_rows, w_gu, b_gu, w_down, b_down)


def _combine_kernel(dest_ref, dest_next_ref, y_hbm, gate_ref, x1_ref, p_ref, wpg_ref, wpp_ref, g_ref, b_ref,
                    o_ref, ybuf0, ybuf1, sem, *, tm):
    i = pl.program_id(0)
    bufs = (ybuf0, ybuf1)

    @pl.when(i == 0)
    def _():
        _gather_rows(dest_ref, TOP_K * tm, y_hbm, ybuf0, sem.at[0])

    for slot in range(2):
        @pl.when((i + 1 < pl.num_programs(0)) & ((i + 1) % 2 == slot))
        def _():
            _gather_rows(dest_next_ref, TOP_K * tm, y_hbm, bufs[slot], sem.at[slot])

    x1 = _rows_as_matrix(x1_ref, 0, tm)
    ple = jax.nn.sigmoid(_mm(x1, wpg_ref[...])) * _mm(p_ref[...], wpp_ref[...])
    h = DN_ALPHA * x1 + ple
    gate = gate_ref[...]

    def finish(ybuf, slot):
        _wait_rows(y_hbm, ybuf, sem.at[slot])
        ffn = h
        for kk in range(TOP_K):
            ffn = ffn + _rows_as_matrix(ybuf, kk * tm, tm) * gate[:, kk:kk + 1]
        o_ref[...] = _layer_norm(ffn, g_ref[...], b_ref[...])

    for slot in range(2):
        pl.when(i % 2 == slot)(functools.partial(finish, bufs[slot], slot))


def _combine(dest, yr, gates, x1_rows, p, wpg, wpp, g, b, tm=256):
    n = p.shape[0]
    nt = n // tm
    full = lambda shape: pl.BlockSpec(shape, lambda i: (0,) * len(shape))
    row = lambda w: pl.BlockSpec((tm, w), lambda i: (i, 0))
    dest_t = dest.reshape(nt, tm, TOP_K).transpose(0, 2, 1).reshape(nt, 1, TOP_K * tm)
    return pl.pallas_call(
        functools.partial(_combine_kernel, tm=tm),
        grid=(nt,),
        in_specs=[pl.BlockSpec((1, 1, TOP_K * tm), lambda i: (i, 0, 0), memory_space=pltpu.SMEM),
                  pl.BlockSpec((1, 1, TOP_K * tm), lambda i: (jnp.minimum(i + 1, nt - 1), 0, 0),
                               memory_space=pltpu.SMEM),
                  pl.BlockSpec(memory_space=pl.ANY),
                  row(LANES), pl.BlockSpec((tm * ROW_TILES, LANES), lambda i: (i, 0)), row(PLE_DIM),
                  full((D_MODEL, D_MODEL)), full((PLE_DIM, D_MODEL)),
                  full((1, D_MODEL)), full((1, D_MODEL))],
        out_specs=row(D_MODEL),
        out_shape=jax.ShapeDtypeStruct((n, D_MODEL), jnp.float32),
        scratch_shapes=[pltpu.VMEM((TOP_K * tm * ROW_TILES, LANES), jnp.float32),
                        pltpu.VMEM((TOP_K * tm * ROW_TILES, LANES), jnp.float32),
                        pltpu.SemaphoreType.DMA((2,))],
        compiler_params=_cparams(1),
        name="combine_ple_ln2",
    )(dest_t, dest_t, yr, gates, x1_rows, p, wpg, wpp, g, b)


def _route(top_idx, n_tokens, tb):
    a = n_tokens * TOP_K
    n_blocks = a // tb + N_EXPERTS
    onehot = (top_idx[:, :, None] == jnp.arange(N_EXPERTS, dtype=jnp.int32)[None, None, :]).astype(jnp.int32)
    member = onehot.sum(axis=1)
    rank = jnp.cumsum(member, axis=0) - member
    counts = member.sum(axis=0)
    padded = (counts + tb - 1) // tb * tb
    pend = jnp.cumsum(padded)
    pstart = pend - padded
    dest = jnp.take_along_axis(rank + pstart[None, :], top_idx, axis=1).astype(jnp.int32)
    tok = jnp.broadcast_to(jnp.arange(n_tokens, dtype=jnp.int32)[:, None], (n_tokens, TOP_K))
    row_tok = jnp.zeros((n_blocks * tb,), jnp.int32).at[dest.reshape(-1)].set(tok.reshape(-1))
    blk_start = jnp.arange(n_blocks, dtype=jnp.int32) * tb
    blk_exp = jnp.minimum((blk_start[:, None] >= pend[None, :]).astype(jnp.int32).sum(axis=1), N_EXPERTS - 1)
    n_used = (pend[-1] // tb).astype(jnp.int32).reshape(1)
    return dest, row_tok, blk_exp, n_used


def kernel(x, p, ln0_g, ln0_b, w_in, pool_w, pool_scale, w_up_attn, w_up_pool, w_out, ln1_g, ln1_b,
           router_w, router_b, exp_w_gu, exp_b_gu, exp_w_down, exp_b_down, ple_w_gate, ple_w_proj,
           ln2_g, ln2_b):
    batch, seq, d = x.shape
    assert d == D_MODEL
    n = batch * seq
    tb = 256
    kc = 512
    tables = _rope_tables(seq)
    bf = lambda a: a.astype(MXU_DTYPE)
    vec = lambda a: a.reshape(1, -1)
    n_le = DEPTH * N_EXPERTS
    w_gu_all = exp_w_gu.reshape(n_le, D_MODEL, 2 * D_FF)
    b_gu_all = exp_b_gu.reshape(n_le, 1, 2 * D_FF)
    w_down_all = exp_w_down.reshape(n_le, D_FF, D_MODEL)
    b_down_all = exp_b_down.reshape(n_le, 1, D_MODEL)

    h = _ln0(x.reshape(n, d), ln0_g, ln0_b)
    for i in range(DEPTH):
        w = w_in[i]
        w1 = bf(jnp.concatenate([w[:, :1352], jnp.zeros((d, OFF_U - 1352), w.dtype), w[:, 1352:1864]], axis=1))
        wg = bf(w[:, 1864:])
        q, k, v_t, qi, ki, wi_t, u = _inproj(h, w1, tables, seq, kc)
        y_attn = _attention(qi, wi_t, q, ki, k, v_t, batch, seq, 256, kc)
        rw = bf(jnp.pad(router_w[i], ((0, 0), (0, LANES - N_EXPERTS))))
        rb = jnp.pad(router_b[i], (0, LANES - N_EXPERTS)).reshape(1, -1)
        x1_rows, idx_l, gate_l = _merge(h, y_attn, u, wg, bf(w_up_attn[i]), bf(pool_w[i]), vec(pool_scale[i]),
                                        bf(w_up_pool[i]), bf(w_out[i]), vec(ln1_g[i]), vec(ln1_b[i]), rw, rb, seq)
        dest, row_tok, blk_exp, n_used = _route(idx_l[:, :TOP_K], n, tb)
        yr = _experts(blk_exp + i * N_EXPERTS, n_used, row_tok, x1_rows,
                      w_gu_all, b_gu_all, w_down_all, b_down_all, tb)
        h = _combine(dest, yr, gate_l, x1_rows, p[i].reshape(n, PLE_DIM), bf(ple_w_gate[i]), bf(ple_w_proj[i]),
                     vec(ln2_g[i]), vec(ln2_b[i]))
    return h.reshape(batch, seq, d)
```

```python
import functools

import jax
import jax.numpy as jnp
import numpy as np
from jax import lax
from jax.experimental import pallas as pl
from jax.experimental.pallas import tpu as pltpu

MXU_DTYPE = jnp.bfloat16

D_MODEL = 1024
HEAD_DIM = 64
N_HEADS = 8
N_KV_HEADS = 2
ROT_DIM = 16
ROPE_THETA = 500000.0
IDX_HEADS = 8
IDX_DIM = 64
MAX_TOPK = 256
POOL_CH = 512
POOL_WINDOWS = (2, 4, 8, 16)
POOL_GROUP = 128
POOL_HALO = 16
N_EXPERTS = 32
TOP_K = 4
D_FF = 1024
SWIGLU_LIMIT = 7.0
SWIGLU_ALPHA = 1.702
PLE_DIM = 256
LN_EPS = 1e-5
DEPTH = 2
DN_ALPHA = (2 * DEPTH) ** 0.25

LANES = 128
SUBLANES = 8
ROW_TILES = D_MODEL // LANES
assert ROW_TILES == SUBLANES

OFF_Q, OFF_K, OFF_V, OFF_QI, OFF_KIW, OFF_U, W1_COLS = 0, 512, 640, 768, 1280, 1408, 1920

VMEM_LIMIT = 56 * 1024 * 1024
NEG_INF = float("-inf")
KEY_NEG_INF = -2139095041
SEARCH_CAP = 36
SEARCH_PASSES_PER_CHECK = 4
ACC_ROWS = 80
GATHER_UNROLL = 8

def _cparams(n_axes, flags=None):
    return pltpu.CompilerParams(dimension_semantics=("arbitrary",) * n_axes,
                                vmem_limit_bytes=VMEM_LIMIT, flags=flags)


def _mm(a, b):
    return jnp.dot(a.astype(MXU_DTYPE), b.astype(MXU_DTYPE), preferred_element_type=jnp.float32)


def _mm_nt(a, b):
    return lax.dot_general(a.astype(MXU_DTYPE), b.astype(MXU_DTYPE), (((1,), (1,)), ((), ())),
                           preferred_element_type=jnp.float32)


def _layer_norm(h, g, b):
    mu = jnp.mean(h, axis=-1, keepdims=True)
    c = h - mu
    var = jnp.mean(c * c, axis=-1, keepdims=True)
    return c * lax.rsqrt(var + LN_EPS) * g + b


def _ln0_kernel(x_ref, g_ref, b_ref, o_ref):
    o_ref[...] = _layer_norm(x_ref[...], g_ref[...], b_ref[...])


def _ln0(x, g, b, tm=512):
    n = x.shape[0]
    return pl.pallas_call(
        _ln0_kernel,
        grid=(n // tm,),
        in_specs=[pl.BlockSpec((tm, D_MODEL), lambda i: (i, 0)),
                  pl.BlockSpec((1, D_MODEL), lambda i: (0, 0)),
                  pl.BlockSpec((1, D_MODEL), lambda i: (0, 0))],
        out_specs=pl.BlockSpec((tm, D_MODEL), lambda i: (i, 0)),
        out_shape=jax.ShapeDtypeStruct((n, D_MODEL), jnp.float32),
        compiler_params=_cparams(1),
        name="ln0",
    )(x, g.reshape(1, -1), b.reshape(1, -1))


def _rope_tables(seq):
    pos = jnp.arange(seq, dtype=jnp.float32)
    inv = ROPE_THETA ** (-jnp.arange(0, ROT_DIM, 2, dtype=jnp.float32) / ROT_DIM)
    ang = pos[:, None] * inv[None, :]
    cos, sin = jnp.cos(ang), jnp.sin(ang)
    half = ROT_DIM // 2
    one = jnp.ones((seq, HEAD_DIM - ROT_DIM), jnp.float32)
    zero = jnp.zeros((seq, HEAD_DIM - ROT_DIM), jnp.float32)
    zh = jnp.zeros((seq, half), jnp.float32)
    c64 = jnp.concatenate([cos, cos, one], axis=1)
    s1_64 = jnp.concatenate([-sin, zh, zero], axis=1)
    s2_64 = jnp.concatenate([zh, sin, zero], axis=1)
    ident_c = jnp.ones((seq, HEAD_DIM), jnp.float32)
    ident_s = jnp.zeros((seq, HEAD_DIM), jnp.float32)
    c = jnp.concatenate([c64, c64, c64, ident_c], axis=1)
    s1 = jnp.concatenate([s1_64, s1_64, s1_64, ident_s], axis=1)
    s2 = jnp.concatenate([s2_64, s2_64, s2_64, ident_s], axis=1)
    return c, s1, s2


def _rope_tile(x, c, s1, s2):
    half = ROT_DIM // 2
    return x * c + pltpu.roll(x, LANES - half, axis=1) * s1 + pltpu.roll(x, half, axis=1) * s2


def _inproj_kernel(x_ref, w_ref, c_ref, s1_ref, s2_ref,
                   q_ref, k_ref, v_ref, qi_ref, ki_ref, wi_ref, u_ref):
    xb = x_ref[...].astype(MXU_DTYPE)
    c, s1, s2 = c_ref[:, :LANES], s1_ref[:, :LANES], s2_ref[:, :LANES]
    ck, s1k, s2k = c_ref[:, LANES:], s1_ref[:, LANES:], s2_ref[:, LANES:]

    def seg(off, width):
        return jnp.dot(xb, w_ref[:, off:off + width], preferred_element_type=jnp.float32)

    def roped(z, scale, out_ref):
        for j in range(z.shape[1] // LANES):
            t = _rope_tile(z[:, j * LANES:(j + 1) * LANES], c, s1, s2)
            out_ref[:, j * LANES:(j + 1) * LANES] = (t * scale).astype(out_ref.dtype)

    roped(seg(OFF_Q, 512), HEAD_DIM ** -0.5, q_ref)
    roped(seg(OFF_K, 128), 1.0, k_ref)
    v_ref[0] = seg(OFF_V, 128).T.astype(v_ref.dtype)
    roped(seg(OFF_QI, 512), IDX_DIM ** -0.5, qi_ref)
    kiw = seg(OFF_KIW, 128)
    ki_ref[...] = _rope_tile(kiw, ck, s1k, s2k).astype(ki_ref.dtype)
    wi_ref[...] = (kiw * (IDX_HEADS ** -0.5)).T
    u_ref[...] = seg(OFF_U, 512)


def _inproj(x, w1, tables, seq, tm):
    n = x.shape[0]
    nseq = seq // tm
    c, s1, s2 = tables
    tab_spec = pl.BlockSpec((tm, 2 * LANES), lambda i: (i % nseq, 0))

    def out(width, dtype):
        return pl.BlockSpec((tm, width), lambda i: (i, 0)), jax.ShapeDtypeStruct((n, width), dtype)

    v_t = (pl.BlockSpec((1, LANES, tm), lambda i: (i, 0, 0)), jax.ShapeDtypeStruct((n // tm, LANES, tm), MXU_DTYPE))
    wi_t = (pl.BlockSpec((LANES, tm), lambda i: (0, i)), jax.ShapeDtypeStruct((LANES, n), jnp.float32))
    outs = [out(512, MXU_DTYPE), out(128, MXU_DTYPE), v_t, out(512, MXU_DTYPE),
            out(128, MXU_DTYPE), wi_t, out(512, jnp.float32)]
    return pl.pallas_call(
        _inproj_kernel,
        grid=(n // tm,),
        in_specs=[pl.BlockSpec((tm, D_MODEL), lambda i: (i, 0)),
                  pl.BlockSpec((D_MODEL, W1_COLS), lambda i: (0, 0)),
                  tab_spec, tab_spec, tab_spec],
        out_specs=[o[0] for o in outs],
        out_shape=[o[1] for o in outs],
        compiler_params=_cparams(1),
        name="inproj",
    )(x, w1, c, s1, s2)


def _sortable_key(s):
    k = lax.bitcast_convert_type(s, jnp.int32)
    return k ^ ((k >> 31) & jnp.int32(0x7FFFFFFF))


def _attn_kernel(qi_ref, wi_ref, q_ref, ki_ref, k_ref, v_ref, tri_ref, o_ref,
                 key_ref, s_ref, acc_ref, *, tq, kc, n_sel):
    j = pl.program_id(1)
    q0 = j * tq
    nch = (q0 + tq + kc - 1) // kc
    qpos = q0 + lax.broadcasted_iota(jnp.int32, (kc, tq), 1)
    krow = lax.broadcasted_iota(jnp.int32, (kc, tq), 0)
    imax = jnp.iinfo(jnp.int32).max

    def score_chunk(c, carry):
        kmin, kmax = carry
        ks = pl.multiple_of(c * kc, kc)
        kic = ki_ref[pl.ds(ks, kc), :][:, :IDX_DIM]
        acc = jnp.zeros((kc, tq), jnp.float32)
        for h in range(IDX_HEADS):
            d = _mm_nt(kic, qi_ref[:, h * IDX_DIM:(h + 1) * IDX_DIM])
            acc = acc + jnp.maximum(d, 0.0) * wi_ref[IDX_DIM + h:IDX_DIM + h + 1, :]
        causal = krow + ks <= qpos
        key = _sortable_key(jnp.where(causal, acc, NEG_INF))
        key_ref[c] = key
        kmin = jnp.minimum(kmin, jnp.min(jnp.where(causal, key, imax), axis=0, keepdims=True))
        kmax = jnp.maximum(kmax, jnp.max(key, axis=0, keepdims=True))
        return kmin, kmax

    kmin, kmax = lax.fori_loop(0, nch, score_chunk,
                               (jnp.full((1, tq), imax, jnp.int32), jnp.full((1, tq), KEY_NEG_INF, jnp.int32)))

    def count_ge(cand):
        def body(c, cnt):
            hit = jnp.where(key_ref[c].reshape(kc // SUBLANES, SUBLANES, tq) >= cand[None], 1, 0)
            return cnt + jnp.sum(hit, axis=0)
        cnt = lax.fori_loop(0, nch, body, jnp.zeros((SUBLANES, tq), jnp.int32))
        for shift in (4, 2, 1):
            cnt = cnt + pltpu.roll(cnt, shift, axis=0)
        return cnt

    rep = lambda a: jnp.broadcast_to(a, (SUBLANES, tq))
    n_valid = q0 + lax.broadcasted_iota(jnp.int32, (SUBLANES, tq), 1) + 1
    all_selected = n_valid <= n_sel
    state0 = (jnp.int32(0),
              jnp.where(all_selected, KEY_NEG_INF, rep(kmin)),
              rep(kmax) + 1,
              n_valid, jnp.zeros((SUBLANES, tq), jnp.int32),
              all_selected.astype(jnp.int32))

    def search_cond(st):
        return (st[0] < SEARCH_CAP) & (jnp.min(st[5]) == 0)

    def search_pass(st):
        lo, hi, c_lo, c_hi, done = st
        active = done == 0
        cand = jnp.where(active, lo + ((hi >> 1) - (lo >> 1)), lo)
        cnt = count_ge(cand)
        up = active & (cnt >= n_sel)
        dn = active & (cnt < n_sel)
        lo, c_lo = jnp.where(up, cand, lo), jnp.where(up, cnt, c_lo)
        hi, c_hi = jnp.where(dn, cand, hi), jnp.where(dn, cnt, c_hi)
        done = jnp.where((done != 0) | (c_lo == n_sel) | (hi - 1 <= lo), 1, 0)
        return lo, hi, c_lo, c_hi, done

    def search_step(st):
        inner = st[1:]
        for _ in range(SEARCH_PASSES_PER_CHECK):
            inner = search_pass(inner)
        return (st[0] + SEARCH_PASSES_PER_CHECK,) + inner

    _, thr, _, c_lo, c_hi, _ = lax.while_loop(search_cond, search_step, state0)
    thr, c_lo, c_hi = thr[0:1, :], c_lo[0:1, :], c_hi[0:1, :]
    need = jnp.where(c_lo == n_sel, float(2 ** 30), (n_sel - c_hi).astype(jnp.float32))

    def bias_chunk(with_ties, c, n_eq_before):
        ks = pl.multiple_of(c * kc, kc)
        key = key_ref[c]
        causal = krow + ks <= qpos
        if with_ties:
            eq = key == thr
            rank = _mm(tri_ref[...], jnp.where(eq, 1.0, 0.0)) + n_eq_before
            sel = ((key > thr) | (eq & (rank <= need))) & causal
            n_eq_before = rank[kc - 1:kc, :]
        else:
            sel = (key >= thr) & causal
        key_ref[c] = lax.bitcast_convert_type(jnp.where(sel, 0.0, NEG_INF), jnp.int32)
        return n_eq_before

    has_ties = jnp.max(c_lo) > n_sel

    @pl.when(has_ties)
    def _():
        lax.fori_loop(0, nch, functools.partial(bias_chunk, True), jnp.zeros((1, tq), jnp.float32))

    @pl.when(jnp.logical_not(has_ties))
    def _():
        lax.fori_loop(0, nch, functools.partial(bias_chunk, False), jnp.zeros((1, tq), jnp.float32))

    acc_ref[...] = jnp.zeros(acc_ref.shape, jnp.float32)
    group = N_HEADS // N_KV_HEADS
    ones_rows = jnp.ones((ACC_ROWS - HEAD_DIM, kc), MXU_DTYPE)

    def attend_chunk(c, ms):
        ks = pl.multiple_of(c * kc, kc)
        bias = lax.bitcast_convert_type(key_ref[c], jnp.float32)
        kch = k_ref[pl.ds(ks, kc), :]
        vch = v_ref[c]
        vext = [jnp.concatenate([vch[g * HEAD_DIM:(g + 1) * HEAD_DIM, :], ones_rows], axis=0)
                for g in range(N_KV_HEADS)]
        slot0 = jnp.minimum(c, 0)
        for h in range(N_HEADS):
            g = h // group
            s_ref[slot0 + h] = _mm_nt(kch[:, g * HEAD_DIM:(g + 1) * HEAD_DIM],
                                      q_ref[:, h * HEAD_DIM:(h + 1) * HEAD_DIM]) + bias
        new_ms = []
        for h in range(N_HEADS):
            m_new = jnp.maximum(ms[h], jnp.max(s_ref[slot0 + h], axis=0, keepdims=True))
            p = jnp.exp(s_ref[slot0 + h] - m_new)
            acc_ref[h] = jnp.exp(ms[h] - m_new) * acc_ref[h] + _mm(vext[h // group], p)
            new_ms.append(m_new)
        return tuple(new_ms)

    lax.fori_loop(0, nch, attend_chunk, tuple(jnp.full((1, tq), -1e30, jnp.float32) for _ in range(N_HEADS)))

    out_t = jnp.concatenate([acc_ref[h, 0:HEAD_DIM, :] / acc_ref[h, HEAD_DIM:HEAD_DIM + 1, :]
                             for h in range(N_HEADS)], axis=0)
    o_ref[...] = out_t.T.astype(o_ref.dtype)


def _attention(qi, wi_t, q, ki, k, v_t, batch, seq, tq, kc):
    n = q.shape[0]
    n_sel = min(MAX_TOPK, seq // 4)
    nq = seq // tq
    nck = seq // kc
    assert seq % tq == 0 and seq % kc == 0 and kc >= n_sel and v_t.shape == (n // kc, LANES, kc)
    tri = jnp.asarray((np.arange(kc)[None, :] <= np.arange(kc)[:, None]).astype(np.float32), MXU_DTYPE)
    qblk = lambda w: pl.BlockSpec((tq, w), lambda b, j: (b * nq + j, 0))
    seqblk = pl.BlockSpec((seq, LANES), lambda b, j: (b, 0))
    return pl.pallas_call(
        functools.partial(_attn_kernel, tq=tq, kc=kc, n_sel=n_sel),
        grid=(batch, nq),
        in_specs=[qblk(512), pl.BlockSpec((LANES, tq), lambda b, j: (0, b * nq + j)), qblk(512),
                  seqblk, seqblk, pl.BlockSpec((nck, LANES, kc), lambda b, j: (b, 0, 0)),
                  pl.BlockSpec((kc, kc), lambda b, j: (0, 0))],
        out_specs=qblk(512),
        out_shape=jax.ShapeDtypeStruct((n, N_HEADS * HEAD_DIM), MXU_DTYPE),
        scratch_shapes=[pltpu.VMEM((nck, kc, tq), jnp.int32),
                        pltpu.VMEM((N_HEADS, kc, tq), jnp.float32),
                        pltpu.VMEM((N_HEADS, ACC_ROWS, tq), jnp.float32)],
        compiler_params=_cparams(2),
        name="dsa_attention",
    )(qi, wi_t, q, ki, k, v_t, tri)


def _merge_kernel(x_ref, ya_ref, u_ref, uh_ref, wg_ref, wua_ref, pw_ref, ps_ref, wup_ref, wo_ref,
                  g_ref, b_ref, rw_ref, rb_ref,
                  x1_ref, idx_ref, gate_ref, ext_ref, *, tm, seq):
    i = pl.program_id(0)
    x = x_ref[...]
    xb = x.astype(MXU_DTYPE)

    first = (i % (seq // tm)) == 0
    ext_ref[0:POOL_HALO, :] = jnp.where(first, 0.0, uh_ref[...])
    ext_ref[POOL_HALO:, :] = u_ref[...]
    pos1 = ((i % (seq // tm)) * tm + 1 + lax.broadcasted_iota(jnp.int32, (tm, 1), 0)).astype(jnp.float32)
    parts = []
    for g, win in enumerate(POOL_WINDOWS):
        e = ext_ref[:, g * POOL_GROUP:(g + 1) * POOL_GROUP]
        w = 1
        while w < win:
            e = e + pltpu.roll(e, w, axis=0)
            w *= 2
        tok = e[POOL_HALO:, :]
        ug = u_ref[:, g * POOL_GROUP:(g + 1) * POOL_GROUP]
        d = tok / jnp.minimum(pos1, float(win)) - ug
        parts.append(_mm(d, pw_ref[g]))
    y_pool = jnp.concatenate(parts, axis=1) * ps_ref[...]

    gates = jnp.dot(xb, wg_ref[...], preferred_element_type=jnp.float32)
    merged = (jax.nn.sigmoid(gates[:, :D_MODEL]) * _mm(ya_ref[...], wua_ref[...])
              + jax.nn.sigmoid(gates[:, D_MODEL:]) * _mm(y_pool, wup_ref[...]))
    mix = _mm(merged, wo_ref[...])
    x1 = _layer_norm(DN_ALPHA * x + mix, g_ref[...], b_ref[...])
    _store_rows(x1_ref, x1)

    logits = _mm(x1, rw_ref[...]) + rb_ref[...]
    lane = lax.broadcasted_iota(jnp.int32, logits.shape, 1)
    work = jnp.where(lane < N_EXPERTS, logits, NEG_INF)
    vals, idxs = [], []
    for _ in range(TOP_K):
        m = jnp.max(work, axis=1, keepdims=True)
        ix = jnp.min(jnp.where(work == m, lane, LANES), axis=1, keepdims=True)
        vals.append(m)
        idxs.append(ix)
        work = jnp.where(lane == ix, NEG_INF, work)
    es = [jnp.exp(vv - vals[0]) for vv in vals]
    den = es[0] + es[1] + es[2] + es[3]
    idx_out = jnp.zeros(logits.shape, jnp.int32)
    gate_out = jnp.zeros(logits.shape, jnp.float32)
    for kk in range(TOP_K):
        idx_out = jnp.where(lane == kk, idxs[kk], idx_out)
        gate_out = jnp.where(lane == kk, es[kk] / den, gate_out)
    idx_ref[...] = idx_out
    gate_ref[...] = gate_out


def _merge(x, y_attn, u, wg, wua, pw, ps, wup, wo, g, b, rw, rb, seq, tm=256):
    n = x.shape[0]
    hb = tm // POOL_HALO
    full = lambda shape: pl.BlockSpec(shape, lambda i: (0,) * len(shape))
    row = lambda w: pl.BlockSpec((tm, w), lambda i: (i, 0))
    return pl.pallas_call(
        functools.partial(_merge_kernel, tm=tm, seq=seq),
        grid=(n // tm,),
        in_specs=[row(D_MODEL), row(512), row(POOL_CH),
                  pl.BlockSpec((POOL_HALO, POOL_CH), lambda i: (jnp.maximum(i * hb - 1, 0), 0)),
                  full((D_MODEL, 2 * D_MODEL)), full((512, D_MODEL)),
                  full((4, POOL_GROUP, POOL_GROUP)), full((1, POOL_CH)), full((POOL_CH, D_MODEL)),
                  full((D_MODEL, D_MODEL)), full((1, D_MODEL)), full((1, D_MODEL)),
                  full((D_MODEL, LANES)), full((1, LANES))],
        out_specs=[pl.BlockSpec((tm * ROW_TILES, LANES), lambda i: (i, 0)), row(LANES), row(LANES)],
        out_shape=[jax.ShapeDtypeStruct((n * ROW_TILES, LANES), jnp.float32),
                   jax.ShapeDtypeStruct((n, LANES), jnp.int32),
                   jax.ShapeDtypeStruct((n, LANES), jnp.float32)],
        scratch_shapes=[pltpu.VMEM((tm + POOL_HALO, POOL_CH), jnp.float32)],
        compiler_params=_cparams(1),
        name="merge_ln1_router",
    )(x, y_attn, u, u, wg, wua, pw, ps, wup, wo, g, b, rw, rb)


def _gather_rows(idx_ref, n_rows, table_hbm, buf_ref, sem):
    def issue(i, carry):
        for j in range(GATHER_UNROLL):
            r = i * GATHER_UNROLL + j
            t = idx_ref[0, 0, r]
            pltpu.make_async_copy(table_hbm.at[pl.ds(pl.multiple_of(t * ROW_TILES, ROW_TILES), ROW_TILES), :],
                                  buf_ref.at[pl.ds(pl.multiple_of(r * ROW_TILES, ROW_TILES), ROW_TILES), :],
                                  sem).start()
        return carry
    assert n_rows % GATHER_UNROLL == 0
    lax.fori_loop(0, n_rows // GATHER_UNROLL, issue, 0)


def _wait_rows(table_hbm, buf_ref, sem):
    pltpu.make_async_copy(table_hbm.at[pl.ds(0, buf_ref.shape[0]), :], buf_ref, sem).wait()


def _rows_as_matrix(buf_ref, r0, n_rows):
    return jnp.concatenate(
        [buf_ref[pl.ds(r0 * ROW_TILES + c, n_rows, stride=ROW_TILES), :] for c in range(ROW_TILES)], axis=1)


def _store_rows(out_ref, y):
    for c in range(ROW_TILES):
        out_ref[pl.ds(c, y.shape[0], stride=ROW_TILES), :] = y[:, c * LANES:(c + 1) * LANES]


def _expert_kernel(be_ref, nu_ref, tok_ref, tok_next_ref, x_hbm, wgu_ref, bgu_ref, wd_ref, bd_ref, y_ref,
                   xbuf0, xbuf1, wgu_bf, wd_bf, sem, *, tb):
    b = pl.program_id(0)
    n_used = nu_ref[0]
    bufs = (xbuf0, xbuf1)

    @pl.when(b == 0)
    def _():
        _gather_rows(tok_ref, tb, x_hbm, xbuf0, sem.at[0])

    for slot in range(2):
        @pl.when((b + 1 < n_used) & ((b + 1) % 2 == slot))
        def _():
            _gather_rows(tok_next_ref, tb, x_hbm, bufs[slot], sem.at[slot])

    @pl.when(b < n_used)
    def _():
        e = be_ref[b]
        e_prev = be_ref[jnp.maximum(b - 1, 0)]

        @pl.when((b == 0) | (e != e_prev))
        def _():
            wgu_bf[...] = wgu_ref[...].astype(MXU_DTYPE)
            wd_bf[...] = wd_ref[...].astype(MXU_DTYPE)

        def compute(xbuf, slot):
            _wait_rows(x_hbm, xbuf, sem.at[slot])
            xr = _rows_as_matrix(xbuf, 0, tb)
            gu = _mm(xr, wgu_bf[...]) + bgu_ref[...]
            gt = jnp.minimum(gu[:, :D_FF], SWIGLU_LIMIT)
            up = jnp.clip(gu[:, D_FF:], -SWIGLU_LIMIT, SWIGLU_LIMIT)
            act = gt * jax.nn.sigmoid(SWIGLU_ALPHA * gt) * (up + 1.0)
            _store_rows(y_ref, _mm(act, wd_bf[...]) + bd_ref[...])

        for slot in range(2):
            pl.when(b % 2 == slot)(functools.partial(compute, bufs[slot], slot))

    @pl.when(b >= n_used)
    def _():
        y_ref[...] = jnp.zeros(y_ref.shape, y_ref.dtype)


def _experts(blk_exp, n_used, row_tok, x1_rows, w_gu, b_gu, w_down, b_down, tb):
    n_blocks = blk_exp.shape[0]
    wspec = lambda r, c: pl.BlockSpec((None, r, c), lambda b, be, nu: (be[b], 0, 0))
    tok = row_tok.reshape(n_blocks, 1, tb)
    return pl.pallas_call(
        functools.partial(_expert_kernel, tb=tb),
        grid_spec=pltpu.PrefetchScalarGridSpec(
            num_scalar_prefetch=2,
            grid=(n_blocks,),
            in_specs=[pl.BlockSpec((1, 1, tb), lambda b, be, nu: (b, 0, 0), memory_space=pltpu.SMEM),
                      pl.BlockSpec((1, 1, tb), lambda b, be, nu: (jnp.minimum(b + 1, n_blocks - 1), 0, 0),
                                   memory_space=pltpu.SMEM),
                      pl.BlockSpec(memory_space=pl.ANY),
                      wspec(D_MODEL, 2 * D_FF), wspec(1, 2 * D_FF), wspec(D_FF, D_MODEL), wspec(1, D_MODEL)],
            out_specs=pl.BlockSpec((tb * ROW_TILES, LANES), lambda b, be, nu: (b, 0)),
            scratch_shapes=[pltpu.VMEM((tb * ROW_TILES, LANES), jnp.float32),
                            pltpu.VMEM((tb * ROW_TILES, LANES), jnp.float32),
                            pltpu.VMEM((D_MODEL, 2 * D_FF), MXU_DTYPE),
                            pltpu.VMEM((D_FF, D_MODEL), MXU_DTYPE),
                            pltpu.SemaphoreType.DMA((2,))]),
        out_shape=jax.ShapeDtypeStruct((n_blocks * tb * ROW_TILES, LANES), jnp.float32),
        compiler_params=_cparams(1),
        name="moe_experts",
    )(blk_exp, n_used, tok, tok, x1_rows, w_gu, b_gu, w_down, b_down)


def _combine_kernel(dest_ref, dest_next_ref, y_hbm, gate_ref, x1_ref, p_ref, wpg_ref, wpp_ref, g_ref, b_ref,
                    o_ref, ybuf0, ybuf1, sem, *, tm):
    i = pl.program_id(0)
    bufs = (ybuf0, ybuf1)

    @pl.when(i == 0)
    def _():
        _gather_rows(dest_ref, TOP_K * tm, y_hbm, ybuf0, sem.at[0])

    for slot in range(2):
        @pl.when((i + 1 < pl.num_programs(0)) & ((i + 1) % 2 == slot))
        def _():
            _gather_rows(dest_next_ref, TOP_K * tm, y_hbm, bufs[slot], sem.at[slot])

    x1 = _rows_as_matrix(x1_ref, 0, tm)
    ple = jax.nn.sigmoid(_mm(x1, wpg_ref[...])) * _mm(p_ref[...], wpp_ref[...])
    h = DN_ALPHA * x1 + ple
    gate = gate_ref[...]

    def finish(ybuf, slot):
        _wait_rows(y_hbm, ybuf, sem.at[slot])
        ffn = h
        for kk in range(TOP_K):
            ffn = ffn + _rows_as_matrix(ybuf, kk * tm, tm) * gate[:, kk:kk + 1]
        o_ref[...] = _layer_norm(ffn, g_ref[...], b_ref[...])

    for slot in range(2):
        pl.when(i % 2 == slot)(functools.partial(finish, bufs[slot], slot))


def _combine(dest, yr, gates, x1_rows, p, wpg, wpp, g, b, tm=256):
    n = p.shape[0]
    nt = n // tm
    full = lambda shape: pl.BlockSpec(shape, lambda i: (0,) * len(shape))
    row = lambda w: pl.BlockSpec((tm, w), lambda i: (i, 0))
    dest_t = dest.reshape(nt, tm, TOP_K).transpose(0, 2, 1).reshape(nt, 1, TOP_K * tm)
    return pl.pallas_call(
        functools.partial(_combine_kernel, tm=tm),
        grid=(nt,),
        in_specs=[pl.BlockSpec((1, 1, TOP_K * tm), lambda i: (i, 0, 0), memory_space=pltpu.SMEM),
                  pl.BlockSpec((1, 1, TOP_K * tm), lambda i: (jnp.minimum(i + 1, nt - 1), 0, 0),
                               memory_space=pltpu.SMEM),
                  pl.BlockSpec(memory_space=pl.ANY),
                  row(LANES), pl.BlockSpec((tm * ROW_TILES, LANES), lambda i: (i, 0)), row(PLE_DIM),
                  full((D_MODEL, D_MODEL)), full((PLE_DIM, D_MODEL)),
                  full((1, D_MODEL)), full((1, D_MODEL))],
        out_specs=row(D_MODEL),
        out_shape=jax.ShapeDtypeStruct((n, D_MODEL), jnp.float32),
        scratch_shapes=[pltpu.VMEM((TOP_K * tm * ROW_TILES, LANES), jnp.float32),
                        pltpu.VMEM((TOP_K * tm * ROW_TILES, LANES), jnp.float32),
                        pltpu.SemaphoreType.DMA((2,))],
        compiler_params=_cparams(1),
        name="combine_ple_ln2",
    )(dest_t, dest_t, yr, gates, x1_rows, p, wpg, wpp, g, b)


def _route(top_idx, n_tokens, tb):
    a = n_tokens * TOP_K
    n_blocks = a // tb + N_EXPERTS
    onehot = (top_idx[:, :, None] == jnp.arange(N_EXPERTS, dtype=jnp.int32)[None, None, :]).astype(jnp.int32)
    member = onehot.sum(axis=1)
    rank = jnp.cumsum(member, axis=0) - member
    counts = member.sum(axis=0)
    padded = (counts + tb - 1) // tb * tb
    pend = jnp.cumsum(padded)
    pstart = pend - padded
    dest = jnp.take_along_axis(rank + pstart[None, :], top_idx, axis=1).astype(jnp.int32)
    tok = jnp.broadcast_to(jnp.arange(n_tokens, dtype=jnp.int32)[:, None], (n_tokens, TOP_K))
    row_tok = jnp.zeros((n_blocks * tb,), jnp.int32).at[dest.reshape(-1)].set(tok.reshape(-1))
    blk_start = jnp.arange(n_blocks, dtype=jnp.int32) * tb
    blk_exp = jnp.minimum((blk_start[:, None] >= pend[None, :]).astype(jnp.int32).sum(axis=1), N_EXPERTS - 1)
    n_used = (pend[-1] // tb).astype(jnp.int32).reshape(1)
    return dest, row_tok, blk_exp, n_used


def kernel(x, p, ln0_g, ln0_b, w_in, pool_w, pool_scale, w_up_attn, w_up_pool, w_out, ln1_g, ln1_b,
           router_w, router_b, exp_w_gu, exp_b_gu, exp_w_down, exp_b_down, ple_w_gate, ple_w_proj,
           ln2_g, ln2_b):
    batch, seq, d = x.shape
    assert d == D_MODEL
    n = batch * seq
    tb = 256
    kc = 512
    tables = _rope_tables(seq)
    bf = lambda a: a.astype(MXU_DTYPE)
    vec = lambda a: a.reshape(1, -1)
    n_le = DEPTH * N_EXPERTS
    w_gu_all = exp_w_gu.reshape(n_le, D_MODEL, 2 * D_FF)
    b_gu_all = exp_b_gu.reshape(n_le, 1, 2 * D_FF)
    w_down_all = exp_w_down.reshape(n_le, D_FF, D_MODEL)
    b_down_all = exp_b_down.reshape(n_le, 1, D_MODEL)

    h = _ln0(x.reshape(n, d), ln0_g, ln0_b)
    for i in range(DEPTH):
        w = w_in[i]
        w1 = bf(jnp.concatenate([w[:, :1352], jnp.zeros((d, OFF_U - 1352), w.dtype), w[:, 1352:1864]], axis=1))
        wg = bf(w[:, 1864:])
        q, k, v_t, qi, ki, wi_t, u = _inproj(h, w1, tables, seq, kc)
        y_attn = _attention(qi, wi_t, q, ki, k, v_t, batch, seq, 512, kc)
        rw = bf(jnp.pad(router_w[i], ((0, 0), (0, LANES - N_EXPERTS))))
        rb = jnp.pad(router_b[i], (0, LANES - N_EXPERTS)).reshape(1, -1)
        x1_rows, idx_l, gate_l = _merge(h, y_attn, u, wg, bf(w_up_attn[i]), bf(pool_w[i]), vec(pool_scale[i]),
                                        bf(w_up_pool[i]), bf(w_out[i]), vec(ln1_g[i]), vec(ln1_b[i]), rw, rb, seq)
        dest, row_tok, blk_exp, n_used = _route(idx_l[:, :TOP_K], n, tb)
        yr = _experts(blk_exp + i * N_EXPERTS, n_used, row_tok, x1_rows,
                      w_gu_all, b_gu_all, w_down_all, b_down_all, tb)
        h = _combine(dest, yr, gate_l, x1_rows, p[i].reshape(n, PLE_DIM), bf(ple_w_gate[i]), bf(ple_w_proj[i]),
                     vec(ln2_g[i]), vec(ln2_b[i]))
    return h.reshape(batch, seq, d)
```

```python
import functools

import jax
import jax.numpy as jnp
import numpy as np
from jax import lax
from jax.experimental import pallas as pl
from jax.experimental.pallas import tpu as pltpu

MXU_DTYPE = jnp.bfloat16

D_MODEL = 1024
HEAD_DIM = 64
N_HEADS = 8
N_KV_HEADS = 2
ROT_DIM = 16
ROPE_THETA = 500000.0
IDX_HEADS = 8
IDX_DIM = 64
MAX_TOPK = 256
POOL_CH = 512
POOL_WINDOWS = (2, 4, 8, 16)
POOL_GROUP = 128
POOL_HALO = 16
N_EXPERTS = 32
TOP_K = 4
D_FF = 1024
SWIGLU_LIMIT = 7.0
SWIGLU_ALPHA = 1.702
PLE_DIM = 256
LN_EPS = 1e-5
DEPTH = 2
DN_ALPHA = (2 * DEPTH) ** 0.25

LANES = 128
SUBLANES = 8
ROW_TILES = D_MODEL // LANES
assert ROW_TILES == SUBLANES

OFF_Q, OFF_K, OFF_V, OFF_QI, OFF_KIW, OFF_U, W1_COLS = 0, 512, 640, 768, 1280, 1408, 1920

VMEM_LIMIT = 56 * 1024 * 1024
NEG_INF = float("-inf")
KEY_NEG_INF = -2139095041
SEARCH_CAP = 20
HALF_BIAS = 32768
SEARCH_PASSES_PER_CHECK = 4
ACC_ROWS = 80
GATHER_UNROLL = 8

def _cparams(n_axes, flags=None):
    return pltpu.CompilerParams(dimension_semantics=("arbitrary",) * n_axes,
                                vmem_limit_bytes=VMEM_LIMIT, flags=flags)


def _mm(a, b):
    return jnp.dot(a.astype(MXU_DTYPE), b.astype(MXU_DTYPE), preferred_element_type=jnp.float32)


def _mm_nt(a, b):
    return lax.dot_general(a.astype(MXU_DTYPE), b.astype(MXU_DTYPE), (((1,), (1,)), ((), ())),
                           preferred_element_type=jnp.float32)


def _layer_norm(h, g, b):
    mu = jnp.mean(h, axis=-1, keepdims=True)
    c = h - mu
    var = jnp.mean(c * c, axis=-1, keepdims=True)
    return c * lax.rsqrt(var + LN_EPS) * g + b


def _ln0_kernel(x_ref, g_ref, b_ref, o_ref):
    o_ref[...] = _layer_norm(x_ref[...], g_ref[...], b_ref[...])


def _ln0(x, g, b, tm=512):
    n = x.shape[0]
    return pl.pallas_call(
        _ln0_kernel,
        grid=(n // tm,),
        in_specs=[pl.BlockSpec((tm, D_MODEL), lambda i: (i, 0)),
                  pl.BlockSpec((1, D_MODEL), lambda i: (0, 0)),
                  pl.BlockSpec((1, D_MODEL), lambda i: (0, 0))],
        out_specs=pl.BlockSpec((tm, D_MODEL), lambda i: (i, 0)),
        out_shape=jax.ShapeDtypeStruct((n, D_MODEL), jnp.float32),
        compiler_params=_cparams(1),
        name="ln0",
    )(x, g.reshape(1, -1), b.reshape(1, -1))


def _rope_tables(seq):
    pos = jnp.arange(seq, dtype=jnp.float32)
    inv = ROPE_THETA ** (-jnp.arange(0, ROT_DIM, 2, dtype=jnp.float32) / ROT_DIM)
    ang = pos[:, None] * inv[None, :]
    cos, sin = jnp.cos(ang), jnp.sin(ang)
    half = ROT_DIM // 2
    one = jnp.ones((seq, HEAD_DIM - ROT_DIM), jnp.float32)
    zero = jnp.zeros((seq, HEAD_DIM - ROT_DIM), jnp.float32)
    zh = jnp.zeros((seq, half), jnp.float32)
    c64 = jnp.concatenate([cos, cos, one], axis=1)
    s1_64 = jnp.concatenate([-sin, zh, zero], axis=1)
    s2_64 = jnp.concatenate([zh, sin, zero], axis=1)
    ident_c = jnp.ones((seq, HEAD_DIM), jnp.float32)
    ident_s = jnp.zeros((seq, HEAD_DIM), jnp.float32)
    c = jnp.concatenate([c64, c64, c64, ident_c], axis=1)
    s1 = jnp.concatenate([s1_64, s1_64, s1_64, ident_s], axis=1)
    s2 = jnp.concatenate([s2_64, s2_64, s2_64, ident_s], axis=1)
    return c, s1, s2


def _rope_tile(x, c, s1, s2):
    half = ROT_DIM // 2
    return x * c + pltpu.roll(x, LANES - half, axis=1) * s1 + pltpu.roll(x, half, axis=1) * s2


def _inproj_kernel(x_ref, w_ref, c_ref, s1_ref, s2_ref,
                   q_ref, k_ref, v_ref, qi_ref, ki_ref, wi_ref, u_ref):
    xb = x_ref[...].astype(MXU_DTYPE)
    c, s1, s2 = c_ref[:, :LANES], s1_ref[:, :LANES], s2_ref[:, :LANES]
    ck, s1k, s2k = c_ref[:, LANES:], s1_ref[:, LANES:], s2_ref[:, LANES:]

    def seg(off, width):
        return jnp.dot(xb, w_ref[:, off:off + width], preferred_element_type=jnp.float32)

    def roped(z, scale, out_ref):
        for j in range(z.shape[1] // LANES):
            t = _rope_tile(z[:, j * LANES:(j + 1) * LANES], c, s1, s2)
            out_ref[:, j * LANES:(j + 1) * LANES] = (t * scale).astype(out_ref.dtype)

    roped(seg(OFF_Q, 512), HEAD_DIM ** -0.5, q_ref)
    roped(seg(OFF_K, 128), 1.0, k_ref)
    v_ref[0] = seg(OFF_V, 128).T.astype(v_ref.dtype)
    roped(seg(OFF_QI, 512), IDX_DIM ** -0.5, qi_ref)
    kiw = seg(OFF_KIW, 128)
    ki_ref[...] = _rope_tile(kiw, ck, s1k, s2k).astype(ki_ref.dtype)
    wi_ref[...] = (kiw * (IDX_HEADS ** -0.5)).T
    u_ref[...] = seg(OFF_U, 512)


def _inproj(x, w1, tables, seq, tm):
    n = x.shape[0]
    nseq = seq // tm
    c, s1, s2 = tables
    tab_spec = pl.BlockSpec((tm, 2 * LANES), lambda i: (i % nseq, 0))

    def out(width, dtype):
        return pl.BlockSpec((tm, width), lambda i: (i, 0)), jax.ShapeDtypeStruct((n, width), dtype)

    v_t = (pl.BlockSpec((1, LANES, tm), lambda i: (i, 0, 0)), jax.ShapeDtypeStruct((n // tm, LANES, tm), MXU_DTYPE))
    wi_t = (pl.BlockSpec((LANES, tm), lambda i: (0, i)), jax.ShapeDtypeStruct((LANES, n), jnp.float32))
    outs = [out(512, MXU_DTYPE), out(128, MXU_DTYPE), v_t, out(512, MXU_DTYPE),
            out(128, MXU_DTYPE), wi_t, out(512, jnp.float32)]
    return pl.pallas_call(
        _inproj_kernel,
        grid=(n // tm,),
        in_specs=[pl.BlockSpec((tm, D_MODEL), lambda i: (i, 0)),
                  pl.BlockSpec((D_MODEL, W1_COLS), lambda i: (0, 0)),
                  tab_spec, tab_spec, tab_spec],
        out_specs=[o[0] for o in outs],
        out_shape=[o[1] for o in outs],
        compiler_params=_cparams(1),
        name="inproj",
    )(x, w1, c, s1, s2)


def _sortable_key(s):
    k = lax.bitcast_convert_type(s, jnp.int32)
    return k ^ ((k >> 31) & jnp.int32(0x7FFFFFFF))


def _attn_kernel(qi_ref, wi_ref, q_ref, ki_ref, k_ref, v_ref, tri_ref, o_ref,
                 key_ref, half_ref, s_ref, acc_ref, *, tq, kc, n_sel):
    j = pl.program_id(1)
    q0 = j * tq
    nch = (q0 + tq + kc - 1) // kc
    qpos = q0 + lax.broadcasted_iota(jnp.int32, (kc, tq), 1)
    krow = lax.broadcasted_iota(jnp.int32, (kc, tq), 0)
    imax = jnp.iinfo(jnp.int32).max

    def score_chunk(c, carry):
        kmin, kmax = carry
        ks = pl.multiple_of(c * kc, kc)
        kic = ki_ref[pl.ds(ks, kc), :][:, :IDX_DIM]
        acc = jnp.zeros((kc, tq), jnp.float32)
        for h in range(IDX_HEADS):
            d = _mm_nt(kic, qi_ref[:, h * IDX_DIM:(h + 1) * IDX_DIM])
            acc = acc + jnp.maximum(d, 0.0) * wi_ref[IDX_DIM + h:IDX_DIM + h + 1, :]
        causal = krow + ks <= qpos
        key = _sortable_key(jnp.where(causal, acc, NEG_INF))
        key_ref[c] = key
        half_ref[c] = (key >> 16).astype(jnp.int16)
        kmin =jnp.minimum(kmin, jnp.min(jnp.where(causal, key, imax), axis=0, keepdims=True))
        kmax = jnp.maximum(kmax, jnp.max(key, axis=0, keepdims=True))
        return kmin, kmax

    kmin, kmax = lax.fori_loop(0, nch, score_chunk,
                               (jnp.full((1, tq), imax, jnp.int32), jnp.full((1, tq), KEY_NEG_INF, jnp.int32)))

    half_rows = 2 * SUBLANES

    def count_half_ge(cand):
        c16 = cand.astype(jnp.int16)
        c16 = jnp.concatenate([c16, c16], axis=0)
        def body(c, cnt):
            hit = jnp.where(half_ref[c].reshape(kc // half_rows, half_rows, tq) >= c16[None],
                            jnp.int16(1), jnp.int16(0))
            part = hit[0]
            for r in range(1, kc // half_rows):
                part = part + hit[r]
            return cnt + part
        cnt = lax.fori_loop(0, nch, body, jnp.zeros((half_rows, tq), jnp.int16)).astype(jnp.int32)
        cnt = cnt[0:SUBLANES, :] + cnt[SUBLANES:, :]
        for shift in (4, 2, 1):
            cnt = cnt + pltpu.roll(cnt, shift, axis=0)
        return cnt

    def bisect(base, state):
        def one_pass(st):
            lo, hi, c_lo, c_hi, done = st
            active = done == 0
            cand = jnp.where(active, lo + ((hi >> 1) - (lo >> 1)), lo)
            cnt = base + count_half_ge(cand)
            up = active & (cnt >= n_sel)
            dn = active & (cnt < n_sel)
            lo, c_lo = jnp.where(up, cand, lo), jnp.where(up, cnt, c_lo)
            hi, c_hi = jnp.where(dn, cand, hi), jnp.where(dn, cnt, c_hi)
            done = jnp.where((done != 0) | (c_lo == n_sel) | (hi - 1 <= lo), 1, 0)
            return lo, hi, c_lo, c_hi, done

        def step(st):
            inner = st[1:]
            for _ in range(SEARCH_PASSES_PER_CHECK):
                inner = one_pass(inner)
            return (st[0] + SEARCH_PASSES_PER_CHECK,) + inner

        cond = lambda st: (st[0] < SEARCH_CAP) & (jnp.min(st[5]) == 0)
        return lax.while_loop(cond, step, (jnp.int32(0),) + state)[1:]

    rep = lambda a: jnp.broadcast_to(a, (SUBLANES, tq))
    zeros = jnp.zeros((SUBLANES, tq), jnp.int32)
    n_valid = q0 + lax.broadcasted_iota(jnp.int32, (SUBLANES, tq), 1) + 1
    all_selected = n_valid <= n_sel

    h_lo = jnp.where(all_selected, KEY_NEG_INF >> 16, rep(kmin) >> 16)
    h, _, c_ge_h, c_gt_h, _ = bisect(zeros, (h_lo, (rep(kmax) >> 16) + 1, n_valid, zeros,
                                             all_selected.astype(jnp.int32)))
    settled = all_selected | (c_ge_h == n_sel)

    def low_chunk(c, carry):
        key = key_ref[c]
        low = jnp.where((key >> 16) == h[0:1, :], (key & 0xFFFF) - HALF_BIAS, -HALF_BIAS)
        half_ref[c] = low.astype(jnp.int16)
        return carry

    lax.fori_loop(0, nch, low_chunk, 0)
    low, _, c_lo, c_hi, _ = bisect(c_gt_h, (jnp.full((SUBLANES, tq), -HALF_BIAS, jnp.int32),
                                            jnp.full((SUBLANES, tq), HALF_BIAS, jnp.int32),
                                            c_ge_h, c_gt_h, settled.astype(jnp.int32)))
    thr = jnp.where(all_selected, KEY_NEG_INF, h * (2 * HALF_BIAS) + (low + HALF_BIAS))
    thr, c_lo, c_hi = thr[0:1, :], c_lo[0:1, :], c_hi[0:1, :]
    need = jnp.where(c_lo == n_sel, float(2 ** 30), (n_sel - c_hi).astype(jnp.float32))

    def bias_chunk(with_ties, c, n_eq_before):
        ks = pl.multiple_of(c * kc, kc)
        key = key_ref[c]
        causal = krow + ks <= qpos
        if with_ties:
            eq = key == thr
            rank = _mm(tri_ref[...], jnp.where(eq, 1.0, 0.0)) + n_eq_before
            sel = ((key > thr) | (eq & (rank <= need))) & causal
            n_eq_before = rank[kc - 1:kc, :]
        else:
            sel = (key >= thr) & causal
        key_ref[c] = lax.bitcast_convert_type(jnp.where(sel, 0.0, NEG_INF), jnp.int32)
        return n_eq_before

    has_ties = jnp.max(c_lo) > n_sel

    @pl.when(has_ties)
    def _():
        lax.fori_loop(0, nch, functools.partial(bias_chunk, True), jnp.zeros((1, tq), jnp.float32))

    @pl.when(jnp.logical_not(has_ties))
    def _():
        lax.fori_loop(0, nch, functools.partial(bias_chunk, False), jnp.zeros((1, tq), jnp.float32))

    acc_ref[...] = jnp.zeros(acc_ref.shape, jnp.float32)
    group = N_HEADS // N_KV_HEADS
    ones_rows = jnp.ones((ACC_ROWS - HEAD_DIM, kc), MXU_DTYPE)

    def attend_chunk(c, ms):
        ks = pl.multiple_of(c * kc, kc)
        bias = lax.bitcast_convert_type(key_ref[c], jnp.float32)
        kch = k_ref[pl.ds(ks, kc), :]
        vch = v_ref[c]
        vext = [jnp.concatenate([vch[g * HEAD_DIM:(g + 1) * HEAD_DIM, :], ones_rows], axis=0)
                for g in range(N_KV_HEADS)]
        slot0 = jnp.minimum(c, 0)
        for h in range(N_HEADS):
            g = h // group
            s_ref[slot0 + h] = _mm_nt(kch[:, g * HEAD_DIM:(g + 1) * HEAD_DIM],
                                      q_ref[:, h * HEAD_DIM:(h + 1) * HEAD_DIM]) + bias
        new_ms = []
        for h in range(N_HEADS):
            m_new = jnp.maximum(ms[h], jnp.max(s_ref[slot0 + h], axis=0, keepdims=True))
            p = jnp.exp(s_ref[slot0 + h] - m_new)
            acc_ref[h] = jnp.exp(ms[h] - m_new) * acc_ref[h] + _mm(vext[h // group], p)
            new_ms.append(m_new)
        return tuple(new_ms)

    lax.fori_loop(0, nch, attend_chunk, tuple(jnp.full((1, tq), -1e30, jnp.float32) for _ in range(N_HEADS)))

    out_t = jnp.concatenate([acc_ref[h, 0:HEAD_DIM, :] / acc_ref[h, HEAD_DIM:HEAD_DIM + 1, :]
                             for h in range(N_HEADS)], axis=0)
    o_ref[...] = out_t.T.astype(o_ref.dtype)


def _attention(qi, wi_t, q, ki, k, v_t, batch, seq, tq, kc):
    n = q.shape[0]
    n_sel = min(MAX_TOPK, seq // 4)
    nq = seq // tq
    nck = seq // kc
    assert seq % tq == 0 and seq % kc == 0 and kc >= n_sel and v_t.shape == (n // kc, LANES, kc)
    tri = jnp.asarray((np.arange(kc)[None, :] <= np.arange(kc)[:, None]).astype(np.float32), MXU_DTYPE)
    qblk = lambda w: pl.BlockSpec((tq, w), lambda b, j: (b * nq + j, 0))
    seqblk = pl.BlockSpec((seq, LANES), lambda b, j: (b, 0))
    return pl.pallas_call(
        functools.partial(_attn_kernel, tq=tq, kc=kc, n_sel=n_sel),
        grid=(batch, nq),
        in_specs=[qblk(512), pl.BlockSpec((LANES, tq), lambda b, j: (0, b * nq + j)), qblk(512),
                  seqblk, seqblk, pl.BlockSpec((nck, LANES, kc), lambda b, j: (b, 0, 0)),
                  pl.BlockSpec((kc, kc), lambda b, j: (0, 0))],
        out_specs=qblk(512),
        out_shape=jax.ShapeDtypeStruct((n, N_HEADS * HEAD_DIM), MXU_DTYPE),
        scratch_shapes=[pltpu.VMEM((nck, kc, tq), jnp.int32),
                        pltpu.VMEM((nck, kc, tq), jnp.int16),
                        pltpu.VMEM((N_HEADS, kc, tq), jnp.float32),
                        pltpu.VMEM((N_HEADS, ACC_ROWS, tq), jnp.float32)],
        compiler_params=_cparams(2),
        name="dsa_attention",
    )(qi, wi_t, q, ki, k, v_t, tri)


def _merge_kernel(x_ref, ya_ref, u_ref, uh_ref, wg_ref, wua_ref, pw_ref, ps_ref, wup_ref, wo_ref,
                  g_ref, b_ref, rw_ref, rb_ref,
                  x1_ref, idx_ref, gate_ref, ext_ref, *, tm, seq):
    i = pl.program_id(0)
    x = x_ref[...]
    xb = x.astype(MXU_DTYPE)

    first = (i % (seq // tm)) == 0
    ext_ref[0:POOL_HALO, :] = jnp.where(first, 0.0, uh_ref[...])
    ext_ref[POOL_HALO:, :] = u_ref[...]
    pos1 = ((i % (seq // tm)) * tm + 1 + lax.broadcasted_iota(jnp.int32, (tm, 1), 0)).astype(jnp.float32)
    parts = []
    for g, win in enumerate(POOL_WINDOWS):
        e = ext_ref[:, g * POOL_GROUP:(g + 1) * POOL_GROUP]
        w = 1
        while w < win:
            e = e + pltpu.roll(e, w, axis=0)
            w *= 2
        tok = e[POOL_HALO:, :]
        ug = u_ref[:, g * POOL_GROUP:(g + 1) * POOL_GROUP]
        d = tok / jnp.minimum(pos1, float(win)) - ug
        parts.append(_mm(d, pw_ref[g]))
    y_pool = jnp.concatenate(parts, axis=1) * ps_ref[...]

    gates = jnp.dot(xb, wg_ref[...], preferred_element_type=jnp.float32)
    merged = (jax.nn.sigmoid(gates[:, :D_MODEL]) * _mm(ya_ref[...], wua_ref[...])
              + jax.nn.sigmoid(gates[:, D_MODEL:]) * _mm(y_pool, wup_ref[...]))
    mix = _mm(merged, wo_ref[...])
    x1 = _layer_norm(DN_ALPHA * x + mix, g_ref[...], b_ref[...])
    _store_rows(x1_ref, x1)

    logits = _mm(x1, rw_ref[...]) + rb_ref[...]
    lane = lax.broadcasted_iota(jnp.int32, logits.shape, 1)
    work = jnp.where(lane < N_EXPERTS, logits, NEG_INF)
    vals, idxs = [], []
    for _ in range(TOP_K):
        m = jnp.max(work, axis=1, keepdims=True)
        ix = jnp.min(jnp.where(work == m, lane, LANES), axis=1, keepdims=True)
        vals.append(m)
        idxs.append(ix)
        work = jnp.where(lane == ix, NEG_INF, work)
    es = [jnp.exp(vv - vals[0]) for vv in vals]
    den = es[0] + es[1] + es[2] + es[3]
    idx_out = jnp.zeros(logits.shape, jnp.int32)
    gate_out = jnp.zeros(logits.shape, jnp.float32)
    for kk in range(TOP_K):
        idx_out = jnp.where(lane == kk, idxs[kk], idx_out)
        gate_out = jnp.where(lane == kk, es[kk] / den, gate_out)
    idx_ref[...] = idx_out
    gate_ref[...] = gate_out


def _merge(x, y_attn, u, wg, wua, pw, ps, wup, wo, g, b, rw, rb, seq, tm=256):
    n = x.shape[0]
    hb = tm // POOL_HALO
    full = lambda shape: pl.BlockSpec(shape, lambda i: (0,) * len(shape))
    row = lambda w: pl.BlockSpec((tm, w), lambda i: (i, 0))
    return pl.pallas_call(
        functools.partial(_merge_kernel, tm=tm, seq=seq),
        grid=(n // tm,),
        in_specs=[row(D_MODEL), row(512), row(POOL_CH),
                  pl.BlockSpec((POOL_HALO, POOL_CH), lambda i: (jnp.maximum(i * hb - 1, 0), 0)),
                  full((D_MODEL, 2 * D_MODEL)), full((512, D_MODEL)),
                  full((4, POOL_GROUP, POOL_GROUP)), full((1, POOL_CH)), full((POOL_CH, D_MODEL)),
                  full((D_MODEL, D_MODEL)), full((1, D_MODEL)), full((1, D_MODEL)),
                  full((D_MODEL, LANES)), full((1, LANES))],
        out_specs=[pl.BlockSpec((tm * ROW_TILES, LANES), lambda i: (i, 0)), row(LANES), row(LANES)],
        out_shape=[jax.ShapeDtypeStruct((n * ROW_TILES, LANES), jnp.float32),
                   jax.ShapeDtypeStruct((n, LANES), jnp.int32),
                   jax.ShapeDtypeStruct((n, LANES), jnp.float32)],
        scratch_shapes=[pltpu.VMEM((tm + POOL_HALO, POOL_CH), jnp.float32)],
        compiler_params=_cparams(1),
        name="merge_ln1_router",
    )(x, y_attn, u, u, wg, wua, pw, ps, wup, wo, g, b, rw, rb)


def _gather_rows(idx_ref, n_rows, table_hbm, buf_ref, sem):
    def issue(i, carry):
        for j in range(GATHER_UNROLL):
            r = i * GATHER_UNROLL + j
            t = idx_ref[0, 0, r]
            pltpu.make_async_copy(table_hbm.at[pl.ds(pl.multiple_of(t * ROW_TILES, ROW_TILES), ROW_TILES), :],
                                  buf_ref.at[pl.ds(pl.multiple_of(r * ROW_TILES, ROW_TILES), ROW_TILES), :],
                                  sem).start()
        return carry
    assert n_rows % GATHER_UNROLL == 0
    lax.fori_loop(0, n_rows // GATHER_UNROLL, issue, 0)


def _wait_rows(table_hbm, buf_ref, sem):
    pltpu.make_async_copy(table_hbm.at[pl.ds(0, buf_ref.shape[0]), :], buf_ref, sem).wait()


def _rows_as_matrix(buf_ref, r0, n_rows):
    return jnp.concatenate(
        [buf_ref[pl.ds(r0 * ROW_TILES + c, n_rows, stride=ROW_TILES), :] for c in range(ROW_TILES)], axis=1)


def _store_rows(out_ref, y):
    for c in range(ROW_TILES):
        out_ref[pl.ds(c, y.shape[0], stride=ROW_TILES), :] = y[:, c * LANES:(c + 1) * LANES]


def _expert_kernel(be_ref, nu_ref, tok_ref, tok_next_ref, x_hbm, wgu_ref, bgu_ref, wd_ref, bd_ref, y_ref,
                   xbuf0, xbuf1, wgu_bf, wd_bf, sem, *, tb):
    b = pl.program_id(0)
    n_used = nu_ref[0]
    bufs = (xbuf0, xbuf1)

    @pl.when(b == 0)
    def _():
        _gather_rows(tok_ref, tb, x_hbm, xbuf0, sem.at[0])

    for slot in range(2):
        @pl.when((b + 1 < n_used) & ((b + 1) % 2 == slot))
        def _():
            _gather_rows(tok_next_ref, tb, x_hbm, bufs[slot], sem.at[slot])

    @pl.when(b < n_used)
    def _():
        e = be_ref[b]
        e_prev = be_ref[jnp.maximum(b - 1, 0)]

        @pl.when((b == 0) | (e != e_prev))
        def _():
            wgu_bf[...] = wgu_ref[...].astype(MXU_DTYPE)
            wd_bf[...] = wd_ref[...].astype(MXU_DTYPE)

        def compute(xbuf, slot):
            _wait_rows(x_hbm, xbuf, sem.at[slot])
            xr = _rows_as_matrix(xbuf, 0, tb)
            gu = _mm(xr, wgu_bf[...]) + bgu_ref[...]
            gt = jnp.minimum(gu[:, :D_FF], SWIGLU_LIMIT)
            up = jnp.clip(gu[:, D_FF:], -SWIGLU_LIMIT, SWIGLU_LIMIT)
            act = gt * jax.nn.sigmoid(SWIGLU_ALPHA * gt) * (up + 1.0)
            _store_rows(y_ref, _mm(act, wd_bf[...]) + bd_ref[...])

        for slot in range(2):
            pl.when(b % 2 == slot)(functools.partial(compute, bufs[slot], slot))

    @pl.when(b >= n_used)
    def _():
        y_ref[...] = jnp.zeros(y_ref.shape, y_ref.dtype)


def _experts(blk_exp, n_used, row_tok, x1_rows, w_gu, b_gu, w_down, b_down, tb):
    n_blocks = blk_exp.shape[0]
    wspec = lambda r, c: pl.BlockSpec((None, r, c), lambda b, be, nu: (be[b], 0, 0))
    tok = row_tok.reshape(n_blocks, 1, tb)
    return pl.pallas_call(
        functools.partial(_expert_kernel, tb=tb),
        grid_spec=pltpu.PrefetchScalarGridSpec(
            num_scalar_prefetch=2,
            grid=(n_blocks,),
            in_specs=[pl.BlockSpec((1, 1, tb), lambda b, be, nu: (b, 0, 0), memory_space=pltpu.SMEM),
                      pl.BlockSpec((1, 1, tb), lambda b, be, nu: (jnp.minimum(b + 1, n_blocks - 1), 0, 0),
                                   memory_space=pltpu.SMEM),
                      pl.BlockSpec(memory_space=pl.ANY),
                      wspec(D_MODEL, 2 * D_FF), wspec(1, 2 * D_FF), wspec(D_FF, D_MODEL), wspec(1, D_MODEL)],
            out_specs=pl.BlockSpec((tb * ROW_TILES, LANES), lambda b, be, nu: (b, 0)),
            scratch_shapes=[pltpu.VMEM((tb * ROW_TILES, LANES), jnp.float32),
                            pltpu.VMEM((tb * ROW_TILES, LANES), jnp.float32),
                            pltpu.VMEM((D_MODEL, 2 * D_FF), MXU_DTYPE),
                            pltpu.VMEM((D_FF, D_MODEL), MXU_DTYPE),
                            pltpu.SemaphoreType.DMA((2,))]),
        out_shape=jax.ShapeDtypeStruct((n_blocks * tb * ROW_TILES, LANES), jnp.float32),
        compiler_params=_cparams(1),
        name="moe_experts",
    )(blk_exp, n_used, tok, tok, x1_rows, w_gu, b_gu, w_down, b_down)


def _combine_kernel(dest_ref, dest_next_ref, y_hbm, gate_ref, x1_ref, p_ref, wpg_ref, wpp_ref, g_ref, b_ref,
                    o_ref, ybuf0, ybuf1, sem, *, tm):
    i = pl.program_id(0)
    bufs = (ybuf0, ybuf1)

    @pl.when(i == 0)
    def _():
        _gather_rows(dest_ref, TOP_K * tm, y_hbm, ybuf0, sem.at[0])

    for slot in range(2):
        @pl.when((i + 1 < pl.num_programs(0)) & ((i + 1) % 2 == slot))
        def _():
            _gather_rows(dest_next_ref, TOP_K * tm, y_hbm, bufs[slot], sem.at[slot])

    x1 = _rows_as_matrix(x1_ref, 0, tm)
    ple = jax.nn.sigmoid(_mm(x1, wpg_ref[...])) * _mm(p_ref[...], wpp_ref[...])
    h = DN_ALPHA * x1 + ple
    gate = gate_ref[...]

    def finish(ybuf, slot):
        _wait_rows(y_hbm, ybuf, sem.at[slot])
        ffn = h
        for kk in range(TOP_K):
            ffn = ffn + _rows_as_matrix(ybuf, kk * tm, tm) * gate[:, kk:kk + 1]
        o_ref[...] = _layer_norm(ffn, g_ref[...], b_ref[...])

    for slot in range(2):
        pl.when(i % 2 == slot)(functools.partial(finish, bufs[slot], slot))


def _combine(dest, yr, gates, x1_rows, p, wpg, wpp, g, b, tm=256):
    n = p.shape[0]
    nt = n // tm
    full = lambda shape: pl.BlockSpec(shape, lambda i: (0,) * len(shape))
    row = lambda w: pl.BlockSpec((tm, w), lambda i: (i, 0))
    dest_t = dest.reshape(nt, tm, TOP_K).transpose(0, 2, 1).reshape(nt, 1, TOP_K * tm)
    return pl.pallas_call(
        functools.partial(_combine_kernel, tm=tm),
        grid=(nt,),
        in_specs=[pl.BlockSpec((1, 1, TOP_K * tm), lambda i: (i, 0, 0), memory_space=pltpu.SMEM),
                  pl.BlockSpec((1, 1, TOP_K * tm), lambda i: (jnp.minimum(i + 1, nt - 1), 0, 0),
                               memory_space=pltpu.SMEM),
                  pl.BlockSpec(memory_space=pl.ANY),
                  row(LANES), pl.BlockSpec((tm * ROW_TILES, LANES), lambda i: (i, 0)), row(PLE_DIM),
                  full((D_MODEL, D_MODEL)), full((PLE_DIM, D_MODEL)),
                  full((1, D_MODEL)), full((1, D_MODEL))],
        out_specs=row(D_MODEL),
        out_shape=jax.ShapeDtypeStruct((n, D_MODEL), jnp.float32),
        scratch_shapes=[pltpu.VMEM((TOP_K * tm * ROW_TILES, LANES), jnp.float32),
                        pltpu.VMEM((TOP_K * tm * ROW_TILES, LANES), jnp.float32),
                        pltpu.SemaphoreType.DMA((2,))],
        compiler_params=_cparams(1),
        name="combine_ple_ln2",
    )(dest_t, dest_t, yr, gates, x1_rows, p, wpg, wpp, g, b)


def _route(top_idx, n_tokens, tb):
    a = n_tokens * TOP_K
    n_blocks = a // tb + N_EXPERTS
    onehot = (top_idx[:, :, None] == jnp.arange(N_EXPERTS, dtype=jnp.int32)[None, None, :]).astype(jnp.int32)
    member = onehot.sum(axis=1)
    rank = jnp.cumsum(member, axis=0) - member
    counts = member.sum(axis=0)
    padded = (counts + tb - 1) // tb * tb
    pend = jnp.cumsum(padded)
    pstart = pend - padded
    dest = jnp.take_along_axis(rank + pstart[None, :], top_idx, axis=1).astype(jnp.int32)
    tok = jnp.broadcast_to(jnp.arange(n_tokens, dtype=jnp.int32)[:, None], (n_tokens, TOP_K))
    row_tok = jnp.zeros((n_blocks * tb,), jnp.int32).at[dest.reshape(-1)].set(tok.reshape(-1))
    blk_start = jnp.arange(n_blocks, dtype=jnp.int32) * tb
    blk_exp = jnp.minimum((blk_start[:, None] >= pend[None, :]).astype(jnp.int32).sum(axis=1), N_EXPERTS - 1)
    n_used = (pend[-1] // tb).astype(jnp.int32).reshape(1)
    return dest, row_tok, blk_exp, n_used


def kernel(x, p, ln0_g, ln0_b, w_in, pool_w, pool_scale, w_up_attn, w_up_pool, w_out, ln1_g, ln1_b,
           router_w, router_b, exp_w_gu, exp_b_gu, exp_w_down, exp_b_down, ple_w_gate, ple_w_proj,
           ln2_g, ln2_b):
    batch, seq, d = x.shape
    assert d == D_MODEL
    n = batch * seq
    tb = 512
    kc = 512
    tables = _rope_tables(seq)
    bf = lambda a: a.astype(MXU_DTYPE)
    vec = lambda a: a.reshape(1, -1)
    n_le = DEPTH * N_EXPERTS
    w_gu_all = exp_w_gu.reshape(n_le, D_MODEL, 2 * D_FF)
    b_gu_all = exp_b_gu.reshape(n_le, 1, 2 * D_FF)
    w_down_all = exp_w_down.reshape(n_le, D_FF, D_MODEL)
    b_down_all = exp_b_down.reshape(n_le, 1, D_MODEL)

    h = _ln0(x.reshape(n, d), ln0_g, ln0_b)
    for i in range(DEPTH):
        w = w_in[i]
        w1 = bf(jnp.concatenate([w[:, :1352], jnp.zeros((d, OFF_U - 1352), w.dtype), w[:, 1352:1864]], axis=1))
        wg = bf(w[:, 1864:])
        q, k, v_t, qi, ki, wi_t, u = _inproj(h, w1, tables, seq, kc)
        y_attn = _attention(qi, wi_t, q, ki, k, v_t, batch, seq, 512, kc)
        rw = bf(jnp.pad(router_w[i], ((0, 0), (0, LANES - N_EXPERTS))))
        rb = jnp.pad(router_b[i], (0, LANES - N_EXPERTS)).reshape(1, -1)
        x1_rows, idx_l, gate_l = _merge(h, y_attn, u, wg, bf(w_up_attn[i]), bf(pool_w[i]), vec(pool_scale[i]),
                                        bf(w_up_pool[i]), bf(w_out[i]), vec(ln1_g[i]), vec(ln1_b[i]), rw, rb, seq)
        dest, row_tok, blk_exp, n_used = _route(idx_l[:, :TOP_K], n, tb)
        yr = _experts(blk_exp + i * N_EXPERTS, n_used, row_tok, x1_rows,
                      w_gu_all, b_gu_all, w_down_all, b_down_all, tb)
        h = _combine(dest, yr, gate_l, x1_rows, p[i].reshape(n, PLE_DIM), bf(ple_w_gate[i]), bf(ple_w_proj[i]),
                     vec(ln2_g[i]), vec(ln2_b[i]))
    return h.reshape(batch, seq, d)
```

```python
import functools

import jax
import jax.numpy as jnp
import numpy as np
from jax import lax
from jax.experimental import pallas as pl
from jax.experimental.pallas import tpu as pltpu

MXU_DTYPE = jnp.bfloat16

D_MODEL = 1024
HEAD_DIM = 64
N_HEADS = 8
N_KV_HEADS = 2
ROT_DIM = 16
ROPE_THETA = 500000.0
IDX_HEADS = 8
IDX_DIM = 64
MAX_TOPK = 256
POOL_CH = 512
POOL_WINDOWS = (2, 4, 8, 16)
POOL_GROUP = 128
POOL_HALO = 16
N_EXPERTS = 32
TOP_K = 4
D_FF = 1024
SWIGLU_LIMIT = 7.0
SWIGLU_ALPHA = 1.702
PLE_DIM = 256
LN_EPS = 1e-5
DEPTH = 2
DN_ALPHA = (2 * DEPTH) ** 0.25
LOG2_E = 1.4426950408889634

LANES = 128
SUBLANES = 8
ROW_TILES = D_MODEL // LANES
assert ROW_TILES == SUBLANES

OFF_Q, OFF_K, OFF_V, OFF_QI, OFF_KIW, OFF_U, W1_COLS = 0, 512, 640, 768, 1280, 1408, 1920

VMEM_LIMIT = 56 * 1024 * 1024
NEG_INF = float("-inf")
KEY_NEG_INF = -2139095041
SEARCH_CAP = 20
HALF_BIAS = 32768
SEARCH_PASSES_PER_CHECK = 4
ACC_ROWS = 80
GATHER_UNROLL = 8

def _cparams(n_axes, flags=None):
    return pltpu.CompilerParams(dimension_semantics=("arbitrary",) * n_axes,
                                vmem_limit_bytes=VMEM_LIMIT, flags=flags)


def _mm(a, b):
    return jnp.dot(a.astype(MXU_DTYPE), b.astype(MXU_DTYPE), preferred_element_type=jnp.float32)


def _mm_nt(a, b):
    return lax.dot_general(a.astype(MXU_DTYPE), b.astype(MXU_DTYPE), (((1,), (1,)), ((), ())),
                           preferred_element_type=jnp.float32)


def _layer_norm(h, g, b):
    mu = jnp.mean(h, axis=-1, keepdims=True)
    c = h - mu
    var = jnp.mean(c * c, axis=-1, keepdims=True)
    return c * lax.rsqrt(var + LN_EPS) * g + b


def _ln0_kernel(x_ref, g_ref, b_ref, o_ref):
    o_ref[...] = _layer_norm(x_ref[...], g_ref[...], b_ref[...])


def _ln0(x, g, b, tm=512):
    n = x.shape[0]
    return pl.pallas_call(
        _ln0_kernel,
        grid=(n // tm,),
        in_specs=[pl.BlockSpec((tm, D_MODEL), lambda i: (i, 0)),
                  pl.BlockSpec((1, D_MODEL), lambda i: (0, 0)),
                  pl.BlockSpec((1, D_MODEL), lambda i: (0, 0))],
        out_specs=pl.BlockSpec((tm, D_MODEL), lambda i: (i, 0)),
        out_shape=jax.ShapeDtypeStruct((n, D_MODEL), jnp.float32),
        compiler_params=_cparams(1),
        name="ln0",
    )(x, g.reshape(1, -1), b.reshape(1, -1))


def _rope_tables(seq):
    pos = jnp.arange(seq, dtype=jnp.float32)
    inv = ROPE_THETA ** (-jnp.arange(0, ROT_DIM, 2, dtype=jnp.float32) / ROT_DIM)
    ang = pos[:, None] * inv[None, :]
    cos, sin = jnp.cos(ang), jnp.sin(ang)
    half = ROT_DIM // 2
    one = jnp.ones((seq, HEAD_DIM - ROT_DIM), jnp.float32)
    zero = jnp.zeros((seq, HEAD_DIM - ROT_DIM), jnp.float32)
    zh = jnp.zeros((seq, half), jnp.float32)
    c64 = jnp.concatenate([cos, cos, one], axis=1)
    s1_64 = jnp.concatenate([-sin, zh, zero], axis=1)
    s2_64 = jnp.concatenate([zh, sin, zero], axis=1)
    ident_c = jnp.ones((seq, HEAD_DIM), jnp.float32)
    ident_s = jnp.zeros((seq, HEAD_DIM), jnp.float32)
    c = jnp.concatenate([c64, c64, c64, ident_c], axis=1)
    s1 = jnp.concatenate([s1_64, s1_64, s1_64, ident_s], axis=1)
    s2 = jnp.concatenate([s2_64, s2_64, s2_64, ident_s], axis=1)
    return c, s1, s2


def _rope_tile(x, c, s1, s2):
    half = ROT_DIM // 2
    return x * c + pltpu.roll(x, LANES - half, axis=1) * s1 + pltpu.roll(x, half, axis=1) * s2


def _inproj_kernel(x_ref, w_ref, c_ref, s1_ref, s2_ref,
                   q_ref, k_ref, v_ref, qi_ref, ki_ref, wi_ref, u_ref):
    xb = x_ref[...].astype(MXU_DTYPE)
    c, s1, s2 = c_ref[:, :LANES], s1_ref[:, :LANES], s2_ref[:, :LANES]
    ck, s1k, s2k = c_ref[:, LANES:], s1_ref[:, LANES:], s2_ref[:, LANES:]

    def seg(off, width):
        return jnp.dot(xb, w_ref[:, off:off + width], preferred_element_type=jnp.float32)

    def roped(z, scale, out_ref):
        for j in range(z.shape[1] // LANES):
            t = _rope_tile(z[:, j * LANES:(j + 1) * LANES], c, s1, s2)
            out_ref[:, j * LANES:(j + 1) * LANES] = (t * scale).astype(out_ref.dtype)

    roped(seg(OFF_Q, 512), HEAD_DIM ** -0.5 * LOG2_E, q_ref)
    roped(seg(OFF_K, 128), 1.0, k_ref)
    v_ref[0] = seg(OFF_V, 128).T.astype(v_ref.dtype)
    roped(seg(OFF_QI, 512), IDX_DIM ** -0.5, qi_ref)
    kiw = seg(OFF_KIW, 128)
    ki_ref[...] = _rope_tile(kiw, ck, s1k, s2k).astype(ki_ref.dtype)
    wi_ref[...] = (kiw * (IDX_HEADS ** -0.5)).T
    u_ref[...] = seg(OFF_U, 512)


def _inproj(x, w1, tables, seq, tm):
    n = x.shape[0]
    nseq = seq // tm
    c, s1, s2 = tables
    tab_spec = pl.BlockSpec((tm, 2 * LANES), lambda i: (i % nseq, 0))

    def out(width, dtype):
        return pl.BlockSpec((tm, width), lambda i: (i, 0)), jax.ShapeDtypeStruct((n, width), dtype)

    v_t = (pl.BlockSpec((1, LANES, tm), lambda i: (i, 0, 0)), jax.ShapeDtypeStruct((n // tm, LANES, tm), MXU_DTYPE))
    wi_t = (pl.BlockSpec((LANES, tm), lambda i: (0, i)), jax.ShapeDtypeStruct((LANES, n), jnp.float32))
    outs = [out(512, MXU_DTYPE), out(128, MXU_DTYPE), v_t, out(512, MXU_DTYPE),
            out(128, MXU_DTYPE), wi_t, out(512, jnp.float32)]
    return pl.pallas_call(
        _inproj_kernel,
        grid=(n // tm,),
        in_specs=[pl.BlockSpec((tm, D_MODEL), lambda i: (i, 0)),
                  pl.BlockSpec((D_MODEL, W1_COLS), lambda i: (0, 0)),
                  tab_spec, tab_spec, tab_spec],
        out_specs=[o[0] for o in outs],
        out_shape=[o[1] for o in outs],
        compiler_params=_cparams(1),
        name="inproj",
    )(x, w1, c, s1, s2)


def _sortable_key(s):
    k = lax.bitcast_convert_type(s, jnp.int32)
    return k ^ ((k >> 31) & jnp.int32(0x7FFFFFFF))


def _attn_kernel(qi_ref, wi_ref, q_ref, ki_ref, k_ref, v_ref, tri_ref, o_ref,
                 key_ref, half_ref, s_ref, acc_ref, *, tq, kc, n_sel):
    j = pl.program_id(1)
    q0 = j * tq
    nch = (q0 + tq + kc - 1) // kc
    qpos = q0 + lax.broadcasted_iota(jnp.int32, (kc, tq), 1)
    krow = lax.broadcasted_iota(jnp.int32, (kc, tq), 0)
    imax = jnp.iinfo(jnp.int32).max

    def score_chunk(c, carry):
        kmin, kmax = carry
        ks = pl.multiple_of(c * kc, kc)
        kic = ki_ref[pl.ds(ks, kc), :][:, :IDX_DIM]
        acc = jnp.zeros((kc, tq), jnp.float32)
        for h in range(IDX_HEADS):
            d = _mm_nt(kic, qi_ref[:, h * IDX_DIM:(h + 1) * IDX_DIM])
            acc = acc + jnp.maximum(d, 0.0) * wi_ref[IDX_DIM + h:IDX_DIM + h + 1, :]
        causal = krow + ks <= qpos
        key = _sortable_key(jnp.where(causal, acc, NEG_INF))
        key_ref[c] = key
        half_ref[c] = (key >> 16).astype(jnp.int16)
        kmin =jnp.minimum(kmin, jnp.min(jnp.where(causal, key, imax), axis=0, keepdims=True))
        kmax = jnp.maximum(kmax, jnp.max(key, axis=0, keepdims=True))
        return kmin, kmax

    kmin, kmax = lax.fori_loop(0, nch, score_chunk,
                               (jnp.full((1, tq), imax, jnp.int32), jnp.full((1, tq), KEY_NEG_INF, jnp.int32)))

    half_rows = 2 * SUBLANES

    def count_half_ge(cand):
        c16 = cand.astype(jnp.int16)
        c16 = jnp.concatenate([c16, c16], axis=0)
        def body(c, cnt):
            hit = jnp.where(half_ref[c].reshape(kc // half_rows, half_rows, tq) >= c16[None],
                            jnp.int16(1), jnp.int16(0))
            part = hit[0]
            for r in range(1, kc // half_rows):
                part = part + hit[r]
            return cnt + part
        cnt = lax.fori_loop(0, nch, body, jnp.zeros((half_rows, tq), jnp.int16)).astype(jnp.int32)
        cnt = cnt[0:SUBLANES, :] + cnt[SUBLANES:, :]
        for shift in (4, 2, 1):
            cnt = cnt + pltpu.roll(cnt, shift, axis=0)
        return cnt

    def bisect(base, state):
        def one_pass(st):
            lo, hi, c_lo, c_hi, done = st
            active = done == 0
            cand = jnp.where(active, lo + ((hi >> 1) - (lo >> 1)), lo)
            cnt = base + count_half_ge(cand)
            up = active & (cnt >= n_sel)
            dn = active & (cnt < n_sel)
            lo, c_lo = jnp.where(up, cand, lo), jnp.where(up, cnt, c_lo)
            hi, c_hi = jnp.where(dn, cand, hi), jnp.where(dn, cnt, c_hi)
            done = jnp.where((done != 0) | (c_lo == n_sel) | (hi - 1 <= lo), 1, 0)
            return lo, hi, c_lo, c_hi, done

        def step(st):
            inner = st[1:]
            for _ in range(SEARCH_PASSES_PER_CHECK):
                inner = one_pass(inner)
            return (st[0] + SEARCH_PASSES_PER_CHECK,) + inner

        cond = lambda st: (st[0] < SEARCH_CAP) & (jnp.min(st[5]) == 0)
        return lax.while_loop(cond, step, (jnp.int32(0),) + state)[1:]

    rep = lambda a: jnp.broadcast_to(a, (SUBLANES, tq))
    zeros = jnp.zeros((SUBLANES, tq), jnp.int32)
    n_valid = q0 + lax.broadcasted_iota(jnp.int32, (SUBLANES, tq), 1) + 1
    all_selected = n_valid <= n_sel

    h_lo = jnp.where(all_selected, KEY_NEG_INF >> 16, rep(kmin) >> 16)
    h, _, c_ge_h, c_gt_h, _ = bisect(zeros, (h_lo, (rep(kmax) >> 16) + 1, n_valid, zeros,
                                             all_selected.astype(jnp.int32)))
    settled = all_selected | (c_ge_h == n_sel)

    def low_chunk(c, carry):
        key = key_ref[c]
        low = jnp.where((key >> 16) == h[0:1, :], (key & 0xFFFF) - HALF_BIAS, -HALF_BIAS)
        half_ref[c] = low.astype(jnp.int16)
        return carry

    lax.fori_loop(0, nch, low_chunk, 0)
    low, _, c_lo, c_hi, _ = bisect(c_gt_h, (jnp.full((SUBLANES, tq), -HALF_BIAS, jnp.int32),
                                            jnp.full((SUBLANES, tq), HALF_BIAS, jnp.int32),
                                            c_ge_h, c_gt_h, settled.astype(jnp.int32)))
    thr = jnp.where(all_selected, KEY_NEG_INF, h * (2 * HALF_BIAS) + (low + HALF_BIAS))
    thr, c_lo, c_hi = thr[0:1, :], c_lo[0:1, :], c_hi[0:1, :]
    need = jnp.where(c_lo == n_sel, float(2 ** 30), (n_sel - c_hi).astype(jnp.float32))

    def bias_chunk(with_ties, c, n_eq_before):
        ks = pl.multiple_of(c * kc, kc)
        key = key_ref[c]
        causal = krow + ks <= qpos
        if with_ties:
            eq = key == thr
            rank = _mm(tri_ref[...], jnp.where(eq, 1.0, 0.0)) + n_eq_before
            sel = ((key > thr) | (eq & (rank <= need))) & causal
            n_eq_before = rank[kc - 1:kc, :]
        else:
            sel = (key >= thr) & causal
        key_ref[c] = lax.bitcast_convert_type(jnp.where(sel, 0.0, NEG_INF), jnp.int32)
        return n_eq_before

    has_ties = jnp.max(c_lo) > n_sel

    @pl.when(has_ties)
    def _():
        lax.fori_loop(0, nch, functools.partial(bias_chunk, True), jnp.zeros((1, tq), jnp.float32))

    @pl.when(jnp.logical_not(has_ties))
    def _():
        lax.fori_loop(0, nch, functools.partial(bias_chunk, False), jnp.zeros((1, tq), jnp.float32))

    acc_ref[...] = jnp.zeros(acc_ref.shape, jnp.float32)
    group = N_HEADS // N_KV_HEADS
    ones_rows = jnp.ones((ACC_ROWS - HEAD_DIM, kc), MXU_DTYPE)

    def attend_chunk(c, ms):
        ks = pl.multiple_of(c * kc, kc)
        bias = lax.bitcast_convert_type(key_ref[c], jnp.float32)
        kch = k_ref[pl.ds(ks, kc), :]
        vch = v_ref[c]
        vext = [jnp.concatenate([vch[g * HEAD_DIM:(g + 1) * HEAD_DIM, :], ones_rows], axis=0)
                for g in range(N_KV_HEADS)]
        slot0 = jnp.minimum(c, 0)
        for h in range(N_HEADS):
            g = h // group
            s_ref[slot0 + h] = _mm_nt(kch[:, g * HEAD_DIM:(g + 1) * HEAD_DIM],
                                      q_ref[:, h * HEAD_DIM:(h + 1) * HEAD_DIM]) + bias
        new_ms = []
        for h in range(N_HEADS):
            m_new = jnp.maximum(ms[h], jnp.max(s_ref[slot0 + h], axis=0, keepdims=True))
            p = jnp.exp2(s_ref[slot0 + h] - m_new)
            acc_ref[h] = jnp.exp2(ms[h] - m_new) * acc_ref[h] + _mm(vext[h // group], p)
            new_ms.append(m_new)
        return tuple(new_ms)

    lax.fori_loop(0, nch, attend_chunk, tuple(jnp.full((1, tq), -1e30, jnp.float32) for _ in range(N_HEADS)))

    out_t = jnp.concatenate([acc_ref[h, 0:HEAD_DIM, :] / acc_ref[h, HEAD_DIM:HEAD_DIM + 1, :]
                             for h in range(N_HEADS)], axis=0)
    o_ref[...] = out_t.T.astype(o_ref.dtype)


def _attention(qi, wi_t, q, ki, k, v_t, batch, seq, tq, kc):
    n = q.shape[0]
    n_sel = min(MAX_TOPK, seq // 4)
    nq = seq // tq
    nck = seq // kc
    assert seq % tq == 0 and seq % kc == 0 and kc >= n_sel and v_t.shape == (n // kc, LANES, kc)
    tri = jnp.asarray((np.arange(kc)[None, :] <= np.arange(kc)[:, None]).astype(np.float32), MXU_DTYPE)
    qblk = lambda w: pl.BlockSpec((tq, w), lambda b, j: (b * nq + j, 0))
    seqblk = pl.BlockSpec((seq, LANES), lambda b, j: (b, 0))
    return pl.pallas_call(
        functools.partial(_attn_kernel, tq=tq, kc=kc, n_sel=n_sel),
        grid=(batch, nq),
        in_specs=[qblk(512), pl.BlockSpec((LANES, tq), lambda b, j: (0, b * nq + j)), qblk(512),
                  seqblk, seqblk, pl.BlockSpec((nck, LANES, kc), lambda b, j: (b, 0, 0)),
                  pl.BlockSpec((kc, kc), lambda b, j: (0, 0))],
        out_specs=qblk(512),
        out_shape=jax.ShapeDtypeStruct((n, N_HEADS * HEAD_DIM), MXU_DTYPE),
        scratch_shapes=[pltpu.VMEM((nck, kc, tq), jnp.int32),
                        pltpu.VMEM((nck, kc, tq), jnp.int16),
                        pltpu.VMEM((N_HEADS, kc, tq), jnp.float32),
                        pltpu.VMEM((N_HEADS, ACC_ROWS, tq), jnp.float32)],
        compiler_params=_cparams(2),
        name="dsa_attention",
    )(qi, wi_t, q, ki, k, v_t, tri)


def _merge_kernel(x_ref, ya_ref, u_ref, uh_ref, wg_ref, wua_ref, pw_ref, ps_ref, wup_ref, wo_ref,
                  g_ref, b_ref, rw_ref, rb_ref,
                  x1_ref, idx_ref, gate_ref, ext_ref, *, tm, seq):
    i = pl.program_id(0)
    x = x_ref[...]
    xb = x.astype(MXU_DTYPE)

    first = (i % (seq // tm)) == 0
    ext_ref[0:POOL_HALO, :] = jnp.where(first, 0.0, uh_ref[...])
    ext_ref[POOL_HALO:, :] = u_ref[...]
    pos1 = ((i % (seq // tm)) * tm + 1 + lax.broadcasted_iota(jnp.int32, (tm, 1), 0)).astype(jnp.float32)
    parts = []
    for g, win in enumerate(POOL_WINDOWS):
        e = ext_ref[:, g * POOL_GROUP:(g + 1) * POOL_GROUP]
        w = 1
        while w < win:
            e = e + pltpu.roll(e, w, axis=0)
            w *= 2
        tok = e[POOL_HALO:, :]
        ug = u_ref[:, g * POOL_GROUP:(g + 1) * POOL_GROUP]
        d = tok / jnp.minimum(pos1, float(win)) - ug
        parts.append(_mm(d, pw_ref[g]))
    y_pool = jnp.concatenate(parts, axis=1) * ps_ref[...]

    gates = jnp.dot(xb, wg_ref[...], preferred_element_type=jnp.float32)
    merged = (jax.nn.sigmoid(gates[:, :D_MODEL]) * _mm(ya_ref[...], wua_ref[...])
              + jax.nn.sigmoid(gates[:, D_MODEL:]) * _mm(y_pool, wup_ref[...]))
    mix = _mm(merged, wo_ref[...])
    x1 = _layer_norm(DN_ALPHA * x + mix, g_ref[...], b_ref[...])
    _store_rows(x1_ref, x1)

    logits = _mm(x1, rw_ref[...]) + rb_ref[...]
    lane = lax.broadcasted_iota(jnp.int32, logits.shape, 1)
    work = jnp.where(lane < N_EXPERTS, logits, NEG_INF)
    vals, idxs = [], []
    for _ in range(TOP_K):
        m = jnp.max(work, axis=1, keepdims=True)
        ix = jnp.min(jnp.where(work == m, lane, LANES), axis=1, keepdims=True)
        vals.append(m)
        idxs.append(ix)
        work = jnp.where(lane == ix, NEG_INF, work)
    es = [jnp.exp(vv - vals[0]) for vv in vals]
    den = es[0] + es[1] + es[2] + es[3]
    idx_out = jnp.zeros(logits.shape, jnp.int32)
    gate_out = jnp.zeros(logits.shape, jnp.float32)
    for kk in range(TOP_K):
        idx_out = jnp.where(lane == kk, idxs[kk], idx_out)
        gate_out = jnp.where(lane == kk, es[kk] / den, gate_out)
    idx_ref[...] = idx_out
    gate_ref[...] = gate_out


def _merge(x, y_attn, u, wg, wua, pw, ps, wup, wo, g, b, rw, rb, seq, tm=256):
    n = x.shape[0]
    hb = tm // POOL_HALO
    full = lambda shape: pl.BlockSpec(shape, lambda i: (0,) * len(shape))
    row = lambda w: pl.BlockSpec((tm, w), lambda i: (i, 0))
    return pl.pallas_call(
        functools.partial(_merge_kernel, tm=tm, seq=seq),
        grid=(n // tm,),
        in_specs=[row(D_MODEL), row(512), row(POOL_CH),
                  pl.BlockSpec((POOL_HALO, POOL_CH), lambda i: (jnp.maximum(i * hb - 1, 0), 0)),
                  full((D_MODEL, 2 * D_MODEL)), full((512, D_MODEL)),
                  full((4, POOL_GROUP, POOL_GROUP)), full((1, POOL_CH)), full((POOL_CH, D_MODEL)),
                  full((D_MODEL, D_MODEL)), full((1, D_MODEL)), full((1, D_MODEL)),
                  full((D_MODEL, LANES)), full((1, LANES))],
        out_specs=[pl.BlockSpec((tm * ROW_TILES, LANES), lambda i: (i, 0)), row(LANES), row(LANES)],
        out_shape=[jax.ShapeDtypeStruct((n * ROW_TILES, LANES), jnp.float32),
                   jax.ShapeDtypeStruct((n, LANES), jnp.int32),
                   jax.ShapeDtypeStruct((n, LANES), jnp.float32)],
        scratch_shapes=[pltpu.VMEM((tm + POOL_HALO, POOL_CH), jnp.float32)],
        compiler_params=_cparams(1),
        name="merge_ln1_router",
    )(x, y_attn, u, u, wg, wua, pw, ps, wup, wo, g, b, rw, rb)


def _gather_rows(idx_ref, n_rows, table_hbm, buf_ref, sem):
    def issue(i, carry):
        for j in range(GATHER_UNROLL):
            r = i * GATHER_UNROLL + j
            t = idx_ref[0, 0, r]
            pltpu.make_async_copy(table_hbm.at[pl.ds(pl.multiple_of(t * ROW_TILES, ROW_TILES), ROW_TILES), :],
                                  buf_ref.at[pl.ds(pl.multiple_of(r * ROW_TILES, ROW_TILES), ROW_TILES), :],
                                  sem).start()
        return carry
    assert n_rows % GATHER_UNROLL == 0
    lax.fori_loop(0, n_rows // GATHER_UNROLL, issue, 0)


def _wait_rows(table_hbm, buf_ref, sem):
    pltpu.make_async_copy(table_hbm.at[pl.ds(0, buf_ref.shape[0]), :], buf_ref, sem).wait()


def _rows_as_matrix(buf_ref, r0, n_rows):
    return jnp.concatenate(
        [buf_ref[pl.ds(r0 * ROW_TILES + c, n_rows, stride=ROW_TILES), :] for c in range(ROW_TILES)], axis=1)


def _store_rows(out_ref, y):
    for c in range(ROW_TILES):
        out_ref[pl.ds(c, y.shape[0], stride=ROW_TILES), :] = y[:, c * LANES:(c + 1) * LANES]


def _expert_kernel(be_ref, nu_ref, first_ref, nxt_ref, wslot_ref, tok_ref, tok_next_ref, x_hbm,
                   wgu_hbm, bgu_ref, wd_hbm, bd_ref, y_ref,
                   xbuf0, xbuf1, wgu_f0, wgu_f1, wd_f0, wd_f1, wgu_bf, wd_bf, sem, wsem, *, tb):
    b = pl.program_id(0)
    n_used = nu_ref[0]
    bufs = (xbuf0, xbuf1)
    wgu_f, wd_f = (wgu_f0, wgu_f1), (wd_f0, wd_f1)

    def weight_copies(e, s):
        return (pltpu.make_async_copy(wgu_hbm.at[e], wgu_f[s], wsem.at[s, 0]),
                pltpu.make_async_copy(wd_hbm.at[e], wd_f[s], wsem.at[s, 1]))

    @pl.when(b == 0)
    def _():
        for cp in weight_copies(be_ref[0], 0):
            cp.start()
        _gather_rows(tok_ref, tb, x_hbm, xbuf0, sem.at[0])

    for slot in range(2):
        @pl.when((b + 1 < n_used) & ((b + 1) % 2 == slot))
        def _():
            _gather_rows(tok_next_ref, tb, x_hbm, bufs[slot], sem.at[slot])

    @pl.when(b < n_used)
    def _():
        for s in range(2):
            @pl.when((first_ref[b] == 1) & (wslot_ref[b] == s))
            def _():
                for cp in weight_copies(be_ref[b], s):
                    cp.wait()
                wgu_bf[...] = wgu_f[s][...].astype(MXU_DTYPE)
                wd_bf[...] = wd_f[s][...].astype(MXU_DTYPE)

                @pl.when(nxt_ref[b] >= 0)
                def _():
                    for cp in weight_copies(nxt_ref[b], 1 - s):
                        cp.start()

        def compute(xbuf, slot):
            _wait_rows(x_hbm, xbuf, sem.at[slot])
            xr = _rows_as_matrix(xbuf, 0, tb)
            gu = _mm(xr, wgu_bf[...]) + bgu_ref[...]
            gt = jnp.minimum(gu[:, :D_FF], SWIGLU_LIMIT)
            up = jnp.clip(gu[:, D_FF:], -SWIGLU_LIMIT, SWIGLU_LIMIT)
            act = gt * jax.nn.sigmoid(SWIGLU_ALPHA * gt) * (up + 1.0)
            _store_rows(y_ref, _mm(act, wd_bf[...]) + bd_ref[...])

        for slot in range(2):
            pl.when(b % 2 == slot)(functools.partial(compute, bufs[slot], slot))

    @pl.when(b >= n_used)
    def _():
        y_ref[...] = jnp.zeros(y_ref.shape, y_ref.dtype)


def _experts(blk_exp, n_used, row_tok, x1_rows, w_gu, b_gu, w_down, b_down, tb):
    n_blocks = blk_exp.shape[0]
    blk = jnp.arange(n_blocks, dtype=jnp.int32)
    first = jnp.concatenate([jnp.ones((1,), jnp.int32), (blk_exp[1:] != blk_exp[:-1]).astype(jnp.int32)])
    wslot = (jnp.cumsum(first) - 1) % 2
    at_or_after = lax.cummin(jnp.where(first == 1, blk, n_blocks), reverse=True)
    next_first = jnp.concatenate([at_or_after[1:], jnp.full((1,), n_blocks, jnp.int32)])
    nxt = jnp.where(next_first < n_used[0], blk_exp[jnp.minimum(next_first, n_blocks - 1)], -1)
    bspec = lambda c: pl.BlockSpec((None, 1, c), lambda b, be, *_: (be[b], 0, 0))
    tok = row_tok.reshape(n_blocks, 1, tb)
    return pl.pallas_call(
        functools.partial(_expert_kernel, tb=tb),
        grid_spec=pltpu.PrefetchScalarGridSpec(
            num_scalar_prefetch=5,
            grid=(n_blocks,),
            in_specs=[pl.BlockSpec((1, 1, tb), lambda b, *_: (b, 0, 0), memory_space=pltpu.SMEM),
                      pl.BlockSpec((1, 1, tb), lambda b, *_: (jnp.minimum(b + 1, n_blocks - 1), 0, 0),
                                   memory_space=pltpu.SMEM),
                      pl.BlockSpec(memory_space=pl.ANY),
                      pl.BlockSpec(memory_space=pl.ANY), bspec(2 * D_FF),
                      pl.BlockSpec(memory_space=pl.ANY), bspec(D_MODEL)],
            out_specs=pl.BlockSpec((tb * ROW_TILES, LANES), lambda b, *_: (b, 0)),
            scratch_shapes=[pltpu.VMEM((tb * ROW_TILES, LANES), jnp.float32),
                            pltpu.VMEM((tb * ROW_TILES, LANES), jnp.float32),
                            pltpu.VMEM((D_MODEL, 2 * D_FF), jnp.float32),
                            pltpu.VMEM((D_MODEL, 2 * D_FF), jnp.float32),
                            pltpu.VMEM((D_FF, D_MODEL), jnp.float32),
                            pltpu.VMEM((D_FF, D_MODEL), jnp.float32),
                            pltpu.VMEM((D_MODEL, 2 * D_FF), MXU_DTYPE),
                            pltpu.VMEM((D_FF, D_MODEL), MXU_DTYPE),
                            pltpu.SemaphoreType.DMA((2,)),
                            pltpu.SemaphoreType.DMA((2, 2))]),
        out_shape=jax.ShapeDtypeStruct((n_blocks * tb * ROW_TILES, LANES), jnp.float32),
        compiler_params=_cparams(1),
        name="moe_experts",
    )(blk_exp, n_used, first, nxt.astype(jnp.int32), wslot.astype(jnp.int32), tok, tok, x1_rows,
      w_gu, b_gu, w_down, b_down)


def _combine_kernel(dest_ref, dest_next_ref, y_hbm, gate_ref, x1_ref, p_ref, wpg_ref, wpp_ref, g_ref, b_ref,
                    o_ref, ybuf0, ybuf1, sem, *, tm):
    i = pl.program_id(0)
    bufs = (ybuf0, ybuf1)

    @pl.when(i == 0)
    def _():
        _gather_rows(dest_ref, TOP_K * tm, y_hbm, ybuf0, sem.at[0])

    for slot in range(2):
        @pl.when((i + 1 < pl.num_programs(0)) & ((i + 1) % 2 == slot))
        def _():
            _gather_rows(dest_next_ref, TOP_K * tm, y_hbm, bufs[slot], sem.at[slot])

    x1 = _rows_as_matrix(x1_ref, 0, tm)
    ple = jax.nn.sigmoid(_mm(x1, wpg_ref[...])) * _mm(p_ref[...], wpp_ref[...])
    h = DN_ALPHA * x1 + ple
    gate = gate_ref[...]

    def finish(ybuf, slot):
        _wait_rows(y_hbm, ybuf, sem.at[slot])
        ffn = h
        for kk in range(TOP_K):
            ffn = ffn + _rows_as_matrix(ybuf, kk * tm, tm) * gate[:, kk:kk + 1]
        o_ref[...] = _layer_norm(ffn, g_ref[...], b_ref[...])

    for slot in range(2):
        pl.when(i % 2 == slot)(functools.partial(finish, bufs[slot], slot))


def _combine(dest, yr, gates, x1_rows, p, wpg, wpp, g, b, tm=256):
    n = p.shape[0]
    nt = n // tm
    full = lambda shape: pl.BlockSpec(shape, lambda i: (0,) * len(shape))
    row = lambda w: pl.BlockSpec((tm, w), lambda i: (i, 0))
    dest_t = dest.reshape(nt, tm, TOP_K).transpose(0, 2, 1).reshape(nt, 1, TOP_K * tm)
    return pl.pallas_call(
        functools.partial(_combine_kernel, tm=tm),
        grid=(nt,),
        in_specs=[pl.BlockSpec((1, 1, TOP_K * tm), lambda i: (i, 0, 0), memory_space=pltpu.SMEM),
                  pl.BlockSpec((1, 1, TOP_K * tm), lambda i: (jnp.minimum(i + 1, nt - 1), 0, 0),
                               memory_space=pltpu.SMEM),
                  pl.BlockSpec(memory_space=pl.ANY),
                  row(LANES), pl.BlockSpec((tm * ROW_TILES, LANES), lambda i: (i, 0)), row(PLE_DIM),
                  full((D_MODEL, D_MODEL)), full((PLE_DIM, D_MODEL)),
                  full((1, D_MODEL)), full((1, D_MODEL))],
        out_specs=row(D_MODEL),
        out_shape=jax.ShapeDtypeStruct((n, D_MODEL), jnp.float32),
        scratch_shapes=[pltpu.VMEM((TOP_K * tm * ROW_TILES, LANES), jnp.float32),
                        pltpu.VMEM((TOP_K * tm * ROW_TILES, LANES), jnp.float32),
                        pltpu.SemaphoreType.DMA((2,))],
        compiler_params=_cparams(1),
        name="combine_ple_ln2",
    )(dest_t, dest_t, yr, gates, x1_rows, p, wpg, wpp, g, b)


def _route(top_idx, n_tokens, tb):
    a = n_tokens * TOP_K
    n_blocks = a // tb + N_EXPERTS
    onehot = (top_idx[:, :, None] == jnp.arange(N_EXPERTS, dtype=jnp.int32)[None, None, :]).astype(jnp.int32)
    member = onehot.sum(axis=1)
    rank = jnp.cumsum(member, axis=0) - member
    counts = member.sum(axis=0)
    padded = (counts + tb - 1) // tb * tb
    pend = jnp.cumsum(padded)
    pstart = pend - padded
    dest = jnp.take_along_axis(rank + pstart[None, :], top_idx, axis=1).astype(jnp.int32)
    tok = jnp.broadcast_to(jnp.arange(n_tokens, dtype=jnp.int32)[:, None], (n_tokens, TOP_K))
    row_tok = jnp.zeros((n_blocks * tb,), jnp.int32).at[dest.reshape(-1)].set(
        tok.reshape(-1), unique_indices=True, mode="promise_in_bounds")
    blk_start = jnp.arange(n_blocks, dtype=jnp.int32) * tb
    blk_exp = jnp.minimum((blk_start[:, None] >= pend[None, :]).astype(jnp.int32).sum(axis=1), N_EXPERTS - 1)
    n_used = (pend[-1] // tb).astype(jnp.int32).reshape(1)
    return dest, row_tok, blk_exp, n_used


def kernel(x, p, ln0_g, ln0_b, w_in, pool_w, pool_scale, w_up_attn, w_up_pool, w_out, ln1_g, ln1_b,
           router_w, router_b, exp_w_gu, exp_b_gu, exp_w_down, exp_b_down, ple_w_gate, ple_w_proj,
           ln2_g, ln2_b):
    batch, seq, d = x.shape
    assert d == D_MODEL
    n = batch * seq
    tb = 256
    kc = 512
    tables = _rope_tables(seq)
    bf = lambda a: a.astype(MXU_DTYPE)
    vec = lambda a: a.reshape(1, -1)
    n_le = DEPTH * N_EXPERTS
    w_gu_all = exp_w_gu.reshape(n_le, D_MODEL, 2 * D_FF)
    b_gu_all = exp_b_gu.reshape(n_le, 1, 2 * D_FF)
    w_down_all = exp_w_down.reshape(n_le, D_FF, D_MODEL)
    b_down_all = exp_b_down.reshape(n_le, 1, D_MODEL)

    h = _ln0(x.reshape(n, d), ln0_g, ln0_b)
    for i in range(DEPTH):
        w = w_in[i]
        w1 = bf(jnp.concatenate([w[:, :1352], jnp.zeros((d, OFF_U - 1352), w.dtype), w[:, 1352:1864]], axis=1))
        wg = bf(w[:, 1864:])
        q, k, v_t, qi, ki, wi_t, u = _inproj(h, w1, tables, seq, kc)
        y_attn = _attention(qi, wi_t, q, ki, k, v_t, batch, seq, 512, kc)
        rw = bf(jnp.pad(router_w[i], ((0, 0), (0, LANES - N_EXPERTS))))
        rb = jnp.pad(router_b[i], (0, LANES - N_EXPERTS)).reshape(1, -1)
        x1_rows, idx_l, gate_l = _merge(h, y_attn, u, wg, bf(w_up_attn[i]), bf(pool_w[i]), vec(pool_scale[i]),
                                        bf(w_up_pool[i]), bf(w_out[i]), vec(ln1_g[i]), vec(ln1_b[i]), rw, rb, seq)
        dest, row_tok, blk_exp, n_used = _route(idx_l[:, :TOP_K], n, tb)
        yr = _experts(blk_exp + i * N_EXPERTS, n_used, row_tok, x1_rows,
                      w_gu_all, b_gu_all, w_down_all, b_down_all, tb)
        h = _combine(dest, yr, gate_l, x1_rows, p[i].reshape(n, PLE_DIM), bf(ple_w_gate[i]), bf(ple_w_proj[i]),
                     vec(ln2_g[i]), vec(ln2_b[i]))
    return h.reshape(batch, seq, d)
```

```python
import functools

import jax
import jax.numpy as jnp
import numpy as np
from jax import lax
from jax.experimental import pallas as pl
from jax.experimental.pallas import tpu as pltpu

MXU_DTYPE = jnp.bfloat16

D_MODEL = 1024
HEAD_DIM = 64
N_HEADS = 8
N_KV_HEADS = 2
ROT_DIM = 16
ROPE_THETA = 500000.0
IDX_HEADS = 8
IDX_DIM = 64
MAX_TOPK = 256
POOL_CH = 512
POOL_WINDOWS = (2, 4, 8, 16)
POOL_GROUP = 128
POOL_HALO = 16
N_EXPERTS = 32
TOP_K = 4
D_FF = 1024
SWIGLU_LIMIT = 7.0
SWIGLU_ALPHA = 1.702
PLE_DIM = 256
LN_EPS = 1e-5
DEPTH = 2
DN_ALPHA = (2 * DEPTH) ** 0.25
LOG2_E = 1.4426950408889634

LANES = 128
SUBLANES = 8
ROW_TILES = D_MODEL // LANES
assert ROW_TILES == SUBLANES

OFF_Q, OFF_K, OFF_V, OFF_QI, OFF_KIW, OFF_U, W1_COLS = 0, 512, 640, 768, 1280, 1408, 1920

VMEM_LIMIT = 56 * 1024 * 1024
NEG_INF = float("-inf")
KEY_NEG_INF = -2139095041
SEARCH_CAP = 20
HALF_BIAS = 32768
SEARCH_PASSES_PER_CHECK = 4
ACC_ROWS = 80
GATHER_UNROLL = 8

def _cparams(n_axes, flags=None):
    return pltpu.CompilerParams(dimension_semantics=("arbitrary",) * n_axes,
                                vmem_limit_bytes=VMEM_LIMIT, flags=flags)


def _mm(a, b):
    return jnp.dot(a.astype(MXU_DTYPE), b.astype(MXU_DTYPE), preferred_element_type=jnp.float32)


def _mm_nt(a, b):
    return lax.dot_general(a.astype(MXU_DTYPE), b.astype(MXU_DTYPE), (((1,), (1,)), ((), ())),
                           preferred_element_type=jnp.float32)


def _layer_norm(h, g, b):
    mu = jnp.mean(h, axis=-1, keepdims=True)
    c = h - mu
    var = jnp.mean(c * c, axis=-1, keepdims=True)
    return c * lax.rsqrt(var + LN_EPS) * g + b


def _ln0_kernel(x_ref, g_ref, b_ref, o_ref):
    o_ref[...] = _layer_norm(x_ref[...], g_ref[...], b_ref[...])


def _ln0(x, g, b, tm=512):
    n = x.shape[0]
    return pl.pallas_call(
        _ln0_kernel,
        grid=(n // tm,),
        in_specs=[pl.BlockSpec((tm, D_MODEL), lambda i: (i, 0)),
                  pl.BlockSpec((1, D_MODEL), lambda i: (0, 0)),
                  pl.BlockSpec((1, D_MODEL), lambda i: (0, 0))],
        out_specs=pl.BlockSpec((tm, D_MODEL), lambda i: (i, 0)),
        out_shape=jax.ShapeDtypeStruct((n, D_MODEL), jnp.float32),
        compiler_params=_cparams(1),
        name="ln0",
    )(x, g.reshape(1, -1), b.reshape(1, -1))


def _rope_tables(seq):
    pos = jnp.arange(seq, dtype=jnp.float32)
    inv = ROPE_THETA ** (-jnp.arange(0, ROT_DIM, 2, dtype=jnp.float32) / ROT_DIM)
    ang = pos[:, None] * inv[None, :]
    cos, sin = jnp.cos(ang), jnp.sin(ang)
    half = ROT_DIM // 2
    one = jnp.ones((seq, HEAD_DIM - ROT_DIM), jnp.float32)
    zero = jnp.zeros((seq, HEAD_DIM - ROT_DIM), jnp.float32)
    zh = jnp.zeros((seq, half), jnp.float32)
    c64 = jnp.concatenate([cos, cos, one], axis=1)
    s1_64 = jnp.concatenate([-sin, zh, zero], axis=1)
    s2_64 = jnp.concatenate([zh, sin, zero], axis=1)
    ident_c = jnp.ones((seq, HEAD_DIM), jnp.float32)
    ident_s = jnp.zeros((seq, HEAD_DIM), jnp.float32)
    c = jnp.concatenate([c64, c64, c64, ident_c], axis=1)
    s1 = jnp.concatenate([s1_64, s1_64, s1_64, ident_s], axis=1)
    s2 = jnp.concatenate([s2_64, s2_64, s2_64, ident_s], axis=1)
    return c, s1, s2


def _rope_tile(x, c, s1, s2):
    half = ROT_DIM // 2
    return x * c + pltpu.roll(x, LANES - half, axis=1) * s1 + pltpu.roll(x, half, axis=1) * s2


def _inproj_kernel(x_ref, w_ref, c_ref, s1_ref, s2_ref,
                   q_ref, k_ref, v_ref, qi_ref, ki_ref, wi_ref, u_ref):
    xb = x_ref[...].astype(MXU_DTYPE)
    c, s1, s2 = c_ref[:, :LANES], s1_ref[:, :LANES], s2_ref[:, :LANES]
    ck, s1k, s2k = c_ref[:, LANES:], s1_ref[:, LANES:], s2_ref[:, LANES:]

    def seg(off, width):
        return jnp.dot(xb, w_ref[:, off:off + width], preferred_element_type=jnp.float32)

    def roped(z, scale, out_ref):
        for j in range(z.shape[1] // LANES):
            t = _rope_tile(z[:, j * LANES:(j + 1) * LANES], c, s1, s2)
            out_ref[:, j * LANES:(j + 1) * LANES] = (t * scale).astype(out_ref.dtype)

    roped(seg(OFF_Q, 512), HEAD_DIM ** -0.5 * LOG2_E, q_ref)
    roped(seg(OFF_K, 128), 1.0, k_ref)
    v_ref[0] = seg(OFF_V, 128).T.astype(v_ref.dtype)
    roped(seg(OFF_QI, 512), IDX_DIM ** -0.5, qi_ref)
    kiw = seg(OFF_KIW, 128)
    ki_ref[...] = _rope_tile(kiw, ck, s1k, s2k).astype(ki_ref.dtype)
    wi_ref[...] = (kiw * (IDX_HEADS ** -0.5)).T
    u_ref[...] = seg(OFF_U, 512)


def _inproj(x, w1, tables, seq, tm):
    n = x.shape[0]
    nseq = seq // tm
    c, s1, s2 = tables
    tab_spec = pl.BlockSpec((tm, 2 * LANES), lambda i: (i % nseq, 0))

    def out(width, dtype):
        return pl.BlockSpec((tm, width), lambda i: (i, 0)), jax.ShapeDtypeStruct((n, width), dtype)

    v_t = (pl.BlockSpec((1, LANES, tm), lambda i: (i, 0, 0)), jax.ShapeDtypeStruct((n // tm, LANES, tm), MXU_DTYPE))
    wi_t = (pl.BlockSpec((LANES, tm), lambda i: (0, i)), jax.ShapeDtypeStruct((LANES, n), jnp.float32))
    outs = [out(512, MXU_DTYPE), out(128, MXU_DTYPE), v_t, out(512, MXU_DTYPE),
            out(128, MXU_DTYPE), wi_t, out(512, jnp.float32)]
    return pl.pallas_call(
        _inproj_kernel,
        grid=(n // tm,),
        in_specs=[pl.BlockSpec((tm, D_MODEL), lambda i: (i, 0)),
                  pl.BlockSpec((D_MODEL, W1_COLS), lambda i: (0, 0)),
                  tab_spec, tab_spec, tab_spec],
        out_specs=[o[0] for o in outs],
        out_shape=[o[1] for o in outs],
        compiler_params=_cparams(1),
        name="inproj",
    )(x, w1, c, s1, s2)


def _sortable_key(s):
    k = lax.bitcast_convert_type(s, jnp.int32)
    return k ^ ((k >> 31) & jnp.int32(0x7FFFFFFF))


def _attn_kernel(qi_ref, wi_ref, q_ref, ki_ref, k_ref, v_ref, tri_ref, o_ref,
                 key_ref, half_ref, s_ref, acc_ref, *, tq, kc, n_sel):
    j = pl.program_id(1)
    q0 = j * tq
    nch = (q0 + tq + kc - 1) // kc
    qpos = q0 + lax.broadcasted_iota(jnp.int32, (kc, tq), 1)
    krow = lax.broadcasted_iota(jnp.int32, (kc, tq), 0)

    def score_chunk(c, carry):
        ks = pl.multiple_of(c * kc, kc)
        kic = ki_ref[pl.ds(ks, kc), :][:, :IDX_DIM]
        acc = jnp.zeros((kc, tq), jnp.float32)
        for h in range(IDX_HEADS):
            d = _mm_nt(kic, qi_ref[:, h * IDX_DIM:(h + 1) * IDX_DIM])
            acc = acc + jnp.maximum(d, 0.0) * wi_ref[IDX_DIM + h:IDX_DIM + h + 1, :]
        causal = krow + ks <= qpos
        key = _sortable_key(jnp.where(causal, acc, NEG_INF))
        key_ref[c] = key
        half_ref[c] = (key >> 16).astype(jnp.int16)
        return carry

    lax.fori_loop(0, nch, score_chunk, 0)

    half_rows = 2 * SUBLANES

    def count_half_ge(cand):
        c16 = cand.astype(jnp.int16)
        c16 = jnp.concatenate([c16, c16], axis=0)
        def body(c, cnt):
            hit = jnp.where(half_ref[c].reshape(kc // half_rows, half_rows, tq) >= c16[None],
                            jnp.int16(1), jnp.int16(0))
            part = hit[0]
            for r in range(1, kc // half_rows):
                part = part + hit[r]
            return cnt + part
        cnt = lax.fori_loop(0, nch, body, jnp.zeros((half_rows, tq), jnp.int16)).astype(jnp.int32)
        cnt = cnt[0:SUBLANES, :] + cnt[SUBLANES:, :]
        for shift in (4, 2, 1):
            cnt = cnt + pltpu.roll(cnt, shift, axis=0)
        return cnt

    def bisect(base, state):
        def one_pass(st):
            lo, hi, c_lo, c_hi, done = st
            active = done == 0
            cand = jnp.where(active, lo + ((hi >> 1) - (lo >> 1)), lo)
            cnt = base + count_half_ge(cand)
            up = active & (cnt >= n_sel)
            dn = active & (cnt < n_sel)
            lo, c_lo = jnp.where(up, cand, lo), jnp.where(up, cnt, c_lo)
            hi, c_hi = jnp.where(dn, cand, hi), jnp.where(dn, cnt, c_hi)
            done = jnp.where((done != 0) | (c_lo == n_sel) | (hi - 1 <= lo), 1, 0)
            return lo, hi, c_lo, c_hi, done

        def step(st):
            inner = st[1:]
            for _ in range(SEARCH_PASSES_PER_CHECK):
                inner = one_pass(inner)
            return (st[0] + SEARCH_PASSES_PER_CHECK,) + inner

        cond = lambda st: (st[0] < SEARCH_CAP) & (jnp.min(st[5]) == 0)
        return lax.while_loop(cond, step, (jnp.int32(0),) + state)[1:]

    zeros =jnp.zeros((SUBLANES, tq), jnp.int32)
    n_valid = q0 + lax.broadcasted_iota(jnp.int32, (SUBLANES, tq), 1) + 1
    all_selected = n_valid <= n_sel

    h, _, c_ge_h, c_gt_h, _ = bisect(zeros, (jnp.full((SUBLANES, tq), -HALF_BIAS, jnp.int32),
                                             jnp.full((SUBLANES, tq), HALF_BIAS, jnp.int32),
                                             jnp.where(all_selected, n_valid, nch * kc), zeros,
                                             all_selected.astype(jnp.int32)))
    settled = all_selected | (c_ge_h == n_sel)

    def low_chunk(c, carry):
        key = key_ref[c]
        low = jnp.where((key >> 16) == h[0:1, :], (key & 0xFFFF) - HALF_BIAS, -HALF_BIAS)
        half_ref[c] = low.astype(jnp.int16)
        return carry

    lax.fori_loop(0, nch, low_chunk, 0)
    low, _, c_lo, c_hi, _ = bisect(c_gt_h, (jnp.full((SUBLANES, tq), -HALF_BIAS, jnp.int32),
                                            jnp.full((SUBLANES, tq), HALF_BIAS, jnp.int32),
                                            c_ge_h, c_gt_h, settled.astype(jnp.int32)))
    thr = jnp.where(all_selected, KEY_NEG_INF, h * (2 * HALF_BIAS) + (low + HALF_BIAS))
    thr, c_lo, c_hi = thr[0:1, :], c_lo[0:1, :], c_hi[0:1, :]
    need = jnp.where(c_lo == n_sel, float(2 ** 30), (n_sel - c_hi).astype(jnp.float32))

    def bias_chunk(with_ties, c, n_eq_before):
        ks = pl.multiple_of(c * kc, kc)
        key = key_ref[c]
        causal = krow + ks <= qpos
        if with_ties:
            eq = key == thr
            rank = _mm(tri_ref[...], jnp.where(eq, 1.0, 0.0)) + n_eq_before
            sel = ((key > thr) | (eq & (rank <= need))) & causal
            n_eq_before = rank[kc - 1:kc, :]
        else:
            sel = (key >= thr) & causal
        key_ref[c] = lax.bitcast_convert_type(jnp.where(sel, 0.0, NEG_INF), jnp.int32)
        return n_eq_before

    has_ties = jnp.max(c_lo) > n_sel

    @pl.when(has_ties)
    def _():
        lax.fori_loop(0, nch, functools.partial(bias_chunk, True), jnp.zeros((1, tq), jnp.float32))

    @pl.when(jnp.logical_not(has_ties))
    def _():
        lax.fori_loop(0, nch, functools.partial(bias_chunk, False), jnp.zeros((1, tq), jnp.float32))

    acc_ref[...] = jnp.zeros(acc_ref.shape, jnp.float32)
    group = N_HEADS // N_KV_HEADS
    ones_rows = jnp.ones((ACC_ROWS - HEAD_DIM, kc), MXU_DTYPE)

    def attend_chunk(c, ms):
        ks = pl.multiple_of(c * kc, kc)
        bias = lax.bitcast_convert_type(key_ref[c], jnp.float32)
        kch = k_ref[pl.ds(ks, kc), :]
        vch = v_ref[c]
        vext = [jnp.concatenate([vch[g * HEAD_DIM:(g + 1) * HEAD_DIM, :], ones_rows], axis=0)
                for g in range(N_KV_HEADS)]
        slot0 = jnp.minimum(c, 0)
        mx = []
        for h in range(N_HEADS):
            g = h // group
            s = _mm_nt(kch[:, g * HEAD_DIM:(g + 1) * HEAD_DIM], q_ref[:, h * HEAD_DIM:(h + 1) * HEAD_DIM]) + bias
            s_ref[slot0 + h] = s
            mx.append(jnp.max(s, axis=0, keepdims=True))
        new_ms = []
        for h in range(N_HEADS):
            m_new = jnp.maximum(ms[h], mx[h])
            p = jnp.exp2(s_ref[slot0 + h] - m_new)
            acc_ref[h] = jnp.exp2(ms[h] - m_new) * acc_ref[h] + _mm(vext[h // group], p)
            new_ms.append(m_new)
        return tuple(new_ms)

    lax.fori_loop(0, nch, attend_chunk, tuple(jnp.full((1, tq), -1e30, jnp.float32) for _ in range(N_HEADS)))

    out_t = jnp.concatenate([acc_ref[h, 0:HEAD_DIM, :] / acc_ref[h, HEAD_DIM:HEAD_DIM + 1, :]
                             for h in range(N_HEADS)], axis=0)
    o_ref[...] = out_t.T.astype(o_ref.dtype)


def _attention(qi, wi_t, q, ki, k, v_t, batch, seq, tq, kc):
    n = q.shape[0]
    n_sel = min(MAX_TOPK, seq // 4)
    nq = seq // tq
    nck = seq // kc
    assert seq % tq == 0 and seq % kc == 0 and kc >= n_sel and v_t.shape == (n // kc, LANES, kc)
    tri = jnp.asarray((np.arange(kc)[None, :] <= np.arange(kc)[:, None]).astype(np.float32), MXU_DTYPE)
    qblk = lambda w: pl.BlockSpec((tq, w), lambda b, j: (b * nq + j, 0))
    seqblk = pl.BlockSpec((seq, LANES), lambda b, j: (b, 0))
    return pl.pallas_call(
        functools.partial(_attn_kernel, tq=tq, kc=kc, n_sel=n_sel),
        grid=(batch, nq),
        in_specs=[qblk(512), pl.BlockSpec((LANES, tq), lambda b, j: (0, b * nq + j)), qblk(512),
                  seqblk, seqblk, pl.BlockSpec((nck, LANES, kc), lambda b, j: (b, 0, 0)),
                  pl.BlockSpec((kc, kc), lambda b, j: (0, 0))],
        out_specs=qblk(512),
        out_shape=jax.ShapeDtypeStruct((n, N_HEADS * HEAD_DIM), MXU_DTYPE),
        scratch_shapes=[pltpu.VMEM((nck, kc, tq), jnp.int32),
                        pltpu.VMEM((nck, kc, tq), jnp.int16),
                        pltpu.VMEM((N_HEADS, kc, tq), jnp.float32),
                        pltpu.VMEM((N_HEADS, ACC_ROWS, tq), jnp.float32)],
        compiler_params=_cparams(2),
        name="dsa_attention",
    )(qi, wi_t, q, ki, k, v_t, tri)


def _merge_kernel(x_ref, ya_ref, u_ref, uh_ref, wg_ref, wua_ref, pw_ref, ps_ref, wup_ref, wo_ref,
                  g_ref, b_ref, rw_ref, rb_ref,
                  x1_ref, idx_ref, gate_ref, ext_ref, *, tm, seq):
    i = pl.program_id(0)
    x = x_ref[...]
    xb = x.astype(MXU_DTYPE)

    first = (i % (seq // tm)) == 0
    ext_ref[0:POOL_HALO, :] = jnp.where(first, 0.0, uh_ref[...])
    ext_ref[POOL_HALO:, :] = u_ref[...]
    pos1 = ((i % (seq // tm)) * tm + 1 + lax.broadcasted_iota(jnp.int32, (tm, 1), 0)).astype(jnp.float32)
    parts = []
    for g, win in enumerate(POOL_WINDOWS):
        e = ext_ref[:, g * POOL_GROUP:(g + 1) * POOL_GROUP]
        w = 1
        while w < win:
            e = e + pltpu.roll(e, w, axis=0)
            w *= 2
        tok = e[POOL_HALO:, :]
        ug = u_ref[:, g * POOL_GROUP:(g + 1) * POOL_GROUP]
        d = tok / jnp.minimum(pos1, float(win)) - ug
        parts.append(_mm(d, pw_ref[g]))
    y_pool = jnp.concatenate(parts, axis=1) * ps_ref[...]

    gates = jnp.dot(xb, wg_ref[...], preferred_element_type=jnp.float32)
    merged = (jax.nn.sigmoid(gates[:, :D_MODEL]) * _mm(ya_ref[...], wua_ref[...])
              + jax.nn.sigmoid(gates[:, D_MODEL:]) * _mm(y_pool, wup_ref[...]))
    mix = _mm(merged, wo_ref[...])
    x1 = _layer_norm(DN_ALPHA * x + mix, g_ref[...], b_ref[...])
    _store_rows(x1_ref, x1)

    logits = _mm(x1, rw_ref[...]) + rb_ref[...]
    lane = lax.broadcasted_iota(jnp.int32, logits.shape, 1)
    work = jnp.where(lane < N_EXPERTS, logits, NEG_INF)
    vals, idxs = [], []
    for _ in range(TOP_K):
        m = jnp.max(work, axis=1, keepdims=True)
        ix = jnp.min(jnp.where(work == m, lane, LANES), axis=1, keepdims=True)
        vals.append(m)
        idxs.append(ix)
        work = jnp.where(lane == ix, NEG_INF, work)
    es = [jnp.exp(vv - vals[0]) for vv in vals]
    den = es[0] + es[1] + es[2] + es[3]
    idx_out = jnp.zeros(logits.shape, jnp.int32)
    gate_out = jnp.zeros(logits.shape, jnp.float32)
    for kk in range(TOP_K):
        idx_out = jnp.where(lane == kk, idxs[kk], idx_out)
        gate_out = jnp.where(lane == kk, es[kk] / den, gate_out)
    idx_ref[...] = idx_out
    gate_ref[...] = gate_out


def _merge(x, y_attn, u, wg, wua, pw, ps, wup, wo, g, b, rw, rb, seq, tm=512):
    n = x.shape[0]
    hb = tm // POOL_HALO
    full = lambda shape: pl.BlockSpec(shape, lambda i: (0,) * len(shape))
    row = lambda w: pl.BlockSpec((tm, w), lambda i: (i, 0))
    return pl.pallas_call(
        functools.partial(_merge_kernel, tm=tm, seq=seq),
        grid=(n // tm,),
        in_specs=[row(D_MODEL), row(512), row(POOL_CH),
                  pl.BlockSpec((POOL_HALO, POOL_CH), lambda i: (jnp.maximum(i * hb - 1, 0), 0)),
                  full((D_MODEL, 2 * D_MODEL)), full((512, D_MODEL)),
                  full((4, POOL_GROUP, POOL_GROUP)), full((1, POOL_CH)), full((POOL_CH, D_MODEL)),
                  full((D_MODEL, D_MODEL)), full((1, D_MODEL)), full((1, D_MODEL)),
                  full((D_MODEL, LANES)), full((1, LANES))],
        out_specs=[pl.BlockSpec((tm * ROW_TILES, LANES), lambda i: (i, 0)), row(LANES), row(LANES)],
        out_shape=[jax.ShapeDtypeStruct((n * ROW_TILES, LANES), jnp.float32),
                   jax.ShapeDtypeStruct((n, LANES), jnp.int32),
                   jax.ShapeDtypeStruct((n, LANES), jnp.float32)],
        scratch_shapes=[pltpu.VMEM((tm + POOL_HALO, POOL_CH), jnp.float32)],
        compiler_params=_cparams(1),
        name="merge_ln1_router",
    )(x, y_attn, u, u, wg, wua, pw, ps, wup, wo, g, b, rw, rb)


def _gather_rows(idx_ref, n_rows, table_hbm, buf_ref, sem):
    def issue(i, carry):
        for j in range(GATHER_UNROLL):
            r = i * GATHER_UNROLL + j
            t = idx_ref[0, 0, r]
            pltpu.make_async_copy(table_hbm.at[pl.ds(pl.multiple_of(t * ROW_TILES, ROW_TILES), ROW_TILES), :],
                                  buf_ref.at[pl.ds(pl.multiple_of(r * ROW_TILES, ROW_TILES), ROW_TILES), :],
                                  sem).start()
        return carry
    assert n_rows % GATHER_UNROLL == 0
    lax.fori_loop(0, n_rows // GATHER_UNROLL, issue, 0)


def _wait_rows(table_hbm, buf_ref, sem):
    pltpu.make_async_copy(table_hbm.at[pl.ds(0, buf_ref.shape[0]), :], buf_ref, sem).wait()


def _rows_as_matrix(buf_ref, r0, n_rows):
    return jnp.concatenate(
        [buf_ref[pl.ds(r0 * ROW_TILES + c, n_rows, stride=ROW_TILES), :] for c in range(ROW_TILES)], axis=1)


def _store_rows(out_ref, y):
    for c in range(ROW_TILES):
        out_ref[pl.ds(c, y.shape[0], stride=ROW_TILES), :] = y[:, c * LANES:(c + 1) * LANES]


def _expert_kernel(be_ref, nu_ref, first_ref, nxt_ref, wslot_ref, tok_ref, tok_next_ref, x_hbm,
                   wgu_hbm, bgu_ref, wd_hbm, bd_ref, y_ref,
                   xbuf0, xbuf1, wgu_f0, wgu_f1, wd_f0, wd_f1, wgu_bf, wd_bf, sem, wsem, *, tb):
    b = pl.program_id(0)
    n_used = nu_ref[0]
    bufs = (xbuf0, xbuf1)
    wgu_f, wd_f = (wgu_f0, wgu_f1), (wd_f0, wd_f1)

    def weight_copies(e, s):
        return (pltpu.make_async_copy(wgu_hbm.at[e], wgu_f[s], wsem.at[s, 0]),
                pltpu.make_async_copy(wd_hbm.at[e], wd_f[s], wsem.at[s, 1]))

    @pl.when(b == 0)
    def _():
        for cp in weight_copies(be_ref[0], 0):
            cp.start()
        _gather_rows(tok_ref, tb, x_hbm, xbuf0, sem.at[0])

    for slot in range(2):
        @pl.when((b + 1 < n_used) & ((b + 1) % 2 == slot))
        def _():
            _gather_rows(tok_next_ref, tb, x_hbm, bufs[slot], sem.at[slot])

    @pl.when(b < n_used)
    def _():
        for s in range(2):
            @pl.when((first_ref[b] == 1) & (wslot_ref[b] == s))
            def _():
                for cp in weight_copies(be_ref[b], s):
                    cp.wait()
                wgu_bf[...] = wgu_f[s][...].astype(MXU_DTYPE)
                wd_bf[...] = wd_f[s][...].astype(MXU_DTYPE)

                @pl.when(nxt_ref[b] >= 0)
                def _():
                    for cp in weight_copies(nxt_ref[b], 1 - s):
                        cp.start()

        def compute(xbuf, slot):
            _wait_rows(x_hbm, xbuf, sem.at[slot])
            xr = _rows_as_matrix(xbuf, 0, tb)
            gu = _mm(xr, wgu_bf[...]) + bgu_ref[...]
            gt = jnp.minimum(gu[:, :D_FF], SWIGLU_LIMIT)
            up = jnp.clip(gu[:, D_FF:], -SWIGLU_LIMIT, SWIGLU_LIMIT)
            act = gt * jax.nn.sigmoid(SWIGLU_ALPHA * gt) * (up + 1.0)
            _store_rows(y_ref, _mm(act, wd_bf[...]) + bd_ref[...])

        for slot in range(2):
            pl.when(b % 2 == slot)(functools.partial(compute, bufs[slot], slot))

    @pl.when(b >= n_used)
    def _():
        y_ref[...] = jnp.zeros(y_ref.shape, y_ref.dtype)


def _experts(blk_exp, n_used, row_tok, x1_rows, w_gu, b_gu, w_down, b_down, tb):
    n_blocks = blk_exp.shape[0]
    blk = jnp.arange(n_blocks, dtype=jnp.int32)
    first = jnp.concatenate([jnp.ones((1,), jnp.int32), (blk_exp[1:] != blk_exp[:-1]).astype(jnp.int32)])
    wslot = (jnp.cumsum(first) - 1) % 2
    at_or_after = lax.cummin(jnp.where(first == 1, blk, n_blocks), reverse=True)
    next_first = jnp.concatenate([at_or_after[1:], jnp.full((1,), n_blocks, jnp.int32)])
    nxt = jnp.where(next_first < n_used[0], blk_exp[jnp.minimum(next_first, n_blocks - 1)], -1)
    bspec = lambda c: pl.BlockSpec((None, 1, c), lambda b, be, *_: (be[b], 0, 0))
    tok = row_tok.reshape(n_blocks, 1, tb)
    return pl.pallas_call(
        functools.partial(_expert_kernel, tb=tb),
        grid_spec=pltpu.PrefetchScalarGridSpec(
            num_scalar_prefetch=5,
            grid=(n_blocks,),
            in_specs=[pl.BlockSpec((1, 1, tb), lambda b, *_: (b, 0, 0), memory_space=pltpu.SMEM),
                      pl.BlockSpec((1, 1, tb), lambda b, *_: (jnp.minimum(b + 1, n_blocks - 1), 0, 0),
                                   memory_space=pltpu.SMEM),
                      pl.BlockSpec(memory_space=pl.ANY),
                      pl.BlockSpec(memory_space=pl.ANY), bspec(2 * D_FF),
                      pl.BlockSpec(memory_space=pl.ANY), bspec(D_MODEL)],
            out_specs=pl.BlockSpec((tb * ROW_TILES, LANES), lambda b, *_: (b, 0)),
            scratch_shapes=[pltpu.VMEM((tb * ROW_TILES, LANES), jnp.float32),
                            pltpu.VMEM((tb * ROW_TILES, LANES), jnp.float32),
                            pltpu.VMEM((D_MODEL, 2 * D_FF), jnp.float32),
                            pltpu.VMEM((D_MODEL, 2 * D_FF), jnp.float32),
                            pltpu.VMEM((D_FF, D_MODEL), jnp.float32),
                            pltpu.VMEM((D_FF, D_MODEL), jnp.float32),
                            pltpu.VMEM((D_MODEL, 2 * D_FF), MXU_DTYPE),
                            pltpu.VMEM((D_FF, D_MODEL), MXU_DTYPE),
                            pltpu.SemaphoreType.DMA((2,)),
                            pltpu.SemaphoreType.DMA((2, 2))]),
        out_shape=jax.ShapeDtypeStruct((n_blocks * tb * ROW_TILES, LANES), jnp.float32),
        compiler_params=_cparams(1),
        name="moe_experts",
    )(blk_exp, n_used, first, nxt.astype(jnp.int32), wslot.astype(jnp.int32), tok, tok, x1_rows,
      w_gu, b_gu, w_down, b_down)


def _combine_kernel(dest_ref, dest_next_ref, y_hbm, gate_ref, x1_ref, p_ref, wpg_ref, wpp_ref, g_ref, b_ref,
                    o_ref, ybuf0, ybuf1, sem, *, tm):
    i = pl.program_id(0)
    bufs = (ybuf0, ybuf1)

    @pl.when(i == 0)
    def _():
        _gather_rows(dest_ref, TOP_K * tm, y_hbm, ybuf0, sem.at[0])

    for slot in range(2):
        @pl.when((i + 1 < pl.num_programs(0)) & ((i + 1) % 2 == slot))
        def _():
            _gather_rows(dest_next_ref, TOP_K * tm, y_hbm, bufs[slot], sem.at[slot])

    x1 = _rows_as_matrix(x1_ref, 0, tm)
    ple = jax.nn.sigmoid(_mm(x1, wpg_ref[...])) * _mm(p_ref[...], wpp_ref[...])
    h = DN_ALPHA * x1 + ple
    gate = gate_ref[...]

    def finish(ybuf, slot):
        _wait_rows(y_hbm, ybuf, sem.at[slot])
        ffn = h
        for kk in range(TOP_K):
            ffn = ffn + _rows_as_matrix(ybuf, kk * tm, tm) * gate[:, kk:kk + 1]
        o_ref[...] = _layer_norm(ffn, g_ref[...], b_ref[...])

    for slot in range(2):
        pl.when(i % 2 == slot)(functools.partial(finish, bufs[slot], slot))


def _combine(dest, yr, gates, x1_rows, p, wpg, wpp, g, b, tm=256):
    n = p.shape[0]
    nt = n // tm
    full = lambda shape: pl.BlockSpec(shape, lambda i: (0,) * len(shape))
    row = lambda w: pl.BlockSpec((tm, w), lambda i: (i, 0))
    dest_t = dest.reshape(nt, tm, TOP_K).transpose(0, 2, 1).reshape(nt, 1, TOP_K * tm)
    return pl.pallas_call(
        functools.partial(_combine_kernel, tm=tm),
        grid=(nt,),
        in_specs=[pl.BlockSpec((1, 1, TOP_K * tm), lambda i: (i, 0, 0), memory_space=pltpu.SMEM),
                  pl.BlockSpec((1, 1, TOP_K * tm), lambda i: (jnp.minimum(i + 1, nt - 1), 0, 0),
                               memory_space=pltpu.SMEM),
                  pl.BlockSpec(memory_space=pl.ANY),
                  row(LANES), pl.BlockSpec((tm * ROW_TILES, LANES), lambda i: (i, 0)), row(PLE_DIM),
                  full((D_MODEL, D_MODEL)), full((PLE_DIM, D_MODEL)),
                  full((1, D_MODEL)), full((1, D_MODEL))],
        out_specs=row(D_MODEL),
        out_shape=jax.ShapeDtypeStruct((n, D_MODEL), jnp.float32),
        scratch_shapes=[pltpu.VMEM((TOP_K * tm * ROW_TILES, LANES), jnp.float32),
                        pltpu.VMEM((TOP_K * tm * ROW_TILES, LANES), jnp.float32),
                        pltpu.SemaphoreType.DMA((2,))],
        compiler_params=_cparams(1),
        name="combine_ple_ln2",
    )(dest_t, dest_t, yr, gates, x1_rows, p, wpg, wpp, g, b)


def _route(top_idx, n_tokens, tb):
    a = n_tokens * TOP_K
    n_blocks = a // tb + N_EXPERTS
    onehot = (top_idx[:, :, None] == jnp.arange(N_EXPERTS, dtype=jnp.int32)[None, None, :]).astype(jnp.int32)
    member = onehot.sum(axis=1)
    rank = jnp.cumsum(member, axis=0) - member
    counts = member.sum(axis=0)
    padded = (counts + tb - 1) // tb * tb
    pend = jnp.cumsum(padded)
    pstart = pend - padded
    dest = jnp.take_along_axis(rank + pstart[None, :], top_idx, axis=1).astype(jnp.int32)
    tok = jnp.broadcast_to(jnp.arange(n_tokens, dtype=jnp.int32)[:, None], (n_tokens, TOP_K))
    row_tok = jnp.zeros((n_blocks * tb,), jnp.int32).at[dest.reshape(-1)].set(
        tok.reshape(-1), unique_indices=True, mode="promise_in_bounds")
    blk_start = jnp.arange(n_blocks, dtype=jnp.int32) * tb
    blk_exp = jnp.minimum((blk_start[:, None] >= pend[None, :]).astype(jnp.int32).sum(axis=1), N_EXPERTS - 1)
    n_used = (pend[-1] // tb).astype(jnp.int32).reshape(1)
    return dest, row_tok, blk_exp, n_used


def kernel(x, p, ln0_g, ln0_b, w_in, pool_w, pool_scale, w_up_attn, w_up_pool, w_out, ln1_g, ln1_b,
           router_w, router_b, exp_w_gu, exp_b_gu, exp_w_down, exp_b_down, ple_w_gate, ple_w_proj,
           ln2_g, ln2_b):
    batch, seq, d = x.shape
    assert d == D_MODEL
    n = batch * seq
    tb = 256
    kc = 512
    tables = _rope_tables(seq)
    bf = lambda a: a.astype(MXU_DTYPE)
    vec = lambda a: a.reshape(1, -1)
    n_le = DEPTH * N_EXPERTS
    w_gu_all = exp_w_gu.reshape(n_le, D_MODEL, 2 * D_FF)
    b_gu_all = exp_b_gu.reshape(n_le, 1, 2 * D_FF)
    w_down_all = exp_w_down.reshape(n_le, D_FF, D_MODEL)
    b_down_all = exp_b_down.reshape(n_le, 1, D_MODEL)

    h = _ln0(x.reshape(n, d), ln0_g, ln0_b)
    for i in range(DEPTH):
        w = w_in[i]
        w1 = bf(jnp.concatenate([w[:, :1352], jnp.zeros((d, OFF_U - 1352), w.dtype), w[:, 1352:1864]], axis=1))
        wg = bf(w[:, 1864:])
        q, k, v_t, qi, ki, wi_t, u = _inproj(h, w1, tables, seq, kc)
        y_attn = _attention(qi, wi_t, q, ki, k, v_t, batch, seq, 512, kc)
        rw = bf(jnp.pad(router_w[i], ((0, 0), (0, LANES - N_EXPERTS))))
        rb = jnp.pad(router_b[i], (0, LANES - N_EXPERTS)).reshape(1, -1)
        x1_rows, idx_l, gate_l = _merge(h, y_attn, u, wg, bf(w_up_attn[i]), bf(pool_w[i]), vec(pool_scale[i]),
                                        bf(w_up_pool[i]), bf(w_out[i]), vec(ln1_g[i]), vec(ln1_b[i]), rw, rb, seq)
        dest, row_tok, blk_exp, n_used = _route(idx_l[:, :TOP_K], n, tb)
        yr = _experts(blk_exp + i * N_EXPERTS, n_used, row_tok, x1_rows,
                      w_gu_all, b_gu_all, w_down_all, b_down_all, tb)
        h = _combine(dest, yr, gate_l, x1_rows, p[i].reshape(n, PLE_DIM), bf(ple_w_gate[i]), bf(ple_w_proj[i]),
                     vec(ln2_g[i]), vec(ln2_b[i]))
    return h.reshape(batch, seq, d)
```

```python
import functools

import jax
import jax.numpy as jnp
import numpy as np
from jax import lax
from jax.experimental import pallas as pl
from jax.experimental.pallas import tpu as pltpu

MXU_DTYPE = jnp.bfloat16

D_MODEL = 1024
HEAD_DIM = 64
N_HEADS = 8
N_KV_HEADS = 2
ROT_DIM = 16
ROPE_THETA = 500000.0
IDX_HEADS = 8
IDX_DIM = 64
MAX_TOPK = 256
POOL_CH = 512
POOL_WINDOWS = (2, 4, 8, 16)
POOL_GROUP = 128
POOL_HALO = 16
N_EXPERTS = 32
TOP_K = 4
D_FF = 1024
SWIGLU_LIMIT = 7.0
SWIGLU_ALPHA = 1.702
PLE_DIM = 256
LN_EPS = 1e-5
DEPTH = 2
DN_ALPHA = (2 * DEPTH) ** 0.25
LOG2_E = 1.4426950408889634

LANES = 128
SUBLANES = 8
ROW_TILES = D_MODEL // LANES
assert ROW_TILES == SUBLANES

OFF_Q, OFF_K, OFF_V, OFF_QI, OFF_KIW, OFF_U, W1_COLS = 0, 512, 640, 768, 1280, 1408, 1920

VMEM_LIMIT = 56 * 1024 * 1024
NEG_INF = float("-inf")
KEY_NEG_INF = -2139095041
SEARCH_CAP = 20
HALF_BIAS = 32768
SEARCH_PASSES_PER_CHECK = 4
ACC_ROWS = 80
GATHER_UNROLL = 8

def _cparams(n_axes, flags=None):
    return pltpu.CompilerParams(dimension_semantics=("arbitrary",) * n_axes,
                                vmem_limit_bytes=VMEM_LIMIT, flags=flags)


def _mm(a, b):
    return jnp.dot(a.astype(MXU_DTYPE), b.astype(MXU_DTYPE), preferred_element_type=jnp.float32)


def _mm_nt(a, b):
    return lax.dot_general(a.astype(MXU_DTYPE), b.astype(MXU_DTYPE), (((1,), (1,)), ((), ())),
                           preferred_element_type=jnp.float32)


def _layer_norm(h, g, b):
    mu = jnp.mean(h, axis=-1, keepdims=True)
    c = h - mu
    var = jnp.mean(c * c, axis=-1, keepdims=True)
    return c * lax.rsqrt(var + LN_EPS) * g + b


def _rope_tables(seq):
    pos = jnp.arange(seq, dtype=jnp.float32)
    inv = ROPE_THETA ** (-jnp.arange(0, ROT_DIM, 2, dtype=jnp.float32) / ROT_DIM)
    ang = pos[:, None] * inv[None, :]
    half = ROT_DIM // 2
    sel_c = np.zeros((half, 2 * LANES), np.float32)
    sel_s1 = np.zeros((half, 2 * LANES), np.float32)
    sel_s2 = np.zeros((half, 2 * LANES), np.float32)
    ones = np.ones((1, 2 * LANES), np.float32)
    for lane in range(2 * LANES - HEAD_DIM):
        d = lane % HEAD_DIM
        if d < ROT_DIM:
            sel_c[d % half, lane], ones[0, lane] = 1.0, 0.0
            (sel_s1 if d < half else sel_s2)[d % half, lane] = -1.0 if d < half else 1.0
    spread = lambda a, sel: jnp.dot(a, jnp.asarray(sel), precision=lax.Precision.HIGHEST)
    cos, sin = jnp.cos(ang), jnp.sin(ang)
    return spread(cos, sel_c) + jnp.asarray(ones), spread(sin, sel_s1), spread(sin, sel_s2)


def _rope_tile(x, c, s1, s2):
    half = ROT_DIM // 2
    return x * c + pltpu.roll(x, LANES - half, axis=1) * s1 + pltpu.roll(x, half, axis=1) * s2


def _inproj_kernel(x_ref, w_ref, c_ref, s1_ref, s2_ref, *rest, entry_norm):
    if entry_norm:
        g_ref, b_ref, q_ref, k_ref, v_ref, qi_ref, ki_ref, wi_ref, u_ref, h_ref = rest
        h = _layer_norm(x_ref[...], g_ref[...], b_ref[...])
        h_ref[...] = h
    else:
        q_ref, k_ref, v_ref, qi_ref, ki_ref, wi_ref, u_ref = rest
        h = x_ref[...]
    xb = h.astype(MXU_DTYPE)
    c, s1, s2 = c_ref[:, :LANES], s1_ref[:, :LANES], s2_ref[:, :LANES]
    ck, s1k, s2k = c_ref[:, LANES:], s1_ref[:, LANES:], s2_ref[:, LANES:]

    def seg(off, width):
        return jnp.dot(xb, w_ref[:, off:off + width], preferred_element_type=jnp.float32)

    def roped(z, scale, out_ref):
        for j in range(z.shape[1] // LANES):
            t = _rope_tile(z[:, j * LANES:(j + 1) * LANES], c, s1, s2)
            out_ref[:, j * LANES:(j + 1) * LANES] = (t * scale).astype(out_ref.dtype)

    roped(seg(OFF_Q, 512), HEAD_DIM ** -0.5 * LOG2_E, q_ref)
    roped(seg(OFF_K, 128), 1.0, k_ref)
    v_ref[0] = seg(OFF_V, 128).T.astype(v_ref.dtype)
    roped(seg(OFF_QI, 512), IDX_DIM ** -0.5, qi_ref)
    kiw = seg(OFF_KIW, 128)
    ki_ref[...] = _rope_tile(kiw, ck, s1k, s2k).astype(ki_ref.dtype)
    wi_ref[...] = (kiw * (IDX_HEADS ** -0.5)).T
    u_ref[...] = seg(OFF_U, 512)


def _inproj(x, w1, tables, seq, tm, entry_ln=None):
    n = x.shape[0]
    nseq = seq // tm
    c, s1, s2 = tables
    tab_spec = pl.BlockSpec((tm, 2 * LANES), lambda i: (i % nseq, 0))

    def out(width, dtype):
        return pl.BlockSpec((tm, width), lambda i: (i, 0)), jax.ShapeDtypeStruct((n, width), dtype)

    v_t = (pl.BlockSpec((1, LANES, tm), lambda i: (i, 0, 0)), jax.ShapeDtypeStruct((n // tm, LANES, tm), MXU_DTYPE))
    wi_t = (pl.BlockSpec((LANES, tm), lambda i: (0, i)), jax.ShapeDtypeStruct((LANES, n), jnp.float32))
    outs = [out(512, MXU_DTYPE), out(128, MXU_DTYPE), v_t, out(512, MXU_DTYPE),
            out(128, MXU_DTYPE), wi_t, out(512, jnp.float32)]
    vec_spec = pl.BlockSpec((1, D_MODEL), lambda i: (0, 0))
    ln_specs, ln_args = [], []
    if entry_ln is not None:
        outs.append(out(D_MODEL, jnp.float32))
        ln_specs, ln_args = [vec_spec, vec_spec], [a.reshape(1, -1) for a in entry_ln]
    return pl.pallas_call(
        functools.partial(_inproj_kernel, entry_norm=entry_ln is not None),
        grid=(n // tm,),
        in_specs=[pl.BlockSpec((tm, D_MODEL), lambda i: (i, 0)),
                  pl.BlockSpec((D_MODEL, W1_COLS), lambda i: (0, 0)),
                  tab_spec, tab_spec, tab_spec] + ln_specs,
        out_specs=[o[0] for o in outs],
        out_shape=[o[1] for o in outs],
        compiler_params=_cparams(1),
        name="inproj",
    )(x, w1, c, s1, s2, *ln_args)


def _sortable_key(s):
    k = lax.bitcast_convert_type(s, jnp.int32)
    return k ^ ((k >> 31) & jnp.int32(0x7FFFFFFF))


def _attn_kernel(qi_ref, wi_ref, q_ref, ki_ref, k_ref, v_ref, tri_ref, o_ref,
                 key_ref, half_ref, s_ref, acc_ref, *, tq, kc, n_sel):
    j = pl.program_id(1)
    q0 = j * tq
    nch = (q0 + tq + kc - 1) // kc
    qpos = q0 + lax.broadcasted_iota(jnp.int32, (kc, tq), 1)
    krow = lax.broadcasted_iota(jnp.int32, (kc, tq), 0)

    def score_chunk(c, carry):
        ks = pl.multiple_of(c * kc, kc)
        kic = ki_ref[pl.ds(ks, kc), :][:, :IDX_DIM]
        acc = jnp.zeros((kc, tq), jnp.float32)
        for h in range(IDX_HEADS):
            d = _mm_nt(kic, qi_ref[:, h * IDX_DIM:(h + 1) * IDX_DIM])
            acc = acc + jnp.maximum(d, 0.0) * wi_ref[IDX_DIM + h:IDX_DIM + h + 1, :]
        causal = krow + ks <= qpos
        key = _sortable_key(jnp.where(causal, acc, NEG_INF))
        key_ref[c] = key
        half_ref[c] = (key >> 16).astype(jnp.int16)
        return carry

    lax.fori_loop(0, nch, score_chunk, 0)

    half_rows = 2 * SUBLANES

    def count_half_ge(cand):
        c16 = cand.astype(jnp.int16)
        c16 = jnp.concatenate([c16, c16], axis=0)
        def body(c, cnt):
            hit = jnp.where(half_ref[c].reshape(kc // half_rows, half_rows, tq) >= c16[None],
                            jnp.int16(1), jnp.int16(0))
            part = hit[0]
            for r in range(1, kc // half_rows):
                part = part + hit[r]
            return cnt + part
        cnt = lax.fori_loop(0, nch, body, jnp.zeros((half_rows, tq), jnp.int16)).astype(jnp.int32)
        cnt = cnt[0:SUBLANES, :] + cnt[SUBLANES:, :]
        for shift in (4, 2, 1):
            cnt = cnt + pltpu.roll(cnt, shift, axis=0)
        return cnt

    def bisect(base, state):
        def one_pass(st):
            lo, hi, c_lo, c_hi, done = st
            active = done == 0
            cand = jnp.where(active, lo + ((hi >> 1) - (lo >> 1)), lo)
            cnt = base + count_half_ge(cand)
            up = active & (cnt >= n_sel)
            dn = active & (cnt < n_sel)
            lo, c_lo = jnp.where(up, cand, lo), jnp.where(up, cnt, c_lo)
            hi, c_hi = jnp.where(dn, cand, hi), jnp.where(dn, cnt, c_hi)
            done = jnp.where((done != 0) | (c_lo == n_sel) | (hi - 1 <= lo), 1, 0)
            return lo, hi, c_lo, c_hi, done

        def step(st):
            inner = st[1:]
            for _ in range(SEARCH_PASSES_PER_CHECK):
                inner = one_pass(inner)
            return (st[0] + SEARCH_PASSES_PER_CHECK,) + inner

        cond = lambda st: (st[0] < SEARCH_CAP) & (jnp.min(st[5]) == 0)
        return lax.while_loop(cond, step, (jnp.int32(0),) + state)[1:]

    zeros =jnp.zeros((SUBLANES, tq), jnp.int32)
    n_valid = q0 + lax.broadcasted_iota(jnp.int32, (SUBLANES, tq), 1) + 1
    all_selected = n_valid <= n_sel

    h, _, c_ge_h, c_gt_h, _ = bisect(zeros, (jnp.full((SUBLANES, tq), -HALF_BIAS, jnp.int32),
                                             jnp.full((SUBLANES, tq), HALF_BIAS, jnp.int32),
                                             jnp.where(all_selected, n_valid, nch * kc), zeros,
                                             all_selected.astype(jnp.int32)))
    settled = all_selected | (c_ge_h == n_sel)

    def low_chunk(c, carry):
        key = key_ref[c]
        low = jnp.where((key >> 16) == h[0:1, :], (key & 0xFFFF) - HALF_BIAS, -HALF_BIAS)
        half_ref[c] = low.astype(jnp.int16)
        return carry

    lax.fori_loop(0, nch, low_chunk, 0)
    low, _, c_lo, c_hi, _ = bisect(c_gt_h, (jnp.full((SUBLANES, tq), -HALF_BIAS, jnp.int32),
                                            jnp.full((SUBLANES, tq), HALF_BIAS, jnp.int32),
                                            c_ge_h, c_gt_h, settled.astype(jnp.int32)))
    thr = jnp.where(all_selected, KEY_NEG_INF, h * (2 * HALF_BIAS) + (low + HALF_BIAS))
    thr, c_lo, c_hi = thr[0:1, :], c_lo[0:1, :], c_hi[0:1, :]
    need = jnp.where(c_lo == n_sel, float(2 ** 30), (n_sel - c_hi).astype(jnp.float32))

    def bias_chunk(with_ties, c, n_eq_before):
        ks = pl.multiple_of(c * kc, kc)
        key = key_ref[c]
        causal = krow + ks <= qpos
        if with_ties:
            eq = key == thr
            rank = _mm(tri_ref[...], jnp.where(eq, 1.0, 0.0)) + n_eq_before
            sel = ((key > thr) | (eq & (rank <= need))) & causal
            n_eq_before = rank[kc - 1:kc, :]
        else:
            sel = (key >= thr) & causal
        key_ref[c] = lax.bitcast_convert_type(jnp.where(sel, 0.0, NEG_INF), jnp.int32)
        return n_eq_before

    has_ties = jnp.max(c_lo) > n_sel

    @pl.when(has_ties)
    def _():
        lax.fori_loop(0, nch, functools.partial(bias_chunk, True), jnp.zeros((1, tq), jnp.float32))

    @pl.when(jnp.logical_not(has_ties))
    def _():
        lax.fori_loop(0, nch, functools.partial(bias_chunk, False), jnp.zeros((1, tq), jnp.float32))

    acc_ref[...] = jnp.zeros(acc_ref.shape, jnp.float32)
    group = N_HEADS // N_KV_HEADS
    ones_rows = jnp.ones((ACC_ROWS - HEAD_DIM, kc), MXU_DTYPE)

    def attend_chunk(c, ms):
        ks = pl.multiple_of(c * kc, kc)
        bias = lax.bitcast_convert_type(key_ref[c], jnp.float32)
        kch = k_ref[pl.ds(ks, kc), :]
        vch = v_ref[c]
        vext = [jnp.concatenate([vch[g * HEAD_DIM:(g + 1) * HEAD_DIM, :], ones_rows], axis=0)
                for g in range(N_KV_HEADS)]
        slot0 = jnp.minimum(c, 0)
        mx = []
        for h in range(N_HEADS):
            g = h // group
            s = _mm_nt(kch[:, g * HEAD_DIM:(g + 1) * HEAD_DIM], q_ref[:, h * HEAD_DIM:(h + 1) * HEAD_DIM]) + bias
            s_ref[slot0 + h] = s
            mx.append(jnp.max(s, axis=0, keepdims=True))
        new_ms = []
        for h in range(N_HEADS):
            m_new = jnp.maximum(ms[h], mx[h])
            p = jnp.exp2(s_ref[slot0 + h] - m_new)
            acc_ref[h] = jnp.exp2(ms[h] - m_new) * acc_ref[h] + _mm(vext[h // group], p)
            new_ms.append(m_new)
        return tuple(new_ms)

    lax.fori_loop(0, nch, attend_chunk, tuple(jnp.full((1, tq), -1e30, jnp.float32) for _ in range(N_HEADS)))

    out_t = jnp.concatenate([acc_ref[h, 0:HEAD_DIM, :] / acc_ref[h, HEAD_DIM:HEAD_DIM + 1, :]
                             for h in range(N_HEADS)], axis=0)
    o_ref[...] = out_t.T.astype(o_ref.dtype)


def _attention(qi, wi_t, q, ki, k, v_t, batch, seq, tq, kc):
    n = q.shape[0]
    n_sel = min(MAX_TOPK, seq // 4)
    nq = seq // tq
    nck = seq // kc
    assert seq % tq == 0 and seq % kc == 0 and kc >= n_sel and v_t.shape == (n // kc, LANES, kc)
    tri = jnp.asarray((np.arange(kc)[None, :] <= np.arange(kc)[:, None]).astype(np.float32), MXU_DTYPE)
    qblk = lambda w: pl.BlockSpec((tq, w), lambda b, j: (b * nq + j, 0))
    seqblk = pl.BlockSpec((seq, LANES), lambda b, j: (b, 0))
    return pl.pallas_call(
        functools.partial(_attn_kernel, tq=tq, kc=kc, n_sel=n_sel),
        grid=(batch, nq),
        in_specs=[qblk(512), pl.BlockSpec((LANES, tq), lambda b, j: (0, b * nq + j)), qblk(512),
                  seqblk, seqblk, pl.BlockSpec((nck, LANES, kc), lambda b, j: (b, 0, 0)),
                  pl.BlockSpec((kc, kc), lambda b, j: (0, 0))],
        out_specs=qblk(512),
        out_shape=jax.ShapeDtypeStruct((n, N_HEADS * HEAD_DIM), MXU_DTYPE),
        scratch_shapes=[pltpu.VMEM((nck, kc, tq), jnp.int32),
                        pltpu.VMEM((nck, kc, tq), jnp.int16),
                        pltpu.VMEM((N_HEADS, kc, tq), jnp.float32),
                        pltpu.VMEM((N_HEADS, ACC_ROWS, tq), jnp.float32)],
        compiler_params=_cparams(2),
        name="dsa_attention",
    )(qi, wi_t, q, ki, k, v_t, tri)


def _merge_kernel(x_ref, ya_ref, u_ref, uh_ref, wg_ref, wua_ref, pw_ref, ps_ref, wup_ref, wo_ref,
                  g_ref, b_ref, rw_ref, rb_ref,
                  x1_ref, idx_ref, gate_ref, ext_ref, *, tm, seq):
    i = pl.program_id(0)
    x = x_ref[...]
    xb = x.astype(MXU_DTYPE)

    first = (i % (seq // tm)) == 0
    ext_ref[0:POOL_HALO, :] = jnp.where(first, 0.0, uh_ref[...])
    ext_ref[POOL_HALO:, :] = u_ref[...]
    pos1 = ((i % (seq // tm)) * tm + 1 + lax.broadcasted_iota(jnp.int32, (tm, 1), 0)).astype(jnp.float32)
    parts = []
    for g, win in enumerate(POOL_WINDOWS):
        e = ext_ref[:, g * POOL_GROUP:(g + 1) * POOL_GROUP]
        w = 1
        while w < win:
            e = e + pltpu.roll(e, w, axis=0)
            w *= 2
        tok = e[POOL_HALO:, :]
        ug = u_ref[:, g * POOL_GROUP:(g + 1) * POOL_GROUP]
        d = tok / jnp.minimum(pos1, float(win)) - ug
        parts.append(_mm(d, pw_ref[g]))
    y_pool = jnp.concatenate(parts, axis=1) * ps_ref[...]

    gates = jnp.dot(xb, wg_ref[...], preferred_element_type=jnp.float32)
    merged = (jax.nn.sigmoid(gates[:, :D_MODEL]) * _mm(ya_ref[...], wua_ref[...])
              + jax.nn.sigmoid(gates[:, D_MODEL:]) * _mm(y_pool, wup_ref[...]))
    mix = _mm(merged, wo_ref[...])
    x1 = _layer_norm(DN_ALPHA * x + mix, g_ref[...], b_ref[...])
    _store_rows(x1_ref, x1)

    logits = _mm(x1, rw_ref[...]) + rb_ref[...]
    lane = lax.broadcasted_iota(jnp.int32, logits.shape, 1)
    work = jnp.where(lane < N_EXPERTS, logits, NEG_INF)
    vals, idxs = [], []
    for _ in range(TOP_K):
        m = jnp.max(work, axis=1, keepdims=True)
        ix = jnp.min(jnp.where(work == m, lane, LANES), axis=1, keepdims=True)
        vals.append(m)
        idxs.append(ix)
        work = jnp.where(lane == ix, NEG_INF, work)
    es = [jnp.exp(vv - vals[0]) for vv in vals]
    den = es[0] + es[1] + es[2] + es[3]
    idx_out = jnp.zeros(logits.shape, jnp.int32)
    gate_out = jnp.zeros(logits.shape, jnp.float32)
    for kk in range(TOP_K):
        idx_out = jnp.where(lane == kk, idxs[kk], idx_out)
        gate_out = jnp.where(lane == kk, es[kk] / den, gate_out)
    idx_ref[...] = idx_out
    gate_ref[...] = gate_out


def _merge(x, y_attn, u, wg, wua, pw, ps, wup, wo, g, b, rw, rb, seq, tm=512):
    n = x.shape[0]
    hb = tm // POOL_HALO
    full = lambda shape: pl.BlockSpec(shape, lambda i: (0,) * len(shape))
    row = lambda w: pl.BlockSpec((tm, w), lambda i: (i, 0))
    return pl.pallas_call(
        functools.partial(_merge_kernel, tm=tm, seq=seq),
        grid=(n // tm,),
        in_specs=[row(D_MODEL), row(512), row(POOL_CH),
                  pl.BlockSpec((POOL_HALO, POOL_CH), lambda i: (jnp.maximum(i * hb - 1, 0), 0)),
                  full((D_MODEL, 2 * D_MODEL)), full((512, D_MODEL)),
                  full((4, POOL_GROUP, POOL_GROUP)), full((1, POOL_CH)), full((POOL_CH, D_MODEL)),
                  full((D_MODEL, D_MODEL)), full((1, D_MODEL)), full((1, D_MODEL)),
                  full((D_MODEL, LANES)), full((1, LANES))],
        out_specs=[pl.BlockSpec((tm * ROW_TILES, LANES), lambda i: (i, 0)), row(LANES), row(LANES)],
        out_shape=[jax.ShapeDtypeStruct((n * ROW_TILES, LANES), jnp.float32),
                   jax.ShapeDtypeStruct((n, LANES), jnp.int32),
                   jax.ShapeDtypeStruct((n, LANES), jnp.float32)],
        scratch_shapes=[pltpu.VMEM((tm + POOL_HALO, POOL_CH), jnp.float32)],
        compiler_params=_cparams(1),
        name="merge_ln1_router",
    )(x, y_attn, u, u, wg, wua, pw, ps, wup, wo, g, b, rw, rb)


def _gather_rows(idx_ref, n_rows, table_hbm, buf_ref, sem):
    def issue(i, carry):
        for j in range(GATHER_UNROLL):
            r = i * GATHER_UNROLL + j
            t = idx_ref[0, 0, r]
            pltpu.make_async_copy(table_hbm.at[pl.ds(pl.multiple_of(t * ROW_TILES, ROW_TILES), ROW_TILES), :],
                                  buf_ref.at[pl.ds(pl.multiple_of(r * ROW_TILES, ROW_TILES), ROW_TILES), :],
                                  sem).start()
        return carry
    assert n_rows % GATHER_UNROLL == 0
    lax.fori_loop(0, n_rows // GATHER_UNROLL, issue, 0)


def _wait_rows(table_hbm, buf_ref, sem):
    pltpu.make_async_copy(table_hbm.at[pl.ds(0, buf_ref.shape[0]), :], buf_ref, sem).wait()


def _rows_as_matrix(buf_ref, r0, n_rows):
    return jnp.concatenate(
        [buf_ref[pl.ds(r0 * ROW_TILES + c, n_rows, stride=ROW_TILES), :] for c in range(ROW_TILES)], axis=1)


def _store_rows(out_ref, y):
    for c in range(ROW_TILES):
        out_ref[pl.ds(c, y.shape[0], stride=ROW_TILES), :] = y[:, c * LANES:(c + 1) * LANES]


def _expert_kernel(be_ref, nu_ref, first_ref, nxt_ref, wslot_ref, tok_ref, tok_next_ref, x_hbm,
                   wgu_hbm, bgu_ref, wd_hbm, bd_ref, y_ref,
                   xbuf0, xbuf1, wgu_f0, wgu_f1, wd_f0, wd_f1, wgu_bf, wd_bf, sem, wsem, *, tb):
    b = pl.program_id(0)
    n_used = nu_ref[0]
    bufs = (xbuf0, xbuf1)
    wgu_f, wd_f = (wgu_f0, wgu_f1), (wd_f0, wd_f1)

    def weight_copies(e, s):
        return (pltpu.make_async_copy(wgu_hbm.at[e], wgu_f[s], wsem.at[s, 0]),
                pltpu.make_async_copy(wd_hbm.at[e], wd_f[s], wsem.at[s, 1]))

    @pl.when(b == 0)
    def _():
        for cp in weight_copies(be_ref[0], 0):
            cp.start()
        _gather_rows(tok_ref, tb, x_hbm, xbuf0, sem.at[0])

    for slot in range(2):
        @pl.when((b + 1 < n_used) & ((b + 1) % 2 == slot))
        def _():
            _gather_rows(tok_next_ref, tb, x_hbm, bufs[slot], sem.at[slot])

    @pl.when(b < n_used)
    def _():
        for s in range(2):
            @pl.when((first_ref[b] == 1) & (wslot_ref[b] == s))
            def _():
                for cp in weight_copies(be_ref[b], s):
                    cp.wait()
                wgu_bf[...] = wgu_f[s][...].astype(MXU_DTYPE)
                wd_bf[...] = wd_f[s][...].astype(MXU_DTYPE)

                @pl.when(nxt_ref[b] >= 0)
                def _():
                    for cp in weight_copies(nxt_ref[b], 1 - s):
                        cp.start()

        def compute(xbuf, slot):
            _wait_rows(x_hbm, xbuf, sem.at[slot])
            xr = _rows_as_matrix(xbuf, 0, tb)
            gu = _mm(xr, wgu_bf[...]) + bgu_ref[...]
            gt = jnp.minimum(gu[:, :D_FF], SWIGLU_LIMIT)
            up = jnp.clip(gu[:, D_FF:], -SWIGLU_LIMIT, SWIGLU_LIMIT)
            act = gt * jax.nn.sigmoid(SWIGLU_ALPHA * gt) * (up + 1.0)
            _store_rows(y_ref, _mm(act, wd_bf[...]) + bd_ref[...])

        for slot in range(2):
            pl.when(b % 2 == slot)(functools.partial(compute, bufs[slot], slot))

    @pl.when(b >= n_used)
    def _():
        y_ref[...] = jnp.zeros(y_ref.shape, y_ref.dtype)


def _experts(blk_exp, n_used, row_tok, x1_rows, w_gu, b_gu, w_down, b_down, tb):
    n_blocks = blk_exp.shape[0]
    blk = jnp.arange(n_blocks, dtype=jnp.int32)
    first = jnp.concatenate([jnp.ones((1,), jnp.int32), (blk_exp[1:] != blk_exp[:-1]).astype(jnp.int32)])
    wslot = (jnp.cumsum(first) - 1) % 2
    at_or_after = lax.cummin(jnp.where(first == 1, blk, n_blocks), reverse=True)
    next_first = jnp.concatenate([at_or_after[1:], jnp.full((1,), n_blocks, jnp.int32)])
    nxt = jnp.where(next_first < n_used[0], blk_exp[jnp.minimum(next_first, n_blocks - 1)], -1)
    bspec = lambda c: pl.BlockSpec((None, 1, c), lambda b, be, *_: (be[b], 0, 0))
    tok = row_tok.reshape(n_blocks, 1, tb)
    return pl.pallas_call(
        functools.partial(_expert_kernel, tb=tb),
        grid_spec=pltpu.PrefetchScalarGridSpec(
            num_scalar_prefetch=5,
            grid=(n_blocks,),
            in_specs=[pl.BlockSpec((1, 1, tb), lambda b, *_: (b, 0, 0), memory_space=pltpu.SMEM),
                      pl.BlockSpec((1, 1, tb), lambda b, *_: (jnp.minimum(b + 1, n_blocks - 1), 0, 0),
                                   memory_space=pltpu.SMEM),
                      pl.BlockSpec(memory_space=pl.ANY),
                      pl.BlockSpec(memory_space=pl.ANY), bspec(2 * D_FF),
                      pl.BlockSpec(memory_space=pl.ANY), bspec(D_MODEL)],
            out_specs=pl.BlockSpec((tb * ROW_TILES, LANES), lambda b, *_: (b, 0)),
            scratch_shapes=[pltpu.VMEM((tb * ROW_TILES, LANES), jnp.float32),
                            pltpu.VMEM((tb * ROW_TILES, LANES), jnp.float32),
                            pltpu.VMEM((D_MODEL, 2 * D_FF), jnp.float32),
                            pltpu.VMEM((D_MODEL, 2 * D_FF), jnp.float32),
                            pltpu.VMEM((D_FF, D_MODEL), jnp.float32),
                            pltpu.VMEM((D_FF, D_MODEL), jnp.float32),
                            pltpu.VMEM((D_MODEL, 2 * D_FF), MXU_DTYPE),
                            pltpu.VMEM((D_FF, D_MODEL), MXU_DTYPE),
                            pltpu.SemaphoreType.DMA((2,)),
                            pltpu.SemaphoreType.DMA((2, 2))]),
        out_shape=jax.ShapeDtypeStruct((n_blocks * tb * ROW_TILES, LANES), jnp.float32),
        compiler_params=_cparams(1),
        name="moe_experts",
    )(blk_exp, n_used, first, nxt.astype(jnp.int32), wslot.astype(jnp.int32), tok, tok, x1_rows,
      w_gu, b_gu, w_down, b_down)


def _combine_kernel(dest_ref, dest_next_ref, y_hbm, gate_ref, x1_ref, p_ref, wpg_ref, wpp_ref, g_ref, b_ref,
                    o_ref, ybuf0, ybuf1, sem, *, tm):
    i = pl.program_id(0)
    bufs = (ybuf0, ybuf1)

    @pl.when(i == 0)
    def _():
        _gather_rows(dest_ref, TOP_K * tm, y_hbm, ybuf0, sem.at[0])

    for slot in range(2):
        @pl.when((i + 1 < pl.num_programs(0)) & ((i + 1) % 2 == slot))
        def _():
            _gather_rows(dest_next_ref, TOP_K * tm, y_hbm, bufs[slot], sem.at[slot])

    x1 = _rows_as_matrix(x1_ref, 0, tm)
    ple = jax.nn.sigmoid(_mm(x1, wpg_ref[...])) * _mm(p_ref[...], wpp_ref[...])
    h = DN_ALPHA * x1 + ple
    gate = gate_ref[...]

    def finish(ybuf, slot):
        _wait_rows(y_hbm, ybuf, sem.at[slot])
        ffn = h
        for kk in range(TOP_K):
            ffn = ffn + _rows_as_matrix(ybuf, kk * tm, tm) * gate[:, kk:kk + 1]
        o_ref[...] = _layer_norm(ffn, g_ref[...], b_ref[...])

    for slot in range(2):
        pl.when(i % 2 == slot)(functools.partial(finish, bufs[slot], slot))


def _combine(dest, yr, gates, x1_rows, p_all, layer, wpg, wpp, g, b, tm=256):
    n = gates.shape[0]
    nt = n // tm
    full = lambda shape: pl.BlockSpec(shape, lambda i: (0,) * len(shape))
    row = lambda w: pl.BlockSpec((tm, w), lambda i: (i, 0))
    dest_t = dest.reshape(nt, tm, TOP_K).transpose(0, 2, 1).reshape(nt, 1, TOP_K * tm)
    return pl.pallas_call(
        functools.partial(_combine_kernel, tm=tm),
        grid=(nt,),
        in_specs=[pl.BlockSpec((1, 1, TOP_K * tm), lambda i: (i, 0, 0), memory_space=pltpu.SMEM),
                  pl.BlockSpec((1, 1, TOP_K * tm), lambda i: (jnp.minimum(i + 1, nt - 1), 0, 0),
                               memory_space=pltpu.SMEM),
                  pl.BlockSpec(memory_space=pl.ANY),
                  row(LANES), pl.BlockSpec((tm * ROW_TILES, LANES), lambda i: (i, 0)),
                  pl.BlockSpec((tm, PLE_DIM), lambda i: (layer * nt + i, 0)),
                  full((D_MODEL, D_MODEL)), full((PLE_DIM, D_MODEL)),
                  full((1, D_MODEL)), full((1, D_MODEL))],
        out_specs=row(D_MODEL),
        out_shape=jax.ShapeDtypeStruct((n, D_MODEL), jnp.float32),
        scratch_shapes=[pltpu.VMEM((TOP_K * tm * ROW_TILES, LANES), jnp.float32),
                        pltpu.VMEM((TOP_K * tm * ROW_TILES, LANES), jnp.float32),
                        pltpu.SemaphoreType.DMA((2,))],
        compiler_params=_cparams(1),
        name="combine_ple_ln2",
    )(dest_t, dest_t, yr, gates, x1_rows, p_all, wpg, wpp, g, b)


def _route(top_idx, n_tokens, tb):
    a = n_tokens * TOP_K
    n_blocks = a // tb + N_EXPERTS
    onehot = (top_idx[:, :, None] == jnp.arange(N_EXPERTS, dtype=jnp.int32)[None, None, :]).astype(jnp.int32)
    member = onehot.sum(axis=1)
    rank = jnp.cumsum(member, axis=0) - member
    counts = member.sum(axis=0)
    padded = (counts + tb - 1) // tb * tb
    pend = jnp.cumsum(padded)
    pstart = pend - padded
    dest = jnp.take_along_axis(rank + pstart[None, :], top_idx, axis=1).astype(jnp.int32)
    tok = jnp.broadcast_to(jnp.arange(n_tokens, dtype=jnp.int32)[:, None], (n_tokens, TOP_K))
    row_tok = jnp.zeros((n_blocks * tb,), jnp.int32).at[dest.reshape(-1)].set(
        tok.reshape(-1), unique_indices=True, mode="promise_in_bounds")
    blk_start = jnp.arange(n_blocks, dtype=jnp.int32) * tb
    blk_exp = jnp.minimum((blk_start[:, None] >= pend[None, :]).astype(jnp.int32).sum(axis=1), N_EXPERTS - 1)
    n_used = (pend[-1] // tb).astype(jnp.int32).reshape(1)
    return dest, row_tok, blk_exp, n_used


def kernel(x, p, ln0_g, ln0_b, w_in, pool_w, pool_scale, w_up_attn, w_up_pool, w_out, ln1_g, ln1_b,
           router_w, router_b, exp_w_gu, exp_b_gu, exp_w_down, exp_b_down, ple_w_gate, ple_w_proj,
           ln2_g, ln2_b):
    batch, seq, d = x.shape
    assert d == D_MODEL
    n = batch * seq
    tb = 256
    kc = 512
    tables = _rope_tables(seq)
    bf = lambda a: a.astype(MXU_DTYPE)
    vec = lambda a: a.reshape(1, -1)
    n_le = DEPTH * N_EXPERTS
    w_gu_all = exp_w_gu.reshape(n_le, D_MODEL, 2 * D_FF)
    b_gu_all = exp_b_gu.reshape(n_le, 1, 2 * D_FF)
    w_down_all = exp_w_down.reshape(n_le, D_FF, D_MODEL)
    b_down_all = exp_b_down.reshape(n_le, 1, D_MODEL)
    p_all = p.reshape(DEPTH * n, PLE_DIM)

    h = x.reshape(n, d)
    for i in range(DEPTH):
        w = w_in[i]
        w1 = bf(jnp.concatenate([w[:, :1352], jnp.zeros((d, OFF_U - 1352), w.dtype), w[:, 1352:1864]], axis=1))
        wg = bf(w[:, 1864:])
        if i == 0:
            q, k, v_t, qi, ki, wi_t, u, h = _inproj(h, w1, tables, seq, kc, entry_ln=(ln0_g, ln0_b))
        else:
            q, k, v_t, qi, ki, wi_t, u = _inproj(h, w1, tables, seq, kc)
        y_attn = _attention(qi, wi_t, q, ki, k, v_t, batch, seq, 512, kc)
        rw = bf(jnp.pad(router_w[i], ((0, 0), (0, LANES - N_EXPERTS))))
        rb = jnp.pad(router_b[i], (0, LANES - N_EXPERTS)).reshape(1, -1)
        x1_rows, idx_l, gate_l = _merge(h, y_attn, u, wg, bf(w_up_attn[i]), bf(pool_w[i]), vec(pool_scale[i]),
                                        bf(w_up_pool[i]), bf(w_out[i]), vec(ln1_g[i]), vec(ln1_b[i]), rw, rb, seq)
        dest, row_tok, blk_exp, n_used = _route(idx_l[:, :TOP_K], n, tb)
        yr = _experts(blk_exp + i * N_EXPERTS, n_used, row_tok, x1_rows,
                      w_gu_all, b_gu_all, w_down_all, b_down_all, tb)
        h = _combine(dest, yr, gate_l, x1_rows, p_all, i, bf(ple_w_gate[i]), bf(ple_w_proj[i]),
                     vec(ln2_g[i]), vec(ln2_b[i]))
    return h.reshape(batch, seq, d)
```

```python
import functools

import jax
import jax.numpy as jnp
import numpy as np
from jax import lax
from jax.experimental import pallas as pl
from jax.experimental.pallas import tpu as pltpu

MXU_DTYPE = jnp.bfloat16

D_MODEL = 1024
HEAD_DIM = 64
N_HEADS = 8
N_KV_HEADS = 2
ROT_DIM = 16
ROPE_THETA = 500000.0
IDX_HEADS = 8
IDX_DIM = 64
MAX_TOPK = 256
POOL_CH = 512
POOL_WINDOWS = (2, 4, 8, 16)
POOL_GROUP = 128
POOL_HALO = 16
N_EXPERTS = 32
TOP_K = 4
D_FF = 1024
SWIGLU_LIMIT = 7.0
SWIGLU_ALPHA = 1.702
PLE_DIM = 256
LN_EPS = 1e-5
DEPTH = 2
DN_ALPHA = (2 * DEPTH) ** 0.25
LOG2_E = 1.4426950408889634

LANES = 128
SUBLANES = 8
ROW_TILES = D_MODEL // LANES
assert ROW_TILES == SUBLANES

OFF_Q, OFF_K, OFF_V, OFF_QI, OFF_KIW, OFF_U, W1_COLS = 0, 512, 640, 768, 1280, 1408, 1920

VMEM_LIMIT = 56 * 1024 * 1024
NEG_INF = float("-inf")
KEY_NEG_INF = -2139095041
SEARCH_CAP = 20
HALF_BIAS = 32768
SEARCH_PASSES_PER_CHECK = 4
ACC_ROWS = 80
GATHER_UNROLL = 8
WEIGHT_DMA_PRIORITY = 1

def _cparams(n_axes, flags=None):
    return pltpu.CompilerParams(dimension_semantics=("arbitrary",) * n_axes,
                                vmem_limit_bytes=VMEM_LIMIT, flags=flags)


def _mm(a, b):
    return jnp.dot(a.astype(MXU_DTYPE), b.astype(MXU_DTYPE), preferred_element_type=jnp.float32)


def _mm_nt(a, b):
    return lax.dot_general(a.astype(MXU_DTYPE), b.astype(MXU_DTYPE), (((1,), (1,)), ((), ())),
                           preferred_element_type=jnp.float32)


def _layer_norm(h, g, b):
    mu = jnp.mean(h, axis=-1, keepdims=True)
    c = h - mu
    var = jnp.mean(c * c, axis=-1, keepdims=True)
    return c * lax.rsqrt(var + LN_EPS) * g + b


def _rope_tables(seq):
    pos = jnp.arange(seq, dtype=jnp.float32)
    inv = ROPE_THETA ** (-jnp.arange(0, ROT_DIM, 2, dtype=jnp.float32) / ROT_DIM)
    ang = pos[:, None] * inv[None, :]
    half = ROT_DIM // 2
    sel_c = np.zeros((half, 2 * LANES), np.float32)
    sel_s1 = np.zeros((half, 2 * LANES), np.float32)
    sel_s2 = np.zeros((half, 2 * LANES), np.float32)
    ones = np.ones((1, 2 * LANES), np.float32)
    for lane in range(2 * LANES - HEAD_DIM):
        d = lane % HEAD_DIM
        if d < ROT_DIM:
            sel_c[d % half, lane], ones[0, lane] = 1.0, 0.0
            (sel_s1 if d < half else sel_s2)[d % half, lane] = -1.0 if d < half else 1.0
    spread = lambda a, sel: jnp.dot(a, jnp.asarray(sel), precision=lax.Precision.HIGHEST)
    cos, sin = jnp.cos(ang), jnp.sin(ang)
    return spread(cos, sel_c) + jnp.asarray(ones), spread(sin, sel_s1), spread(sin, sel_s2)


def _rope_tile(x, c, s1, s2):
    half = ROT_DIM // 2
    return x * c + pltpu.roll(x, LANES - half, axis=1) * s1 + pltpu.roll(x, half, axis=1) * s2


def _inproj_kernel(x_ref, w_ref, c_ref, s1_ref, s2_ref, *rest, entry_norm):
    if entry_norm:
        g_ref, b_ref, q_ref, k_ref, v_ref, qi_ref, ki_ref, wi_ref, u_ref, h_ref = rest
        h = _layer_norm(x_ref[...], g_ref[...], b_ref[...])
        h_ref[...] = h
    else:
        q_ref, k_ref, v_ref, qi_ref, ki_ref, wi_ref, u_ref = rest
        h = x_ref[...]
    xb = h.astype(MXU_DTYPE)
    c, s1, s2 = c_ref[:, :LANES], s1_ref[:, :LANES], s2_ref[:, :LANES]
    ck, s1k, s2k = c_ref[:, LANES:], s1_ref[:, LANES:], s2_ref[:, LANES:]

    def seg(off, width):
        return jnp.dot(xb, w_ref[:, off:off + width], preferred_element_type=jnp.float32)

    def roped(z, scale, out_ref):
        for j in range(z.shape[1] // LANES):
            t = _rope_tile(z[:, j * LANES:(j + 1) * LANES], c, s1, s2)
            out_ref[:, j * LANES:(j + 1) * LANES] = (t * scale).astype(out_ref.dtype)

    roped(seg(OFF_Q, 512), HEAD_DIM ** -0.5 * LOG2_E, q_ref)
    roped(seg(OFF_K, 128), 1.0, k_ref)
    v_ref[0] = seg(OFF_V, 128).T.astype(v_ref.dtype)
    roped(seg(OFF_QI, 512), IDX_DIM ** -0.5, qi_ref)
    kiw = seg(OFF_KIW, 128)
    ki_ref[...] = _rope_tile(kiw, ck, s1k, s2k).astype(ki_ref.dtype)
    wi_ref[...] = (kiw * (IDX_HEADS ** -0.5)).T
    u_ref[...] = seg(OFF_U, 512)


def _inproj(x, w1, tables, seq, tm, entry_ln=None):
    n = x.shape[0]
    nseq = seq // tm
    c, s1, s2 = tables
    tab_spec = pl.BlockSpec((tm, 2 * LANES), lambda i: (i % nseq, 0))

    def out(width, dtype):
        return pl.BlockSpec((tm, width), lambda i: (i, 0)), jax.ShapeDtypeStruct((n, width), dtype)

    v_t = (pl.BlockSpec((1, LANES, tm), lambda i: (i, 0, 0)), jax.ShapeDtypeStruct((n // tm, LANES, tm), MXU_DTYPE))
    wi_t = (pl.BlockSpec((LANES, tm), lambda i: (0, i)), jax.ShapeDtypeStruct((LANES, n), jnp.float32))
    outs = [out(512, MXU_DTYPE), out(128, MXU_DTYPE), v_t, out(512, MXU_DTYPE),
            out(128, MXU_DTYPE), wi_t, out(512, jnp.float32)]
    vec_spec = pl.BlockSpec((1, D_MODEL), lambda i: (0, 0))
    ln_specs, ln_args = [], []
    if entry_ln is not None:
        outs.append(out(D_MODEL, jnp.float32))
        ln_specs, ln_args = [vec_spec, vec_spec], [a.reshape(1, -1) for a in entry_ln]
    return pl.pallas_call(
        functools.partial(_inproj_kernel, entry_norm=entry_ln is not None),
        grid=(n // tm,),
        in_specs=[pl.BlockSpec((tm, D_MODEL), lambda i: (i, 0)),
                  pl.BlockSpec((D_MODEL, W1_COLS), lambda i: (0, 0)),
                  tab_spec, tab_spec, tab_spec] + ln_specs,
        out_specs=[o[0] for o in outs],
        out_shape=[o[1] for o in outs],
        compiler_params=_cparams(1),
        name="inproj",
    )(x, w1, c, s1, s2, *ln_args)


def _sortable_key(s):
    k = lax.bitcast_convert_type(s, jnp.int32)
    return k ^ ((k >> 31) & jnp.int32(0x7FFFFFFF))


def _attn_kernel(qi_ref, wi_ref, q_ref, ki_ref, k_ref, v_ref, tri_ref, o_ref,
                 key_ref, half_ref, s_ref, acc_ref, *, tq, kc, n_sel):
    j = pl.program_id(1)
    q0 = j * tq
    nch = (q0 + tq + kc - 1) // kc
    qpos = q0 + lax.broadcasted_iota(jnp.int32, (kc, tq), 1)
    krow = lax.broadcasted_iota(jnp.int32, (kc, tq), 0)

    def score_chunk(c, carry):
        ks = pl.multiple_of(c * kc, kc)
        kic = ki_ref[pl.ds(ks, kc), :][:, :IDX_DIM]
        acc = jnp.zeros((kc, tq), jnp.float32)
        for h in range(IDX_HEADS):
            d = _mm_nt(kic, qi_ref[:, h * IDX_DIM:(h + 1) * IDX_DIM])
            acc = acc + jnp.maximum(d, 0.0) * wi_ref[IDX_DIM + h:IDX_DIM + h + 1, :]
        causal = krow + ks <= qpos
        key = _sortable_key(jnp.where(causal, acc, NEG_INF))
        key_ref[c] = key
        half_ref[c] = (key >> 16).astype(jnp.int16)
        return carry

    lax.fori_loop(0, nch, score_chunk, 0)

    half_rows = 2 * SUBLANES

    def count_half_ge(cand):
        c16 = cand.astype(jnp.int16)
        c16 = jnp.concatenate([c16, c16], axis=0)
        def body(c, cnt):
            hit = jnp.where(half_ref[c].reshape(kc // half_rows, half_rows, tq) >= c16[None],
                            jnp.int16(1), jnp.int16(0))
            part = hit[0]
            for r in range(1, kc // half_rows):
                part = part + hit[r]
            return cnt + part
        cnt = lax.fori_loop(0, nch, body, jnp.zeros((half_rows, tq), jnp.int16)).astype(jnp.int32)
        cnt = cnt[0:SUBLANES, :] + cnt[SUBLANES:, :]
        for shift in (4, 2, 1):
            cnt = cnt + pltpu.roll(cnt, shift, axis=0)
        return cnt

    def bisect(base, state):
        def one_pass(st):
            lo, hi, c_lo, c_hi, done = st
            active = done == 0
            cand = jnp.where(active, lo + ((hi >> 1) - (lo >> 1)), lo)
            cnt = base + count_half_ge(cand)
            up = active & (cnt >= n_sel)
            dn = active & (cnt < n_sel)
            lo, c_lo = jnp.where(up, cand, lo), jnp.where(up, cnt, c_lo)
            hi, c_hi = jnp.where(dn, cand, hi), jnp.where(dn, cnt, c_hi)
            done = jnp.where((done != 0) | (c_lo == n_sel) | (hi - 1 <= lo), 1, 0)
            return lo, hi, c_lo, c_hi, done

        def step(st):
            inner = st[1:]
            for _ in range(SEARCH_PASSES_PER_CHECK):
                inner = one_pass(inner)
            return (st[0] + SEARCH_PASSES_PER_CHECK,) + inner

        cond = lambda st: (st[0] < SEARCH_CAP) & (jnp.min(st[5]) == 0)
        return lax.while_loop(cond, step, (jnp.int32(0),) + state)[1:]

    zeros =jnp.zeros((SUBLANES, tq), jnp.int32)
    n_valid = q0 + lax.broadcasted_iota(jnp.int32, (SUBLANES, tq), 1) + 1
    all_selected = n_valid <= n_sel

    h, _, c_ge_h, c_gt_h, _ = bisect(zeros, (jnp.full((SUBLANES, tq), -HALF_BIAS, jnp.int32),
                                             jnp.full((SUBLANES, tq), HALF_BIAS, jnp.int32),
                                             jnp.where(all_selected, n_valid, nch * kc), zeros,
                                             all_selected.astype(jnp.int32)))
    settled = all_selected | (c_ge_h == n_sel)

    def low_chunk(c, carry):
        key = key_ref[c]
        low = jnp.where((key >> 16) == h[0:1, :], (key & 0xFFFF) - HALF_BIAS, -HALF_BIAS)
        half_ref[c] = low.astype(jnp.int16)
        return carry

    lax.fori_loop(0, nch, low_chunk, 0)
    low, _, c_lo, c_hi, _ = bisect(c_gt_h, (jnp.full((SUBLANES, tq), -HALF_BIAS, jnp.int32),
                                            jnp.full((SUBLANES, tq), HALF_BIAS, jnp.int32),
                                            c_ge_h, c_gt_h, settled.astype(jnp.int32)))
    thr = jnp.where(all_selected, KEY_NEG_INF, h * (2 * HALF_BIAS) + (low + HALF_BIAS))
    thr, c_lo, c_hi = thr[0:1, :], c_lo[0:1, :], c_hi[0:1, :]
    need = jnp.where(c_lo == n_sel, float(2 ** 30), (n_sel - c_hi).astype(jnp.float32))

    def bias_chunk(with_ties, c, n_eq_before):
        ks = pl.multiple_of(c * kc, kc)
        key = key_ref[c]
        causal = krow + ks <= qpos
        if with_ties:
            eq = key == thr
            rank = _mm(tri_ref[...], jnp.where(eq, 1.0, 0.0)) + n_eq_before
            sel = ((key > thr) | (eq & (rank <= need))) & causal
            n_eq_before = rank[kc - 1:kc, :]
        else:
            sel = (key >= thr) & causal
        key_ref[c] = lax.bitcast_convert_type(jnp.where(sel, 0.0, NEG_INF), jnp.int32)
        return n_eq_before

    has_ties = jnp.max(c_lo) > n_sel

    @pl.when(has_ties)
    def _():
        lax.fori_loop(0, nch, functools.partial(bias_chunk, True), jnp.zeros((1, tq), jnp.float32))

    @pl.when(jnp.logical_not(has_ties))
    def _():
        lax.fori_loop(0, nch, functools.partial(bias_chunk, False), jnp.zeros((1, tq), jnp.float32))

    acc_ref[...] = jnp.zeros(acc_ref.shape, jnp.float32)
    group = N_HEADS // N_KV_HEADS
    ones_rows = jnp.ones((ACC_ROWS - HEAD_DIM, kc), MXU_DTYPE)

    def attend_chunk(c, ms):
        ks = pl.multiple_of(c * kc, kc)
        bias = lax.bitcast_convert_type(key_ref[c], jnp.float32)
        kch = k_ref[pl.ds(ks, kc), :]
        vch = v_ref[c]
        vext = [jnp.concatenate([vch[g * HEAD_DIM:(g + 1) * HEAD_DIM, :], ones_rows], axis=0)
                for g in range(N_KV_HEADS)]
        slot0 = jnp.minimum(c, 0)
        mx = []
        for h in range(N_HEADS):
            g = h // group
            s = _mm_nt(kch[:, g * HEAD_DIM:(g + 1) * HEAD_DIM], q_ref[:, h * HEAD_DIM:(h + 1) * HEAD_DIM]) + bias
            s_ref[slot0 + h] = s
            mx.append(jnp.max(s, axis=0, keepdims=True))
        new_ms = []
        for h in range(N_HEADS):
            m_new = jnp.maximum(ms[h], mx[h])
            p = jnp.exp2(s_ref[slot0 + h] - m_new)
            acc_ref[h] = jnp.exp2(ms[h] - m_new) * acc_ref[h] + _mm(vext[h // group], p)
            new_ms.append(m_new)
        return tuple(new_ms)

    lax.fori_loop(0, nch, attend_chunk, tuple(jnp.full((1, tq), -1e30, jnp.float32) for _ in range(N_HEADS)))

    out_t = jnp.concatenate([acc_ref[h, 0:HEAD_DIM, :] / acc_ref[h, HEAD_DIM:HEAD_DIM + 1, :]
                             for h in range(N_HEADS)], axis=0)
    o_ref[...] = out_t.T.astype(o_ref.dtype)


def _attention(qi, wi_t, q, ki, k, v_t, batch, seq, tq, kc):
    n = q.shape[0]
    n_sel = min(MAX_TOPK, seq // 4)
    nq = seq // tq
    nck = seq // kc
    assert seq % tq == 0 and seq % kc == 0 and kc >= n_sel and v_t.shape == (n // kc, LANES, kc)
    tri = jnp.asarray((np.arange(kc)[None, :] <= np.arange(kc)[:, None]).astype(np.float32), MXU_DTYPE)
    qblk = lambda w: pl.BlockSpec((tq, w), lambda b, j: (b * nq + j, 0))
    seqblk = pl.BlockSpec((seq, LANES), lambda b, j: (b, 0))
    return pl.pallas_call(
        functools.partial(_attn_kernel, tq=tq, kc=kc, n_sel=n_sel),
        grid=(batch, nq),
        in_specs=[qblk(512), pl.BlockSpec((LANES, tq), lambda b, j: (0, b * nq + j)), qblk(512),
                  seqblk, seqblk, pl.BlockSpec((nck, LANES, kc), lambda b, j: (b, 0, 0)),
                  pl.BlockSpec((kc, kc), lambda b, j: (0, 0))],
        out_specs=qblk(512),
        out_shape=jax.ShapeDtypeStruct((n, N_HEADS * HEAD_DIM), MXU_DTYPE),
        scratch_shapes=[pltpu.VMEM((nck, kc, tq), jnp.int32),
                        pltpu.VMEM((nck, kc, tq), jnp.int16),
                        pltpu.VMEM((N_HEADS, kc, tq), jnp.float32),
                        pltpu.VMEM((N_HEADS, ACC_ROWS, tq), jnp.float32)],
        compiler_params=_cparams(2),
        name="dsa_attention",
    )(qi, wi_t, q, ki, k, v_t, tri)


def _merge_kernel(x_ref, ya_ref, u_ref, uh_ref, wg_ref, wua_ref, pw_ref, ps_ref, wup_ref, wo_ref,
                  g_ref, b_ref, rw_ref, rb_ref,
                  x1_ref, idx_ref, gate_ref, ext_ref, *, tm, seq):
    i = pl.program_id(0)
    x = x_ref[...]
    xb = x.astype(MXU_DTYPE)

    first = (i % (seq // tm)) == 0
    ext_ref[0:POOL_HALO, :] = jnp.where(first, 0.0, uh_ref[...])
    ext_ref[POOL_HALO:, :] = u_ref[...]
    pos1 = ((i % (seq // tm)) * tm + 1 + lax.broadcasted_iota(jnp.int32, (tm, 1), 0)).astype(jnp.float32)
    parts = []
    for g, win in enumerate(POOL_WINDOWS):
        e = ext_ref[:, g * POOL_GROUP:(g + 1) * POOL_GROUP]
        w = 1
        while w < win:
            e = e + pltpu.roll(e, w, axis=0)
            w *= 2
        tok = e[POOL_HALO:, :]
        ug = u_ref[:, g * POOL_GROUP:(g + 1) * POOL_GROUP]
        d = tok / jnp.minimum(pos1, float(win)) - ug
        parts.append(_mm(d, pw_ref[g]))
    y_pool = jnp.concatenate(parts, axis=1) * ps_ref[...]

    gates = jnp.dot(xb, wg_ref[...], preferred_element_type=jnp.float32)
    merged = (jax.nn.sigmoid(gates[:, :D_MODEL]) * _mm(ya_ref[...], wua_ref[...])
              + jax.nn.sigmoid(gates[:, D_MODEL:]) * _mm(y_pool, wup_ref[...]))
    mix = _mm(merged, wo_ref[...])
    x1 = _layer_norm(DN_ALPHA * x + mix, g_ref[...], b_ref[...])
    _store_rows(x1_ref, x1)

    logits = _mm(x1, rw_ref[...]) + rb_ref[...]
    lane = lax.broadcasted_iota(jnp.int32, logits.shape, 1)
    work = jnp.where(lane < N_EXPERTS, logits, NEG_INF)
    vals, idxs = [], []
    for _ in range(TOP_K):
        m = jnp.max(work, axis=1, keepdims=True)
        ix = jnp.min(jnp.where(work == m, lane, LANES), axis=1, keepdims=True)
        vals.append(m)
        idxs.append(ix)
        work = jnp.where(lane == ix, NEG_INF, work)
    es = [jnp.exp(vv - vals[0]) for vv in vals]
    den = es[0] + es[1] + es[2] + es[3]
    idx_out = jnp.zeros(logits.shape, jnp.int32)
    gate_out = jnp.zeros(logits.shape, jnp.float32)
    for kk in range(TOP_K):
        idx_out = jnp.where(lane == kk, idxs[kk], idx_out)
        gate_out = jnp.where(lane == kk, es[kk] / den, gate_out)
    idx_ref[...] = idx_out
    gate_ref[...] = gate_out


def _merge(x, y_attn, u, wg, wua, pw, ps, wup, wo, g, b, rw, rb, seq, tm=512):
    n = x.shape[0]
    hb = tm // POOL_HALO
    full = lambda shape: pl.BlockSpec(shape, lambda i: (0,) * len(shape))
    row = lambda w: pl.BlockSpec((tm, w), lambda i: (i, 0))
    return pl.pallas_call(
        functools.partial(_merge_kernel, tm=tm, seq=seq),
        grid=(n // tm,),
        in_specs=[row(D_MODEL), row(512), row(POOL_CH),
                  pl.BlockSpec((POOL_HALO, POOL_CH), lambda i: (jnp.maximum(i * hb - 1, 0), 0)),
                  full((D_MODEL, 2 * D_MODEL)), full((512, D_MODEL)),
                  full((4, POOL_GROUP, POOL_GROUP)), full((1, POOL_CH)), full((POOL_CH, D_MODEL)),
                  full((D_MODEL, D_MODEL)), full((1, D_MODEL)), full((1, D_MODEL)),
                  full((D_MODEL, LANES)), full((1, LANES))],
        out_specs=[pl.BlockSpec((tm * ROW_TILES, LANES), lambda i: (i, 0)), row(LANES), row(LANES)],
        out_shape=[jax.ShapeDtypeStruct((n * ROW_TILES, LANES), jnp.float32),
                   jax.ShapeDtypeStruct((n, LANES), jnp.int32),
                   jax.ShapeDtypeStruct((n, LANES), jnp.float32)],
        scratch_shapes=[pltpu.VMEM((tm + POOL_HALO, POOL_CH), jnp.float32)],
        compiler_params=_cparams(1),
        name="merge_ln1_router",
    )(x, y_attn, u, u, wg, wua, pw, ps, wup, wo, g, b, rw, rb)


def _gather_rows(idx_ref, n_rows, table_hbm, buf_ref, sem, both_queues=False):
    def issue(i, carry):
        for j in range(GATHER_UNROLL):
            r = i * GATHER_UNROLL + j
            t = idx_ref[0, 0, r]
            pltpu.make_async_copy(table_hbm.at[pl.ds(pl.multiple_of(t * ROW_TILES, ROW_TILES), ROW_TILES), :],
                                  buf_ref.at[pl.ds(pl.multiple_of(r * ROW_TILES, ROW_TILES), ROW_TILES), :],
                                  sem).start(priority=j % 2 if both_queues else 0)
        return carry
    assert n_rows % GATHER_UNROLL == 0
    lax.fori_loop(0, n_rows // GATHER_UNROLL, issue, 0)


def _wait_rows(table_hbm, buf_ref, sem):
    pltpu.make_async_copy(table_hbm.at[pl.ds(0, buf_ref.shape[0]), :], buf_ref, sem).wait()


def _rows_as_matrix(buf_ref, r0, n_rows):
    return jnp.concatenate(
        [buf_ref[pl.ds(r0 * ROW_TILES + c, n_rows, stride=ROW_TILES), :] for c in range(ROW_TILES)], axis=1)


def _store_rows(out_ref, y):
    for c in range(ROW_TILES):
        out_ref[pl.ds(c, y.shape[0], stride=ROW_TILES), :] = y[:, c * LANES:(c + 1) * LANES]


def _expert_kernel(be_ref, nu_ref, first_ref, nxt_ref, wslot_ref, tok_ref, tok_next_ref, x_hbm,
                   wgu_hbm, bgu_ref, wd_hbm, bd_ref, y_ref,
                   xbuf0, xbuf1, wgu_f0, wgu_f1, wd_f0, wd_f1, wgu_bf, wd_bf, sem, wsem, *, tb):
    b = pl.program_id(0)
    n_used = nu_ref[0]
    bufs = (xbuf0, xbuf1)
    wgu_f, wd_f = (wgu_f0, wgu_f1), (wd_f0, wd_f1)

    def weight_copies(e, s):
        return (pltpu.make_async_copy(wgu_hbm.at[e], wgu_f[s], wsem.at[s, 0]),
                pltpu.make_async_copy(wd_hbm.at[e], wd_f[s], wsem.at[s, 1]))

    @pl.when(b == 0)
    def _():
        for cp in weight_copies(be_ref[0], 0):
            cp.start(priority=WEIGHT_DMA_PRIORITY)
        _gather_rows(tok_ref, tb, x_hbm, xbuf0, sem.at[0])

    for slot in range(2):
        @pl.when((b + 1 < n_used) & ((b + 1) % 2 == slot))
        def _():
            _gather_rows(tok_next_ref, tb, x_hbm, bufs[slot], sem.at[slot])

    @pl.when(b < n_used)
    def _():
        for s in range(2):
            @pl.when((first_ref[b] == 1) & (wslot_ref[b] == s))
            def _():
                for cp in weight_copies(be_ref[b], s):
                    cp.wait()
                wgu_bf[...] = wgu_f[s][...].astype(MXU_DTYPE)
                wd_bf[...] = wd_f[s][...].astype(MXU_DTYPE)

                @pl.when(nxt_ref[b] >= 0)
                def _():
                    for cp in weight_copies(nxt_ref[b], 1 - s):
                        cp.start(priority=WEIGHT_DMA_PRIORITY)

        def compute(xbuf, slot):
            _wait_rows(x_hbm, xbuf, sem.at[slot])
            xr = _rows_as_matrix(xbuf, 0, tb)
            gu = _mm(xr, wgu_bf[...]) + bgu_ref[...]
            gt = jnp.minimum(gu[:, :D_FF], SWIGLU_LIMIT)
            up = jnp.clip(gu[:, D_FF:], -SWIGLU_LIMIT, SWIGLU_LIMIT)
            act = gt * jax.nn.sigmoid(SWIGLU_ALPHA * gt) * (up + 1.0)
            _store_rows(y_ref, _mm(act, wd_bf[...]) + bd_ref[...])

        for slot in range(2):
            pl.when(b % 2 == slot)(functools.partial(compute, bufs[slot], slot))

    @pl.when(b >= n_used)
    def _():
        y_ref[...] = jnp.zeros(y_ref.shape, y_ref.dtype)


def _experts(blk_exp, n_used, row_tok, x1_rows, w_gu, b_gu, w_down, b_down, tb):
    n_blocks = blk_exp.shape[0]
    blk = jnp.arange(n_blocks, dtype=jnp.int32)
    first = jnp.concatenate([jnp.ones((1,), jnp.int32), (blk_exp[1:] != blk_exp[:-1]).astype(jnp.int32)])
    wslot = (jnp.cumsum(first) - 1) % 2
    at_or_after = lax.cummin(jnp.where(first == 1, blk, n_blocks), reverse=True)
    next_first = jnp.concatenate([at_or_after[1:], jnp.full((1,), n_blocks, jnp.int32)])
    nxt = jnp.where(next_first < n_used[0], blk_exp[jnp.minimum(next_first, n_blocks - 1)], -1)
    bspec = lambda c: pl.BlockSpec((None, 1, c), lambda b, be, *_: (be[b], 0, 0))
    tok = row_tok.reshape(n_blocks, 1, tb)
    return pl.pallas_call(
        functools.partial(_expert_kernel, tb=tb),
        grid_spec=pltpu.PrefetchScalarGridSpec(
            num_scalar_prefetch=5,
            grid=(n_blocks,),
            in_specs=[pl.BlockSpec((1, 1, tb), lambda b, *_: (b, 0, 0), memory_space=pltpu.SMEM),
                      pl.BlockSpec((1, 1, tb), lambda b, *_: (jnp.minimum(b + 1, n_blocks - 1), 0, 0),
                                   memory_space=pltpu.SMEM),
                      pl.BlockSpec(memory_space=pl.ANY),
                      pl.BlockSpec(memory_space=pl.ANY), bspec(2 * D_FF),
                      pl.BlockSpec(memory_space=pl.ANY), bspec(D_MODEL)],
            out_specs=pl.BlockSpec((tb * ROW_TILES, LANES), lambda b, *_: (b, 0)),
            scratch_shapes=[pltpu.VMEM((tb * ROW_TILES, LANES), jnp.float32),
                            pltpu.VMEM((tb * ROW_TILES, LANES), jnp.float32),
                            pltpu.VMEM((D_MODEL, 2 * D_FF), jnp.float32),
                            pltpu.VMEM((D_MODEL, 2 * D_FF), jnp.float32),
                            pltpu.VMEM((D_FF, D_MODEL), jnp.float32),
                            pltpu.VMEM((D_FF, D_MODEL), jnp.float32),
                            pltpu.VMEM((D_MODEL, 2 * D_FF), MXU_DTYPE),
                            pltpu.VMEM((D_FF, D_MODEL), MXU_DTYPE),
                            pltpu.SemaphoreType.DMA((2,)),
                            pltpu.SemaphoreType.DMA((2, 2))]),
        out_shape=jax.ShapeDtypeStruct((n_blocks * tb * ROW_TILES, LANES), jnp.float32),
        compiler_params=_cparams(1),
        name="moe_experts",
    )(blk_exp, n_used, first, nxt.astype(jnp.int32), wslot.astype(jnp.int32), tok, tok, x1_rows,
      w_gu, b_gu, w_down, b_down)


def _combine_kernel(dest_ref, dest_next_ref, y_hbm, gate_ref, x1_ref, p_ref, wpg_ref, wpp_ref, g_ref, b_ref,
                    o_ref, ybuf0, ybuf1, sem, *, tm):
    i = pl.program_id(0)
    bufs = (ybuf0, ybuf1)

    @pl.when(i == 0)
    def _():
        _gather_rows(dest_ref, TOP_K * tm, y_hbm, ybuf0, sem.at[0], both_queues=True)

    for slot in range(2):
        @pl.when((i + 1 < pl.num_programs(0)) & ((i + 1) % 2 == slot))
        def _():
            _gather_rows(dest_next_ref, TOP_K * tm, y_hbm, bufs[slot], sem.at[slot], both_queues=True)

    x1 = _rows_as_matrix(x1_ref, 0, tm)
    ple = jax.nn.sigmoid(_mm(x1, wpg_ref[...])) * _mm(p_ref[...], wpp_ref[...])
    h = DN_ALPHA * x1 + ple
    gate = gate_ref[...]

    def finish(ybuf, slot):
        _wait_rows(y_hbm, ybuf, sem.at[slot])
        ffn = h
        for kk in range(TOP_K):
            ffn = ffn + _rows_as_matrix(ybuf, kk * tm, tm) * gate[:, kk:kk + 1]
        o_ref[...] = _layer_norm(ffn, g_ref[...], b_ref[...])

    for slot in range(2):
        pl.when(i % 2 == slot)(functools.partial(finish, bufs[slot], slot))


def _combine(dest, yr, gates, x1_rows, p_all, layer, wpg, wpp, g, b, tm=256):
    n = gates.shape[0]
    nt = n // tm
    full = lambda shape: pl.BlockSpec(shape, lambda i: (0,) * len(shape))
    row = lambda w: pl.BlockSpec((tm, w), lambda i: (i, 0))
    dest_t = dest.reshape(nt, tm, TOP_K).transpose(0, 2, 1).reshape(nt, 1, TOP_K * tm)
    return pl.pallas_call(
        functools.partial(_combine_kernel, tm=tm),
        grid=(nt,),
        in_specs=[pl.BlockSpec((1, 1, TOP_K * tm), lambda i: (i, 0, 0), memory_space=pltpu.SMEM),
                  pl.BlockSpec((1, 1, TOP_K * tm), lambda i: (jnp.minimum(i + 1, nt - 1), 0, 0),
                               memory_space=pltpu.SMEM),
                  pl.BlockSpec(memory_space=pl.ANY),
                  row(LANES), pl.BlockSpec((tm * ROW_TILES, LANES), lambda i: (i, 0)),
                  pl.BlockSpec((tm, PLE_DIM), lambda i: (layer * nt + i, 0)),
                  full((D_MODEL, D_MODEL)), full((PLE_DIM, D_MODEL)),
                  full((1, D_MODEL)), full((1, D_MODEL))],
        out_specs=row(D_MODEL),
        out_shape=jax.ShapeDtypeStruct((n, D_MODEL), jnp.float32),
        scratch_shapes=[pltpu.VMEM((TOP_K * tm * ROW_TILES, LANES), jnp.float32),
                        pltpu.VMEM((TOP_K * tm * ROW_TILES, LANES), jnp.float32),
                        pltpu.SemaphoreType.DMA((2,))],
        compiler_params=_cparams(1),
        name="combine_ple_ln2",
    )(dest_t, dest_t, yr, gates, x1_rows, p_all, wpg, wpp, g, b)


def _route(top_idx, n_tokens, tb):
    a = n_tokens * TOP_K
    n_blocks = a // tb + N_EXPERTS
    onehot = (top_idx[:, :, None] == jnp.arange(N_EXPERTS, dtype=jnp.int32)[None, None, :]).astype(jnp.int32)
    member = onehot.sum(axis=1)
    rank = jnp.cumsum(member, axis=0) - member
    counts = member.sum(axis=0)
    padded = (counts + tb - 1) // tb * tb
    pend = jnp.cumsum(padded)
    pstart = pend - padded
    dest = jnp.take_along_axis(rank + pstart[None, :], top_idx, axis=1).astype(jnp.int32)
    tok = jnp.broadcast_to(jnp.arange(n_tokens, dtype=jnp.int32)[:, None], (n_tokens, TOP_K))
    row_tok = jnp.zeros((n_blocks * tb,), jnp.int32).at[dest.reshape(-1)].set(
        tok.reshape(-1), unique_indices=True, mode="promise_in_bounds")
    blk_start = jnp.arange(n_blocks, dtype=jnp.int32) * tb
    blk_exp = jnp.minimum((blk_start[:, None] >= pend[None, :]).astype(jnp.int32).sum(axis=1), N_EXPERTS - 1)
    n_used = (pend[-1] // tb).astype(jnp.int32).reshape(1)
    return dest, row_tok, blk_exp, n_used


def kernel(x, p, ln0_g, ln0_b, w_in, pool_w, pool_scale, w_up_attn, w_up_pool, w_out, ln1_g, ln1_b,
           router_w, router_b, exp_w_gu, exp_b_gu, exp_w_down, exp_b_down, ple_w_gate, ple_w_proj,
           ln2_g, ln2_b):
    batch, seq, d = x.shape
    assert d == D_MODEL
    n = batch * seq
    tb = 256
    kc = 512
    tables = _rope_tables(seq)
    bf = lambda a: a.astype(MXU_DTYPE)
    vec = lambda a: a.reshape(1, -1)
    n_le = DEPTH * N_EXPERTS
    w_gu_all = exp_w_gu.reshape(n_le, D_MODEL, 2 * D_FF)
    b_gu_all = exp_b_gu.reshape(n_le, 1, 2 * D_FF)
    w_down_all = exp_w_down.reshape(n_le, D_FF, D_MODEL)
    b_down_all = exp_b_down.reshape(n_le, 1, D_MODEL)
    p_all = p.reshape(DEPTH * n, PLE_DIM)

    h = x.reshape(n, d)
    for i in range(DEPTH):
        w = w_in[i]
        w1 = bf(jnp.concatenate([w[:, :1352], jnp.zeros((d, OFF_U - 1352), w.dtype), w[:, 1352:1864]], axis=1))
        wg = bf(w[:, 1864:])
        if i == 0:
            q, k, v_t, qi, ki, wi_t, u, h = _inproj(h, w1, tables, seq, kc, entry_ln=(ln0_g, ln0_b))
        else:
            q, k, v_t, qi, ki, wi_t, u = _inproj(h, w1, tables, seq, kc)
        y_attn = _attention(qi, wi_t, q, ki, k, v_t, batch, seq, 512, kc)
        rw = bf(jnp.pad(router_w[i], ((0, 0), (0, LANES - N_EXPERTS))))
        rb = jnp.pad(router_b[i], (0, LANES - N_EXPERTS)).reshape(1, -1)
        x1_rows, idx_l, gate_l = _merge(h, y_attn, u, wg, bf(w_up_attn[i]), bf(pool_w[i]), vec(pool_scale[i]),
                                        bf(w_up_pool[i]), bf(w_out[i]), vec(ln1_g[i]), vec(ln1_b[i]), rw, rb, seq)
        dest, row_tok, blk_exp, n_used = _route(idx_l[:, :TOP_K], n, tb)
        yr = _experts(blk_exp + i * N_EXPERTS, n_used, row_tok, x1_rows,
                      w_gu_all, b_gu_all, w_down_all, b_down_all, tb)
        h = _combine(dest, yr, gate_l, x1_rows, p_all, i, bf(ple_w_gate[i]), bf(ple_w_proj[i]),
                     vec(ln2_g[i]), vec(ln2_b[i]))
    return h.reshape(batch, seq, d)
```

```python
import functools

import jax
import jax.numpy as jnp
import numpy as np
from jax import lax
from jax.experimental import pallas as pl
from jax.experimental.pallas import tpu as pltpu

MXU_DTYPE = jnp.bfloat16

D_MODEL = 1024
HEAD_DIM = 64
N_HEADS = 8
N_KV_HEADS = 2
ROT_DIM = 16
ROPE_THETA = 500000.0
IDX_HEADS = 8
IDX_DIM = 64
MAX_TOPK = 256
POOL_CH = 512
POOL_WINDOWS = (2, 4, 8, 16)
POOL_GROUP = 128
POOL_HALO = 16
N_EXPERTS = 32
TOP_K = 4
D_FF = 1024
SWIGLU_LIMIT = 7.0
SWIGLU_ALPHA = 1.702
PLE_DIM = 256
LN_EPS = 1e-5
DEPTH = 2
DN_ALPHA = (2 * DEPTH) ** 0.25
LOG2_E = 1.4426950408889634

LANES = 128
SUBLANES = 8
ROW_TILES = D_MODEL // LANES
assert ROW_TILES == SUBLANES

OFF_Q, OFF_K, OFF_V, OFF_QI, OFF_KIW, OFF_U, W1_COLS = 0, 512, 640, 768, 1280, 1408, 1920

VMEM_LIMIT = 56 * 1024 * 1024
NEG_INF = float("-inf")
KEY_NEG_INF = -2139095041
HALF_BITS = 16
HALF_BIAS = 1 << (HALF_BITS - 1)
SEARCH_UNROLL = 4
ACC_ROWS = 80
GATHER_UNROLL = 8
WEIGHT_DMA_PRIORITY = 1

def _cparams(n_axes, flags=None):
    return pltpu.CompilerParams(dimension_semantics=("arbitrary",) * n_axes,
                                vmem_limit_bytes=VMEM_LIMIT, flags=flags)


def _mm(a, b):
    return jnp.dot(a.astype(MXU_DTYPE), b.astype(MXU_DTYPE), preferred_element_type=jnp.float32)


def _mm_nt(a, b):
    return lax.dot_general(a.astype(MXU_DTYPE), b.astype(MXU_DTYPE), (((1,), (1,)), ((), ())),
                           preferred_element_type=jnp.float32)


def _layer_norm(h, g, b):
    mu = jnp.mean(h, axis=-1, keepdims=True)
    c = h - mu
    var = jnp.mean(c * c, axis=-1, keepdims=True)
    return c * lax.rsqrt(var + LN_EPS) * g + b


def _rope_tables(seq):
    pos = jnp.arange(seq, dtype=jnp.float32)
    inv = ROPE_THETA ** (-jnp.arange(0, ROT_DIM, 2, dtype=jnp.float32) / ROT_DIM)
    ang = pos[:, None] * inv[None, :]
    half = ROT_DIM // 2
    sel_c = np.zeros((half, 2 * LANES), np.float32)
    sel_s1 = np.zeros((half, 2 * LANES), np.float32)
    sel_s2 = np.zeros((half, 2 * LANES), np.float32)
    ones = np.ones((1, 2 * LANES), np.float32)
    for lane in range(2 * LANES - HEAD_DIM):
        d = lane % HEAD_DIM
        if d < ROT_DIM:
            sel_c[d % half, lane], ones[0, lane] = 1.0, 0.0
            (sel_s1 if d < half else sel_s2)[d % half, lane] = -1.0 if d < half else 1.0
    spread = lambda a, sel: jnp.dot(a, jnp.asarray(sel), precision=lax.Precision.HIGHEST)
    cos, sin = jnp.cos(ang), jnp.sin(ang)
    return spread(cos, sel_c) + jnp.asarray(ones), spread(sin, sel_s1), spread(sin, sel_s2)


def _rope_tile(x, c, s1, s2):
    half = ROT_DIM // 2
    return x * c + pltpu.roll(x, LANES - half, axis=1) * s1 + pltpu.roll(x, half, axis=1) * s2


def _inproj_kernel(x_ref, w_ref, c_ref, s1_ref, s2_ref, *rest, entry_norm):
    if entry_norm:
        g_ref, b_ref, q_ref, k_ref, v_ref, qi_ref, ki_ref, wi_ref, u_ref, h_ref = rest
        h = _layer_norm(x_ref[...], g_ref[...], b_ref[...])
        h_ref[...] = h
    else:
        q_ref, k_ref, v_ref, qi_ref, ki_ref, wi_ref, u_ref = rest
        h = x_ref[...]
    xb = h.astype(MXU_DTYPE)
    c, s1, s2 = c_ref[:, :LANES], s1_ref[:, :LANES], s2_ref[:, :LANES]
    ck, s1k, s2k = c_ref[:, LANES:], s1_ref[:, LANES:], s2_ref[:, LANES:]

    def seg(off, width):
        return jnp.dot(xb, w_ref[:, off:off + width], preferred_element_type=jnp.float32)

    def roped(z, scale, out_ref):
        for j in range(z.shape[1] // LANES):
            t = _rope_tile(z[:, j * LANES:(j + 1) * LANES], c, s1, s2)
            out_ref[:, j * LANES:(j + 1) * LANES] = (t * scale).astype(out_ref.dtype)

    roped(seg(OFF_Q, 512), HEAD_DIM ** -0.5 * LOG2_E, q_ref)
    roped(seg(OFF_K, 128), 1.0, k_ref)
    v_ref[0] = seg(OFF_V, 128).T.astype(v_ref.dtype)
    roped(seg(OFF_QI, 512), IDX_DIM ** -0.5, qi_ref)
    kiw = seg(OFF_KIW, 128)
    ki_ref[...] = _rope_tile(kiw, ck, s1k, s2k).astype(ki_ref.dtype)
    wi_ref[...] = (kiw * (IDX_HEADS ** -0.5)).T
    u_ref[...] = seg(OFF_U, 512)


def _inproj(x, w1, tables, seq, tm, entry_ln=None):
    n = x.shape[0]
    nseq = seq // tm
    c, s1, s2 = tables
    tab_spec = pl.BlockSpec((tm, 2 * LANES), lambda i: (i % nseq, 0))

    def out(width, dtype):
        return pl.BlockSpec((tm, width), lambda i: (i, 0)), jax.ShapeDtypeStruct((n, width), dtype)

    v_t = (pl.BlockSpec((1, LANES, tm), lambda i: (i, 0, 0)), jax.ShapeDtypeStruct((n // tm, LANES, tm), MXU_DTYPE))
    wi_t = (pl.BlockSpec((LANES, tm), lambda i: (0, i)), jax.ShapeDtypeStruct((LANES, n), jnp.float32))
    outs = [out(512, MXU_DTYPE), out(128, MXU_DTYPE), v_t, out(512, MXU_DTYPE),
            out(128, MXU_DTYPE), wi_t, out(512, jnp.float32)]
    vec_spec = pl.BlockSpec((1, D_MODEL), lambda i: (0, 0))
    ln_specs, ln_args = [], []
    if entry_ln is not None:
        outs.append(out(D_MODEL, jnp.float32))
        ln_specs, ln_args = [vec_spec, vec_spec], [a.reshape(1, -1) for a in entry_ln]
    return pl.pallas_call(
        functools.partial(_inproj_kernel, entry_norm=entry_ln is not None),
        grid=(n // tm,),
        in_specs=[pl.BlockSpec((tm, D_MODEL), lambda i: (i, 0)),
                  pl.BlockSpec((D_MODEL, W1_COLS), lambda i: (0, 0)),
                  tab_spec, tab_spec, tab_spec] + ln_specs,
        out_specs=[o[0] for o in outs],
        out_shape=[o[1] for o in outs],
        compiler_params=_cparams(1),
        name="inproj",
    )(x, w1, c, s1, s2, *ln_args)


def _sortable_key(s):
    k = lax.bitcast_convert_type(s, jnp.int32)
    return k ^ ((k >> 31) & jnp.int32(0x7FFFFFFF))


def _attn_kernel(qi_ref, wi_ref, q_ref, ki_ref, k_ref, v_ref, tri_ref, o_ref,
                 key_ref, half_ref, s_ref, acc_ref, *, tq, kc, n_sel):
    j = pl.program_id(1)
    q0 = j * tq
    nch = (q0 + tq + kc - 1) // kc
    qpos = q0 + lax.broadcasted_iota(jnp.int32, (kc, tq), 1)
    krow = lax.broadcasted_iota(jnp.int32, (kc, tq), 0)

    def score_chunk(c, carry):
        ks = pl.multiple_of(c * kc, kc)
        kic = ki_ref[pl.ds(ks, kc), :][:, :IDX_DIM]
        acc = jnp.zeros((kc, tq), jnp.float32)
        for h in range(IDX_HEADS):
            d = _mm_nt(kic, qi_ref[:, h * IDX_DIM:(h + 1) * IDX_DIM])
            acc = acc + jnp.maximum(d, 0.0) * wi_ref[IDX_DIM + h:IDX_DIM + h + 1, :]
        causal = krow + ks <= qpos
        key = _sortable_key(jnp.where(causal, acc, NEG_INF))
        key_ref[c] = key
        half_ref[c] = (key >> 16).astype(jnp.int16)
        return carry

    lax.fori_loop(0, nch, score_chunk, 0)

    half_rows = 2 * SUBLANES

    def count_half_ge(cand):
        c16 = cand.astype(jnp.int16)
        c16 = jnp.concatenate([c16, c16], axis=0)
        def body(c, cnt):
            hit = jnp.where(half_ref[c].reshape(kc // half_rows, half_rows, tq) >= c16[None],
                            jnp.int16(1), jnp.int16(0))
            part = hit[0]
            for r in range(1, kc // half_rows):
                part = part + hit[r]
            return cnt + part
        cnt = lax.fori_loop(0, nch, body, jnp.zeros((half_rows, tq), jnp.int16)).astype(jnp.int32)
        cnt = cnt[0:SUBLANES, :] + cnt[SUBLANES:, :]
        for shift in (4, 2, 1):
            cnt = cnt + pltpu.roll(cnt, shift, axis=0)
        return cnt

    def bisect(base, frozen, c_lo, c_hi):
        def one_pass(st):
            lo, hi, c_lo, c_hi = st
            cand = lo + ((hi >> 1) - (lo >> 1))
            cnt = base + count_half_ge(cand)
            up = jnp.logical_not(frozen) & (cnt >= n_sel)
            dn = jnp.logical_not(frozen) & (cnt < n_sel)
            return (jnp.where(up, cand, lo), jnp.where(dn, cand, hi),
                    jnp.where(up, cnt, c_lo), jnp.where(dn, cnt, c_hi))

        def step(_, st):
            for _ in range(SEARCH_UNROLL):
                st = one_pass(st)
            return st

        lo0 = jnp.full((SUBLANES, tq), -HALF_BIAS, jnp.int32)
        lo, _, c_lo, c_hi = lax.fori_loop(0, HALF_BITS // SEARCH_UNROLL, step, (lo0, -lo0, c_lo, c_hi))
        return lo, c_lo, c_hi

    zeros = jnp.zeros((SUBLANES, tq), jnp.int32)
    n_valid = q0 + lax.broadcasted_iota(jnp.int32, (SUBLANES, tq), 1) + 1
    all_selected = n_valid <= n_sel

    h, c_ge_h, c_gt_h = bisect(zeros, all_selected, jnp.where(all_selected, n_valid, nch * kc), zeros)
    settled = all_selected | (c_ge_h == n_sel)

    def low_chunk(c, carry):
        key = key_ref[c]
        low = jnp.where((key >> 16) == h[0:1, :], (key & 0xFFFF) - HALF_BIAS, -HALF_BIAS)
        half_ref[c] = low.astype(jnp.int16)
        return carry

    lax.fori_loop(0, nch, low_chunk, 0)
    low, c_lo, c_hi = bisect(c_gt_h, settled, c_ge_h, c_gt_h)
    thr = jnp.where(all_selected, KEY_NEG_INF, h * (2 * HALF_BIAS) + (low + HALF_BIAS))
    thr, c_lo, c_hi = thr[0:1, :], c_lo[0:1, :], c_hi[0:1, :]
    need = jnp.where(c_lo == n_sel, float(2 ** 30), (n_sel - c_hi).astype(jnp.float32))

    def bias_chunk(with_ties, c, n_eq_before):
        ks = pl.multiple_of(c * kc, kc)
        key = key_ref[c]
        causal = krow + ks <= qpos
        if with_ties:
            eq = key == thr
            rank = _mm(tri_ref[...], jnp.where(eq, 1.0, 0.0)) + n_eq_before
            sel = ((key > thr) | (eq & (rank <= need))) & causal
            n_eq_before = rank[kc - 1:kc, :]
        else:
            sel = (key >= thr) & causal
        key_ref[c] = lax.bitcast_convert_type(jnp.where(sel, 0.0, NEG_INF), jnp.int32)
        return n_eq_before

    has_ties = jnp.max(c_lo) > n_sel

    @pl.when(has_ties)
    def _():
        lax.fori_loop(0, nch, functools.partial(bias_chunk, True), jnp.zeros((1, tq), jnp.float32))

    @pl.when(jnp.logical_not(has_ties))
    def _():
        lax.fori_loop(0, nch, functools.partial(bias_chunk, False), jnp.zeros((1, tq), jnp.float32))

    acc_ref[...] = jnp.zeros(acc_ref.shape, jnp.float32)
    group = N_HEADS // N_KV_HEADS
    ones_rows = jnp.ones((ACC_ROWS - HEAD_DIM, kc), MXU_DTYPE)

    def attend_chunk(c, ms):
        ks = pl.multiple_of(c * kc, kc)
        bias = lax.bitcast_convert_type(key_ref[c], jnp.float32)
        kch = k_ref[pl.ds(ks, kc), :]
        vch = v_ref[c]
        vext = [jnp.concatenate([vch[g * HEAD_DIM:(g + 1) * HEAD_DIM, :], ones_rows], axis=0)
                for g in range(N_KV_HEADS)]
        slot0 = jnp.minimum(c, 0)
        mx = []
        for h in range(N_HEADS):
            g = h // group
            s = _mm_nt(kch[:, g * HEAD_DIM:(g + 1) * HEAD_DIM], q_ref[:, h * HEAD_DIM:(h + 1) * HEAD_DIM]) + bias
            s_ref[slot0 + h] = s
            mx.append(jnp.max(s, axis=0, keepdims=True))
        new_ms = []
        for h in range(N_HEADS):
            m_new = jnp.maximum(ms[h], mx[h])
            p = jnp.exp2(s_ref[slot0 + h] - m_new)
            acc_ref[h] = jnp.exp2(ms[h] - m_new) * acc_ref[h] + _mm(vext[h // group], p)
            new_ms.append(m_new)
        return tuple(new_ms)

    lax.fori_loop(0, nch, attend_chunk, tuple(jnp.full((1, tq), -1e30, jnp.float32) for _ in range(N_HEADS)))

    out_t = jnp.concatenate([acc_ref[h, 0:HEAD_DIM, :] / acc_ref[h, HEAD_DIM:HEAD_DIM + 1, :]
                             for h in range(N_HEADS)], axis=0)
    o_ref[...] = out_t.T.astype(o_ref.dtype)


def _attention(qi, wi_t, q, ki, k, v_t, batch, seq, tq, kc):
    n = q.shape[0]
    n_sel = min(MAX_TOPK, seq // 4)
    nq = seq // tq
    nck = seq // kc
    assert seq % tq == 0 and seq % kc == 0 and kc >= n_sel and v_t.shape == (n // kc, LANES, kc)
    tri = jnp.asarray((np.arange(kc)[None, :] <= np.arange(kc)[:, None]).astype(np.float32), MXU_DTYPE)
    qblk = lambda w: pl.BlockSpec((tq, w), lambda b, j: (b * nq + j, 0))
    seqblk = pl.BlockSpec((seq, LANES), lambda b, j: (b, 0))
    return pl.pallas_call(
        functools.partial(_attn_kernel, tq=tq, kc=kc, n_sel=n_sel),
        grid=(batch, nq),
        in_specs=[qblk(512), pl.BlockSpec((LANES, tq), lambda b, j: (0, b * nq + j)), qblk(512),
                  seqblk, seqblk, pl.BlockSpec((nck, LANES, kc), lambda b, j: (b, 0, 0)),
                  pl.BlockSpec((kc, kc), lambda b, j: (0, 0))],
        out_specs=qblk(512),
        out_shape=jax.ShapeDtypeStruct((n, N_HEADS * HEAD_DIM), MXU_DTYPE),
        scratch_shapes=[pltpu.VMEM((nck, kc, tq), jnp.int32),
                        pltpu.VMEM((nck, kc, tq), jnp.int16),
                        pltpu.VMEM((N_HEADS, kc, tq), jnp.float32),
                        pltpu.VMEM((N_HEADS, ACC_ROWS, tq), jnp.float32)],
        compiler_params=_cparams(2),
        name="dsa_attention",
    )(qi, wi_t, q, ki, k, v_t, tri)


def _merge_kernel(x_ref, ya_ref, u_ref, uh_ref, wg_ref, wua_ref, pw_ref, ps_ref, wup_ref, wo_ref,
                  g_ref, b_ref, rw_ref, rb_ref,
                  x1_ref, idx_ref, gate_ref, ext_ref, *, tm, seq):
    i = pl.program_id(0)
    x = x_ref[...]
    xb = x.astype(MXU_DTYPE)

    first = (i % (seq // tm)) == 0
    ext_ref[0:POOL_HALO, :] = jnp.where(first, 0.0, uh_ref[...])
    ext_ref[POOL_HALO:, :] = u_ref[...]
    pos1 = ((i % (seq // tm)) * tm + 1 + lax.broadcasted_iota(jnp.int32, (tm, 1), 0)).astype(jnp.float32)
    parts = []
    for g, win in enumerate(POOL_WINDOWS):
        e = ext_ref[:, g * POOL_GROUP:(g + 1) * POOL_GROUP]
        w = 1
        while w < win:
            e = e + pltpu.roll(e, w, axis=0)
            w *= 2
        tok = e[POOL_HALO:, :]
        ug = u_ref[:, g * POOL_GROUP:(g + 1) * POOL_GROUP]
        d = tok / jnp.minimum(pos1, float(win)) - ug
        parts.append(_mm(d, pw_ref[g]))
    y_pool = jnp.concatenate(parts, axis=1) * ps_ref[...]

    gates = jnp.dot(xb, wg_ref[...], preferred_element_type=jnp.float32)
    merged = (jax.nn.sigmoid(gates[:, :D_MODEL]) * _mm(ya_ref[...], wua_ref[...])
              + jax.nn.sigmoid(gates[:, D_MODEL:]) * _mm(y_pool, wup_ref[...]))
    mix = _mm(merged, wo_ref[...])
    x1 = _layer_norm(DN_ALPHA * x + mix, g_ref[...], b_ref[...])
    _store_rows(x1_ref, x1)

    logits = _mm(x1, rw_ref[...]) + rb_ref[...]
    lane = lax.broadcasted_iota(jnp.int32, logits.shape, 1)
    work = jnp.where(lane < N_EXPERTS, logits, NEG_INF)
    vals, idxs = [], []
    for _ in range(TOP_K):
        m = jnp.max(work, axis=1, keepdims=True)
        ix = jnp.min(jnp.where(work == m, lane, LANES), axis=1, keepdims=True)
        vals.append(m)
        idxs.append(ix)
        work = jnp.where(lane == ix, NEG_INF, work)
    es = [jnp.exp(vv - vals[0]) for vv in vals]
    den = es[0] + es[1] + es[2] + es[3]
    idx_out = jnp.zeros(logits.shape, jnp.int32)
    gate_out = jnp.zeros(logits.shape, jnp.float32)
    for kk in range(TOP_K):
        idx_out = jnp.where(lane == kk, idxs[kk], idx_out)
        gate_out = jnp.where(lane == kk, es[kk] / den, gate_out)
    idx_ref[...] = idx_out
    gate_ref[...] = gate_out


def _merge(x, y_attn, u, wg, wua, pw, ps, wup, wo, g, b, rw, rb, seq, tm=512):
    n = x.shape[0]
    hb = tm // POOL_HALO
    full = lambda shape: pl.BlockSpec(shape, lambda i: (0,) * len(shape))
    row = lambda w: pl.BlockSpec((tm, w), lambda i: (i, 0))
    return pl.pallas_call(
        functools.partial(_merge_kernel, tm=tm, seq=seq),
        grid=(n // tm,),
        in_specs=[row(D_MODEL), row(512), row(POOL_CH),
                  pl.BlockSpec((POOL_HALO, POOL_CH), lambda i: (jnp.maximum(i * hb - 1, 0), 0)),
                  full((D_MODEL, 2 * D_MODEL)), full((512, D_MODEL)),
                  full((4, POOL_GROUP, POOL_GROUP)), full((1, POOL_CH)), full((POOL_CH, D_MODEL)),
                  full((D_MODEL, D_MODEL)), full((1, D_MODEL)), full((1, D_MODEL)),
                  full((D_MODEL, LANES)), full((1, LANES))],
        out_specs=[pl.BlockSpec((tm * ROW_TILES, LANES), lambda i: (i, 0)), row(LANES), row(LANES)],
        out_shape=[jax.ShapeDtypeStruct((n * ROW_TILES, LANES), jnp.float32),
                   jax.ShapeDtypeStruct((n, LANES), jnp.int32),
                   jax.ShapeDtypeStruct((n, LANES), jnp.float32)],
        scratch_shapes=[pltpu.VMEM((tm + POOL_HALO, POOL_CH), jnp.float32)],
        compiler_params=_cparams(1),
        name="merge_ln1_router",
    )(x, y_attn, u, u, wg, wua, pw, ps, wup, wo, g, b, rw, rb)


def _gather_rows(idx_ref, n_rows, table_hbm, buf_ref, sem, both_queues=False):
    def issue(i, carry):
        for j in range(GATHER_UNROLL):
            r = i * GATHER_UNROLL + j
            t = idx_ref[0, 0, r]
            pltpu.make_async_copy(table_hbm.at[pl.ds(pl.multiple_of(t * ROW_TILES, ROW_TILES), ROW_TILES), :],
                                  buf_ref.at[pl.ds(pl.multiple_of(r * ROW_TILES, ROW_TILES), ROW_TILES), :],
                                  sem).start(priority=j % 2 if both_queues else 0)
        return carry
    assert n_rows % GATHER_UNROLL == 0
    lax.fori_loop(0, n_rows // GATHER_UNROLL, issue, 0)


def _wait_rows(table_hbm, buf_ref, sem):
    pltpu.make_async_copy(table_hbm.at[pl.ds(0, buf_ref.shape[0]), :], buf_ref, sem).wait()


def _rows_as_matrix(buf_ref, r0, n_rows):
    return jnp.concatenate(
        [buf_ref[pl.ds(r0 * ROW_TILES + c, n_rows, stride=ROW_TILES), :] for c in range(ROW_TILES)], axis=1)


def _store_rows(out_ref, y):
    for c in range(ROW_TILES):
        out_ref[pl.ds(c, y.shape[0], stride=ROW_TILES), :] = y[:, c * LANES:(c + 1) * LANES]


def _expert_kernel(be_ref, nu_ref, first_ref, nxt_ref, wslot_ref, tok_ref, tok_next_ref, x_hbm,
                   wgu_hbm, bgu_ref, wd_hbm, bd_ref, y_ref,
                   xbuf0, xbuf1, wgu_f0, wgu_f1, wd_f0, wd_f1, wgu_bf, wd_bf, sem, wsem, *, tb):
    b = pl.program_id(0)
    n_used = nu_ref[0]
    bufs = (xbuf0, xbuf1)
    wgu_f, wd_f = (wgu_f0, wgu_f1), (wd_f0, wd_f1)

    def weight_copies(e, s):
        return (pltpu.make_async_copy(wgu_hbm.at[e], wgu_f[s], wsem.at[s, 0]),
                pltpu.make_async_copy(wd_hbm.at[e], wd_f[s], wsem.at[s, 1]))

    @pl.when(b == 0)
    def _():
        for cp in weight_copies(be_ref[0], 0):
            cp.start(priority=WEIGHT_DMA_PRIORITY)
        _gather_rows(tok_ref, tb, x_hbm, xbuf0, sem.at[0])

    for slot in range(2):
        @pl.when((b + 1 < n_used) & ((b + 1) % 2 == slot))
        def _():
            _gather_rows(tok_next_ref, tb, x_hbm, bufs[slot], sem.at[slot])

    @pl.when(b < n_used)
    def _():
        for s in range(2):
            @pl.when((first_ref[b] == 1) & (wslot_ref[b] == s))
            def _():
                for cp in weight_copies(be_ref[b], s):
                    cp.wait()
                wgu_bf[...] = wgu_f[s][...].astype(MXU_DTYPE)
                wd_bf[...] = wd_f[s][...].astype(MXU_DTYPE)

                @pl.when(nxt_ref[b] >= 0)
                def _():
                    for cp in weight_copies(nxt_ref[b], 1 - s):
                        cp.start(priority=WEIGHT_DMA_PRIORITY)

        def compute(xbuf, slot):
            _wait_rows(x_hbm, xbuf, sem.at[slot])
            xr = _rows_as_matrix(xbuf, 0, tb)
            gu = _mm(xr, wgu_bf[...]) + bgu_ref[...]
            gt = jnp.minimum(gu[:, :D_FF], SWIGLU_LIMIT)
            up = jnp.clip(gu[:, D_FF:], -SWIGLU_LIMIT, SWIGLU_LIMIT)
            act = gt * jax.nn.sigmoid(SWIGLU_ALPHA * gt) * (up + 1.0)
            _store_rows(y_ref, _mm(act, wd_bf[...]) + bd_ref[...])

        for slot in range(2):
            pl.when(b % 2 == slot)(functools.partial(compute, bufs[slot], slot))

    @pl.when(b >= n_used)
    def _():
        y_ref[...] = jnp.zeros(y_ref.shape, y_ref.dtype)


def _experts(blk_exp, n_used, row_tok, x1_rows, w_gu, b_gu, w_down, b_down, tb):
    n_blocks = blk_exp.shape[0]
    blk = jnp.arange(n_blocks, dtype=jnp.int32)
    first = jnp.concatenate([jnp.ones((1,), jnp.int32), (blk_exp[1:] != blk_exp[:-1]).astype(jnp.int32)])
    wslot = (jnp.cumsum(first) - 1) % 2
    at_or_after = lax.cummin(jnp.where(first == 1, blk, n_blocks), reverse=True)
    next_first = jnp.concatenate([at_or_after[1:], jnp.full((1,), n_blocks, jnp.int32)])
    nxt = jnp.where(next_first < n_used[0], blk_exp[jnp.minimum(next_first, n_blocks - 1)], -1)
    bspec = lambda c: pl.BlockSpec((None, 1, c), lambda b, be, *_: (be[b], 0, 0))
    tok = row_tok.reshape(n_blocks, 1, tb)
    return pl.pallas_call(
        functools.partial(_expert_kernel, tb=tb),
        grid_spec=pltpu.PrefetchScalarGridSpec(
            num_scalar_prefetch=5,
            grid=(n_blocks,),
            in_specs=[pl.BlockSpec((1, 1, tb), lambda b, *_: (b, 0, 0), memory_space=pltpu.SMEM),
                      pl.BlockSpec((1, 1, tb), lambda b, *_: (jnp.minimum(b + 1, n_blocks - 1), 0, 0),
                                   memory_space=pltpu.SMEM),
                      pl.BlockSpec(memory_space=pl.ANY),
                      pl.BlockSpec(memory_space=pl.ANY), bspec(2 * D_FF),
                      pl.BlockSpec(memory_space=pl.ANY), bspec(D_MODEL)],
            out_specs=pl.BlockSpec((tb * ROW_TILES, LANES), lambda b, *_: (b, 0)),
            scratch_shapes=[pltpu.VMEM((tb * ROW_TILES, LANES), jnp.float32),
                            pltpu.VMEM((tb * ROW_TILES, LANES), jnp.float32),
                            pltpu.VMEM((D_MODEL, 2 * D_FF), jnp.float32),
                            pltpu.VMEM((D_MODEL, 2 * D_FF), jnp.float32),
                            pltpu.VMEM((D_FF, D_MODEL), jnp.float32),
                            pltpu.VMEM((D_FF, D_MODEL), jnp.float32),
                            pltpu.VMEM((D_MODEL, 2 * D_FF), MXU_DTYPE),
                            pltpu.VMEM((D_FF, D_MODEL), MXU_DTYPE),
                            pltpu.SemaphoreType.DMA((2,)),
                            pltpu.SemaphoreType.DMA((2, 2))]),
        out_shape=jax.ShapeDtypeStruct((n_blocks * tb * ROW_TILES, LANES), jnp.float32),
        compiler_params=_cparams(1),
        name="moe_experts",
    )(blk_exp, n_used, first, nxt.astype(jnp.int32), wslot.astype(jnp.int32), tok, tok, x1_rows,
      w_gu, b_gu, w_down, b_down)


def _combine_kernel(dest_ref, dest_next_ref, y_hbm, gate_ref, x1_ref, p_ref, wpg_ref, wpp_ref, g_ref, b_ref,
                    o_ref, ybuf0, ybuf1, sem, *, tm):
    i = pl.program_id(0)
    bufs = (ybuf0, ybuf1)

    @pl.when(i == 0)
    def _():
        _gather_rows(dest_ref, TOP_K * tm, y_hbm, ybuf0, sem.at[0], both_queues=True)

    for slot in range(2):
        @pl.when((i + 1 < pl.num_programs(0)) & ((i + 1) % 2 == slot))
        def _():
            _gather_rows(dest_next_ref, TOP_K * tm, y_hbm, bufs[slot], sem.at[slot], both_queues=True)

    x1 = _rows_as_matrix(x1_ref, 0, tm)
    ple = jax.nn.sigmoid(_mm(x1, wpg_ref[...])) * _mm(p_ref[...], wpp_ref[...])
    h = DN_ALPHA * x1 + ple
    gate = gate_ref[...]

    def finish(ybuf, slot):
        _wait_rows(y_hbm, ybuf, sem.at[slot])
        ffn = h
        for kk in range(TOP_K):
            ffn = ffn + _rows_as_matrix(ybuf, kk * tm, tm) * gate[:, kk:kk + 1]
        o_ref[...] = _layer_norm(ffn, g_ref[...], b_ref[...])

    for slot in range(2):
        pl.when(i % 2 == slot)(functools.partial(finish, bufs[slot], slot))


def _combine(dest, yr, gates, x1_rows, p_all, layer, wpg, wpp, g, b, tm=256):
    n = gates.shape[0]
    nt = n // tm
    full = lambda shape: pl.BlockSpec(shape, lambda i: (0,) * len(shape))
    row = lambda w: pl.BlockSpec((tm, w), lambda i: (i, 0))
    dest_t = dest.reshape(nt, tm, TOP_K).transpose(0, 2, 1).reshape(nt, 1, TOP_K * tm)
    return pl.pallas_call(
        functools.partial(_combine_kernel, tm=tm),
        grid=(nt,),
        in_specs=[pl.BlockSpec((1, 1, TOP_K * tm), lambda i: (i, 0, 0), memory_space=pltpu.SMEM),
                  pl.BlockSpec((1, 1, TOP_K * tm), lambda i: (jnp.minimum(i + 1, nt - 1), 0, 0),
                               memory_space=pltpu.SMEM),
                  pl.BlockSpec(memory_space=pl.ANY),
                  row(LANES), pl.BlockSpec((tm * ROW_TILES, LANES), lambda i: (i, 0)),
                  pl.BlockSpec((tm, PLE_DIM), lambda i: (layer * nt + i, 0)),
                  full((D_MODEL, D_MODEL)), full((PLE_DIM, D_MODEL)),
                  full((1, D_MODEL)), full((1, D_MODEL))],
        out_specs=row(D_MODEL),
        out_shape=jax.ShapeDtypeStruct((n, D_MODEL), jnp.float32),
        scratch_shapes=[pltpu.VMEM((TOP_K * tm * ROW_TILES, LANES), jnp.float32),
                        pltpu.VMEM((TOP_K * tm * ROW_TILES, LANES), jnp.float32),
                        pltpu.SemaphoreType.DMA((2,))],
        compiler_params=_cparams(1),
        name="combine_ple_ln2",
    )(dest_t, dest_t, yr, gates, x1_rows, p_all, wpg, wpp, g, b)


def _route(top_idx, n_tokens, tb):
    a = n_tokens * TOP_K
    n_blocks = a // tb + N_EXPERTS
    onehot = (top_idx[:, :, None] == jnp.arange(N_EXPERTS, dtype=jnp.int32)[None, None, :]).astype(jnp.int32)
    member = onehot.sum(axis=1)
    rank = jnp.cumsum(member, axis=0) - member
    counts = member.sum(axis=0)
    padded = (counts + tb - 1) // tb * tb
    pend = jnp.cumsum(padded)
    pstart = pend - padded
    dest = jnp.take_along_axis(rank + pstart[None, :], top_idx, axis=1).astype(jnp.int32)
    tok = jnp.broadcast_to(jnp.arange(n_tokens, dtype=jnp.int32)[:, None], (n_tokens, TOP_K))
    row_tok = jnp.zeros((n_blocks * tb,), jnp.int32).at[dest.reshape(-1)].set(
        tok.reshape(-1), unique_indices=True, mode="promise_in_bounds")
    blk_start = jnp.arange(n_blocks, dtype=jnp.int32) * tb
    blk_exp = jnp.minimum((blk_start[:, None] >= pend[None, :]).astype(jnp.int32).sum(axis=1), N_EXPERTS - 1)
    n_used = (pend[-1] // tb).astype(jnp.int32).reshape(1)
    return dest, row_tok, blk_exp, n_used


def kernel(x, p, ln0_g, ln0_b, w_in, pool_w, pool_scale, w_up_attn, w_up_pool, w_out, ln1_g, ln1_b,
           router_w, router_b, exp_w_gu, exp_b_gu, exp_w_down, exp_b_down, ple_w_gate, ple_w_proj,
           ln2_g, ln2_b):
    batch, seq, d = x.shape
    assert d == D_MODEL
    n = batch * seq
    tb = 256
    kc = 512
    tables = _rope_tables(seq)
    bf = lambda a: a.astype(MXU_DTYPE)
    vec = lambda a: a.reshape(1, -1)
    n_le = DEPTH * N_EXPERTS
    w_gu_all = exp_w_gu.reshape(n_le, D_MODEL, 2 * D_FF)
    b_gu_all = exp_b_gu.reshape(n_le, 1, 2 * D_FF)
    w_down_all = exp_w_down.reshape(n_le, D_FF, D_MODEL)
    b_down_all = exp_b_down.reshape(n_le, 1, D_MODEL)
    p_all = p.reshape(DEPTH * n, PLE_DIM)

    h = x.reshape(n, d)
    for i in range(DEPTH):
        w = w_in[i]
        w1 = bf(jnp.concatenate([w[:, :1352], jnp.zeros((d, OFF_U - 1352), w.dtype), w[:, 1352:1864]], axis=1))
        wg = bf(w[:, 1864:])
        if i == 0:
            q, k, v_t, qi, ki, wi_t, u, h = _inproj(h, w1, tables, seq, kc, entry_ln=(ln0_g, ln0_b))
        else:
            q, k, v_t, qi, ki, wi_t, u = _inproj(h, w1, tables, seq, kc)
        y_attn = _attention(qi, wi_t, q, ki, k, v_t, batch, seq, 512, kc)
        rw = bf(jnp.pad(router_w[i], ((0, 0), (0, LANES - N_EXPERTS))))
        rb = jnp.pad(router_b[i], (0, LANES - N_EXPERTS)).reshape(1, -1)
        x1_rows, idx_l, gate_l = _merge(h, y_attn, u, wg, bf(w_up_attn[i]), bf(pool_w[i]), vec(pool_scale[i]),
                                        bf(w_up_pool[i]), bf(w_out[i]), vec(ln1_g[i]), vec(ln1_b[i]), rw, rb, seq)
        dest, row_tok, blk_exp, n_used = _route(idx_l[:, :TOP_K], n, tb)
        yr = _experts(blk_exp + i * N_EXPERTS, n_used, row_tok, x1_rows,
                      w_gu_all, b_gu_all, w_down_all, b_down_all, tb)
        h = _combine(dest, yr, gate_l, x1_rows, p_all, i, bf(ple_w_gate[i]), bf(ple_w_proj[i]),
                     vec(ln2_g[i]), vec(ln2_b[i]))
    return h.reshape(batch, seq, d)
```

```python
import functools

import jax
import jax.numpy as jnp
import numpy as np
from jax import lax
from jax.experimental import pallas as pl
from jax.experimental.pallas import tpu as pltpu

MXU_DTYPE = jnp.bfloat16

D_MODEL = 1024
HEAD_DIM = 64
N_HEADS = 8
N_KV_HEADS = 2
ROT_DIM = 16
ROPE_THETA = 500000.0
IDX_HEADS = 8
IDX_DIM = 64
MAX_TOPK = 256
POOL_CH = 512
POOL_WINDOWS = (2, 4, 8, 16)
POOL_GROUP = 128
POOL_HALO = 16
N_EXPERTS = 32
TOP_K = 4
D_FF = 1024
SWIGLU_LIMIT = 7.0
SWIGLU_ALPHA = 1.702
PLE_DIM = 256
LN_EPS = 1e-5
DEPTH = 2
DN_ALPHA = (2 * DEPTH) ** 0.25
LOG2_E = 1.4426950408889634

LANES = 128
SUBLANES = 8
ROW_TILES = D_MODEL // LANES
assert ROW_TILES == SUBLANES

OFF_Q, OFF_K, OFF_V, OFF_QI, OFF_KIW, OFF_U, W1_COLS = 0, 512, 640, 768, 1280, 1408, 1920

VMEM_LIMIT = 56 * 1024 * 1024
NEG_INF = float("-inf")
KEY_NEG_INF = -2139095041
HALF_BITS = 16
HALF_BIAS = 1 << (HALF_BITS - 1)
SEARCH_UNROLL = 4
ACC_ROWS = 80
GATHER_UNROLL = 8
WEIGHT_DMA_PRIORITY = 1

def _cparams(n_axes, flags=None):
    return pltpu.CompilerParams(dimension_semantics=("arbitrary",) * n_axes,
                                vmem_limit_bytes=VMEM_LIMIT, flags=flags)


def _mm(a, b):
    return jnp.dot(a.astype(MXU_DTYPE), b.astype(MXU_DTYPE), preferred_element_type=jnp.float32)


def _mm_nt(a, b):
    return lax.dot_general(a.astype(MXU_DTYPE), b.astype(MXU_DTYPE), (((1,), (1,)), ((), ())),
                           preferred_element_type=jnp.float32)


def _layer_norm(h, g, b):
    mu = jnp.mean(h, axis=-1, keepdims=True)
    c = h - mu
    var = jnp.mean(c * c, axis=-1, keepdims=True)
    return c * lax.rsqrt(var + LN_EPS) * g + b


def _rope_tables(seq):
    pos = jnp.arange(seq, dtype=jnp.float32)
    inv = ROPE_THETA ** (-jnp.arange(0, ROT_DIM, 2, dtype=jnp.float32) / ROT_DIM)
    ang = pos[:, None] * inv[None, :]
    half = ROT_DIM // 2
    sel_c = np.zeros((half, 2 * LANES), np.float32)
    sel_s1 = np.zeros((half, 2 * LANES), np.float32)
    sel_s2 = np.zeros((half, 2 * LANES), np.float32)
    ones = np.ones((1, 2 * LANES), np.float32)
    for lane in range(2 * LANES - HEAD_DIM):
        d = lane % HEAD_DIM
        if d < ROT_DIM:
            sel_c[d % half, lane], ones[0, lane] = 1.0, 0.0
            (sel_s1 if d < half else sel_s2)[d % half, lane] = -1.0 if d < half else 1.0
    spread = lambda a, sel: jnp.dot(a, jnp.asarray(sel), precision=lax.Precision.HIGHEST)
    cos, sin = jnp.cos(ang), jnp.sin(ang)
    return spread(cos, sel_c) + jnp.asarray(ones), spread(sin, sel_s1), spread(sin, sel_s2)


def _rope_tile(x, c, s1, s2):
    half = ROT_DIM // 2
    return x * c + pltpu.roll(x, LANES - half, axis=1) * s1 + pltpu.roll(x, half, axis=1) * s2


def _inproj_kernel(x_ref, w_ref, c_ref, s1_ref, s2_ref, *rest, entry_norm):
    if entry_norm:
        g_ref, b_ref, q_ref, k_ref, v_ref, qi_ref, ki_ref, wi_ref, u_ref, h_ref = rest
        h = _layer_norm(x_ref[...], g_ref[...], b_ref[...])
        h_ref[...] = h
    else:
        q_ref, k_ref, v_ref, qi_ref, ki_ref, wi_ref, u_ref = rest
        h = x_ref[...]
    xb = h.astype(MXU_DTYPE)
    c, s1, s2 = c_ref[:, :LANES], s1_ref[:, :LANES], s2_ref[:, :LANES]
    ck, s1k, s2k = c_ref[:, LANES:], s1_ref[:, LANES:], s2_ref[:, LANES:]

    def seg(off, width):
        return jnp.dot(xb, w_ref[:, off:off + width], preferred_element_type=jnp.float32)

    def roped(z, scale, out_ref):
        for j in range(z.shape[1] // LANES):
            t = _rope_tile(z[:, j * LANES:(j + 1) * LANES], c, s1, s2)
            out_ref[:, j * LANES:(j + 1) * LANES] = (t * scale).astype(out_ref.dtype)

    roped(seg(OFF_Q, 512), HEAD_DIM ** -0.5 * LOG2_E, q_ref)
    roped(seg(OFF_K, 128), 1.0, k_ref)
    v_ref[0] = seg(OFF_V, 128).T.astype(v_ref.dtype)
    roped(seg(OFF_QI, 512), IDX_DIM ** -0.5, qi_ref)
    kiw = seg(OFF_KIW, 128)
    ki_ref[...] = _rope_tile(kiw, ck, s1k, s2k).astype(ki_ref.dtype)
    wi_ref[...] = (kiw * (IDX_HEADS ** -0.5)).T
    u_ref[...] = seg(OFF_U, 512)


def _inproj(x, w1, tables, seq, tm, entry_ln=None):
    n = x.shape[0]
    nseq = seq // tm
    c, s1, s2 = tables
    tab_spec = pl.BlockSpec((tm, 2 * LANES), lambda i: (i % nseq, 0))

    def out(width, dtype):
        return pl.BlockSpec((tm, width), lambda i: (i, 0)), jax.ShapeDtypeStruct((n, width), dtype)

    v_t = (pl.BlockSpec((1, LANES, tm), lambda i: (i, 0, 0)), jax.ShapeDtypeStruct((n // tm, LANES, tm), MXU_DTYPE))
    wi_t = (pl.BlockSpec((LANES, tm), lambda i: (0, i)), jax.ShapeDtypeStruct((LANES, n), jnp.float32))
    outs = [out(512, MXU_DTYPE), out(128, MXU_DTYPE), v_t, out(512, MXU_DTYPE),
            out(128, MXU_DTYPE), wi_t, out(512, jnp.float32)]
    vec_spec = pl.BlockSpec((1, D_MODEL), lambda i: (0, 0))
    ln_specs, ln_args = [], []
    if entry_ln is not None:
        outs.append(out(D_MODEL, jnp.float32))
        ln_specs, ln_args = [vec_spec, vec_spec], [a.reshape(1, -1) for a in entry_ln]
    return pl.pallas_call(
        functools.partial(_inproj_kernel, entry_norm=entry_ln is not None),
        grid=(n // tm,),
        in_specs=[pl.BlockSpec((tm, D_MODEL), lambda i: (i, 0)),
                  pl.BlockSpec((D_MODEL, W1_COLS), lambda i: (0, 0)),
                  tab_spec, tab_spec, tab_spec] + ln_specs,
        out_specs=[o[0] for o in outs],
        out_shape=[o[1] for o in outs],
        compiler_params=_cparams(1),
        name="inproj",
    )(x, w1, c, s1, s2, *ln_args)


def _sortable_key(s):
    k = lax.bitcast_convert_type(s, jnp.int32)
    return k ^ ((k >> 31) & jnp.int32(0x7FFFFFFF))


def _attn_kernel(qi_ref, wi_ref, q_ref, ki_ref, k_ref, v_ref, tri_ref, o_ref,
                 key_ref, half_ref, s_ref, acc_ref, *, tq, kc, n_sel):
    j = pl.program_id(1)
    q0 = j * tq
    nch = (q0 + tq + kc - 1) // kc
    qpos = q0 + lax.broadcasted_iota(jnp.int32, (kc, tq), 1)
    krow = lax.broadcasted_iota(jnp.int32, (kc, tq), 0)

    def score_chunk(c, carry):
        ks = pl.multiple_of(c * kc, kc)
        kic = ki_ref[pl.ds(ks, kc), :][:, :IDX_DIM]
        acc = jnp.zeros((kc, tq), jnp.float32)
        for h in range(IDX_HEADS):
            d = _mm_nt(kic, qi_ref[:, h * IDX_DIM:(h + 1) * IDX_DIM])
            acc = acc + jnp.maximum(d, 0.0) * wi_ref[IDX_DIM + h:IDX_DIM + h + 1, :]
        causal = krow + ks <= qpos
        key = _sortable_key(jnp.where(causal, acc, NEG_INF))
        key_ref[c] = key
        half_ref[c] = (key >> 16).astype(jnp.int16)
        return carry

    lax.fori_loop(0, nch, score_chunk, 0)

    half_rows = 2 * SUBLANES

    def count_half_ge(cand):
        c16 = cand.astype(jnp.int16)
        c16 = jnp.concatenate([c16, c16], axis=0)
        def body(c, cnt):
            hit = jnp.where(half_ref[c].reshape(kc // half_rows, half_rows, tq) >= c16[None],
                            jnp.int16(1), jnp.int16(0))
            part = hit[0]
            for r in range(1, kc // half_rows):
                part = part + hit[r]
            return cnt + part
        cnt = lax.fori_loop(0, nch, body, jnp.zeros((half_rows, tq), jnp.int16)).astype(jnp.int32)
        cnt = cnt[0:SUBLANES, :] + cnt[SUBLANES:, :]
        for shift in (4, 2, 1):
            cnt = cnt + pltpu.roll(cnt, shift, axis=0)
        return cnt

    def bisect(base, frozen, c_lo, c_hi):
        def one_pass(st):
            lo, hi, c_lo, c_hi = st
            cand = lo + ((hi >> 1) - (lo >> 1))
            cnt = base + count_half_ge(cand)
            up = jnp.logical_not(frozen) & (cnt >= n_sel)
            dn = jnp.logical_not(frozen) & (cnt < n_sel)
            return (jnp.where(up, cand, lo), jnp.where(dn, cand, hi),
                    jnp.where(up, cnt, c_lo), jnp.where(dn, cnt, c_hi))

        def step(_, st):
            for _ in range(SEARCH_UNROLL):
                st = one_pass(st)
            return st

        lo0 = jnp.full((SUBLANES, tq), -HALF_BIAS, jnp.int32)
        lo, _, c_lo, c_hi = lax.fori_loop(0, HALF_BITS // SEARCH_UNROLL, step, (lo0, -lo0, c_lo, c_hi))
        return lo, c_lo, c_hi

    zeros = jnp.zeros((SUBLANES, tq), jnp.int32)
    n_valid = q0 + lax.broadcasted_iota(jnp.int32, (SUBLANES, tq), 1) + 1
    all_selected = n_valid <= n_sel

    h, c_ge_h, c_gt_h = bisect(zeros, all_selected, jnp.where(all_selected, n_valid, nch * kc), zeros)
    settled = all_selected | (c_ge_h == n_sel)

    def low_chunk(c, carry):
        key = key_ref[c]
        low = jnp.where((key >> 16) == h[0:1, :], (key & 0xFFFF) - HALF_BIAS, -HALF_BIAS)
        half_ref[c] = low.astype(jnp.int16)
        return carry

    lax.fori_loop(0, nch, low_chunk, 0)
    low, c_lo, c_hi = bisect(c_gt_h, settled, c_ge_h, c_gt_h)
    thr = jnp.where(all_selected, KEY_NEG_INF, h * (2 * HALF_BIAS) + (low + HALF_BIAS))
    thr, c_lo, c_hi = thr[0:1, :], c_lo[0:1, :], c_hi[0:1, :]
    need = jnp.where(c_lo == n_sel, float(2 ** 30), (n_sel - c_hi).astype(jnp.float32))

    def bias_chunk(with_ties, c, n_eq_before):
        ks = pl.multiple_of(c * kc, kc)
        key = key_ref[c]
        causal = krow + ks <= qpos
        if with_ties:
            eq = key == thr
            rank = _mm(tri_ref[...], jnp.where(eq, 1.0, 0.0)) + n_eq_before
            sel = ((key > thr) | (eq & (rank <= need))) & causal
            n_eq_before = rank[kc - 1:kc, :]
        else:
            sel = (key >= thr) & causal
        key_ref[c] = lax.bitcast_convert_type(jnp.where(sel, 0.0, NEG_INF), jnp.int32)
        return n_eq_before

    has_ties = jnp.max(c_lo) > n_sel

    @pl.when(has_ties)
    def _():
        lax.fori_loop(0, nch, functools.partial(bias_chunk, True), jnp.zeros((1, tq), jnp.float32))

    @pl.when(jnp.logical_not(has_ties))
    def _():
        lax.fori_loop(0, nch, functools.partial(bias_chunk, False), jnp.zeros((1, tq), jnp.float32))

    acc_ref[...] = jnp.zeros(acc_ref.shape, jnp.float32)
    group = N_HEADS // N_KV_HEADS
    ones_rows = jnp.ones((ACC_ROWS - HEAD_DIM, kc), MXU_DTYPE)

    def attend_chunk(c, ms):
        ks = pl.multiple_of(c * kc, kc)
        bias = lax.bitcast_convert_type(key_ref[c], jnp.float32)
        kch = k_ref[pl.ds(ks, kc), :]
        vch = v_ref[c]
        vext = [jnp.concatenate([vch[g * HEAD_DIM:(g + 1) * HEAD_DIM, :], ones_rows], axis=0)
                for g in range(N_KV_HEADS)]
        slot0 = jnp.minimum(c, 0)
        mx = []
        for h in range(N_HEADS):
            g = h // group
            s = _mm_nt(kch[:, g * HEAD_DIM:(g + 1) * HEAD_DIM], q_ref[:, h * HEAD_DIM:(h + 1) * HEAD_DIM]) + bias
            s_ref[slot0 + h] = s
            mx.append(jnp.max(s, axis=0, keepdims=True))
        new_ms = []
        for h in range(N_HEADS):
            m_new = jnp.maximum(ms[h], mx[h])
            p = jnp.exp2(s_ref[slot0 + h] - m_new)
            acc_ref[h] = jnp.exp2(ms[h] - m_new) * acc_ref[h] + _mm(vext[h // group], p)
            new_ms.append(m_new)
        return tuple(new_ms)

    lax.fori_loop(0, nch, attend_chunk, tuple(jnp.full((1, tq), -1e30, jnp.float32) for _ in range(N_HEADS)))

    out_t = jnp.concatenate([acc_ref[h, 0:HEAD_DIM, :] / acc_ref[h, HEAD_DIM:HEAD_DIM + 1, :]
                             for h in range(N_HEADS)], axis=0)
    o_ref[...] = out_t.T.astype(o_ref.dtype)


def _attention(qi, wi_t, q, ki, k, v_t, batch, seq, tq, kc):
    n = q.shape[0]
    n_sel = min(MAX_TOPK, seq // 4)
    nq = seq // tq
    nck = seq // kc
    assert seq % tq == 0 and seq % kc == 0 and kc >= n_sel and v_t.shape == (n // kc, LANES, kc)
    tri = jnp.asarray((np.arange(kc)[None, :] <= np.arange(kc)[:, None]).astype(np.float32), MXU_DTYPE)
    qblk = lambda w: pl.BlockSpec((tq, w), lambda b, j: (b * nq + j, 0))
    seqblk = pl.BlockSpec((seq, LANES), lambda b, j: (b, 0))
    return pl.pallas_call(
        functools.partial(_attn_kernel, tq=tq, kc=kc, n_sel=n_sel),
        grid=(batch, nq),
        in_specs=[qblk(512), pl.BlockSpec((LANES, tq), lambda b, j: (0, b * nq + j)), qblk(512),
                  seqblk, seqblk, pl.BlockSpec((nck, LANES, kc), lambda b, j: (b, 0, 0)),
                  pl.BlockSpec((kc, kc), lambda b, j: (0, 0))],
        out_specs=qblk(512),
        out_shape=jax.ShapeDtypeStruct((n, N_HEADS * HEAD_DIM), MXU_DTYPE),
        scratch_shapes=[pltpu.VMEM((nck, kc, tq), jnp.int32),
                        pltpu.VMEM((nck, kc, tq), jnp.int16),
                        pltpu.VMEM((N_HEADS, kc, tq), jnp.float32),
                        pltpu.VMEM((N_HEADS, ACC_ROWS, tq), jnp.float32)],
        compiler_params=_cparams(2),
        name="dsa_attention",
    )(qi, wi_t, q, ki, k, v_t, tri)


def _merge_kernel(x_ref, ya_ref, u_ref, uh_ref, wg_ref, wua_ref, pw_ref, ps_ref, wup_ref, wo_ref,
                  g_ref, b_ref, rw_ref, rb_ref,
                  x1_ref, idx_ref, gate_ref, ext_ref, *, tm, seq):
    i = pl.program_id(0)
    x = x_ref[...]
    xb = x.astype(MXU_DTYPE)

    first = (i % (seq // tm)) == 0
    ext_ref[0:POOL_HALO, :] = jnp.where(first, 0.0, uh_ref[...])
    ext_ref[POOL_HALO:, :] = u_ref[...]
    pos1 = ((i % (seq // tm)) * tm + 1 + lax.broadcasted_iota(jnp.int32, (tm, 1), 0)).astype(jnp.float32)
    parts = []
    for g, win in enumerate(POOL_WINDOWS):
        e = ext_ref[:, g * POOL_GROUP:(g + 1) * POOL_GROUP]
        w = 1
        while w < win:
            e = e + pltpu.roll(e, w, axis=0)
            w *= 2
        tok = e[POOL_HALO:, :]
        ug = u_ref[:, g * POOL_GROUP:(g + 1) * POOL_GROUP]
        d = tok / jnp.minimum(pos1, float(win)) - ug
        parts.append(_mm(d, pw_ref[g]))
    y_pool = jnp.concatenate(parts, axis=1) * ps_ref[...]

    gates = jnp.dot(xb, wg_ref[...], preferred_element_type=jnp.float32)
    merged = (jax.nn.sigmoid(gates[:, :D_MODEL]) * _mm(ya_ref[...], wua_ref[...])
              + jax.nn.sigmoid(gates[:, D_MODEL:]) * _mm(y_pool, wup_ref[...]))
    mix = _mm(merged, wo_ref[...])
    x1 = _layer_norm(DN_ALPHA * x + mix, g_ref[...], b_ref[...])
    _store_rows(x1_ref, x1)

    logits = _mm(x1, rw_ref[...]) + rb_ref[...]
    lane = lax.broadcasted_iota(jnp.int32, logits.shape, 1)
    work = jnp.where(lane < N_EXPERTS, logits, NEG_INF)
    vals, idxs = [], []
    for _ in range(TOP_K):
        m = jnp.max(work, axis=1, keepdims=True)
        ix = jnp.min(jnp.where(work == m, lane, LANES), axis=1, keepdims=True)
        vals.append(m)
        idxs.append(ix)
        work = jnp.where(lane == ix, NEG_INF, work)
    es = [jnp.exp(vv - vals[0]) for vv in vals]
    den = es[0] + es[1] + es[2] + es[3]
    idx_out = jnp.zeros(logits.shape, jnp.int32)
    gate_out = jnp.zeros(logits.shape, jnp.float32)
    for kk in range(TOP_K):
        idx_out = jnp.where(lane == kk, idxs[kk], idx_out)
        gate_out = jnp.where(lane == kk, es[kk] / den, gate_out)
    idx_ref[...] = idx_out
    gate_ref[...] = gate_out


def _merge(x, y_attn, u, wg, wua, pw, ps, wup, wo, g, b, rw, rb, seq, tm=512):
    n = x.shape[0]
    hb = tm // POOL_HALO
    full = lambda shape: pl.BlockSpec(shape, lambda i: (0,) * len(shape))
    row = lambda w: pl.BlockSpec((tm, w), lambda i: (i, 0))
    return pl.pallas_call(
        functools.partial(_merge_kernel, tm=tm, seq=seq),
        grid=(n // tm,),
        in_specs=[row(D_MODEL), row(512), row(POOL_CH),
                  pl.BlockSpec((POOL_HALO, POOL_CH), lambda i: (jnp.maximum(i * hb - 1, 0), 0)),
                  full((D_MODEL, 2 * D_MODEL)), full((512, D_MODEL)),
                  full((4, POOL_GROUP, POOL_GROUP)), full((1, POOL_CH)), full((POOL_CH, D_MODEL)),
                  full((D_MODEL, D_MODEL)), full((1, D_MODEL)), full((1, D_MODEL)),
                  full((D_MODEL, LANES)), full((1, LANES))],
        out_specs=[pl.BlockSpec((tm * ROW_TILES, LANES), lambda i: (i, 0)), row(LANES), row(LANES)],
        out_shape=[jax.ShapeDtypeStruct((n * ROW_TILES, LANES), jnp.float32),
                   jax.ShapeDtypeStruct((n, LANES), jnp.int32),
                   jax.ShapeDtypeStruct((n, LANES), jnp.float32)],
        scratch_shapes=[pltpu.VMEM((tm + POOL_HALO, POOL_CH), jnp.float32)],
        compiler_params=_cparams(1),
        name="merge_ln1_router",
    )(x, y_attn, u, u, wg, wua, pw, ps, wup, wo, g, b, rw, rb)


def _gather_rows(idx_ref, n_rows, table_hbm, buf_ref, sem, both_queues=False):
    def issue(i, carry):
        for j in range(GATHER_UNROLL):
            r = i * GATHER_UNROLL + j
            t = idx_ref[0, 0, r]
            pltpu.make_async_copy(table_hbm.at[pl.ds(pl.multiple_of(t * ROW_TILES, ROW_TILES), ROW_TILES), :],
                                  buf_ref.at[pl.ds(pl.multiple_of(r * ROW_TILES, ROW_TILES), ROW_TILES), :],
                                  sem).start(priority=j % 2 if both_queues else 0)
        return carry
    assert n_rows % GATHER_UNROLL == 0
    lax.fori_loop(0, n_rows // GATHER_UNROLL, issue, 0)


def _wait_rows(table_hbm, buf_ref, sem):
    pltpu.make_async_copy(table_hbm.at[pl.ds(0, buf_ref.shape[0]), :], buf_ref, sem).wait()


def _rows_as_matrix(buf_ref, r0, n_rows):
    return jnp.concatenate(
        [buf_ref[pl.ds(r0 * ROW_TILES + c, n_rows, stride=ROW_TILES), :] for c in range(ROW_TILES)], axis=1)


def _store_rows(out_ref, y):
    for c in range(ROW_TILES):
        out_ref[pl.ds(c, y.shape[0], stride=ROW_TILES), :] = y[:, c * LANES:(c + 1) * LANES]


def _expert_kernel(be_ref, nu_ref, first_ref, nxt_ref, wslot_ref, tok_ref, tok_next_ref, x_hbm,
                   wgu_hbm, bgu_ref, wd_hbm, bd_ref, y_ref,
                   xbuf0, xbuf1, wgu_f0, wgu_f1, wd_f0, wd_f1, wgu_bf, wd_bf, sem, wsem, *, tb):
    b = pl.program_id(0)
    n_used = nu_ref[0]
    bufs = (xbuf0, xbuf1)
    wgu_f, wd_f = (wgu_f0, wgu_f1), (wd_f0, wd_f1)

    def weight_copies(e, s):
        return (pltpu.make_async_copy(wgu_hbm.at[e], wgu_f[s], wsem.at[s, 0]),
                pltpu.make_async_copy(wd_hbm.at[e], wd_f[s], wsem.at[s, 1]))

    @pl.when(b == 0)
    def _():
        for cp in weight_copies(be_ref[0], 0):
            cp.start(priority=WEIGHT_DMA_PRIORITY)
        _gather_rows(tok_ref, tb, x_hbm, xbuf0, sem.at[0])

    for slot in range(2):
        @pl.when((b + 1 < n_used) & ((b + 1) % 2 == slot))
        def _():
            _gather_rows(tok_next_ref, tb, x_hbm, bufs[slot], sem.at[slot])

    @pl.when(b < n_used)
    def _():
        for s in range(2):
            @pl.when((first_ref[b] == 1) & (wslot_ref[b] == s))
            def _():
                for cp in weight_copies(be_ref[b], s):
                    cp.wait()
                wgu_bf[...] = wgu_f[s][...].astype(MXU_DTYPE)
                wd_bf[...] = wd_f[s][...].astype(MXU_DTYPE)

                @pl.when(nxt_ref[b] >= 0)
                def _():
                    for cp in weight_copies(nxt_ref[b], 1 - s):
                        cp.start(priority=WEIGHT_DMA_PRIORITY)

        def compute(xbuf, slot):
            _wait_rows(x_hbm, xbuf, sem.at[slot])
            xr = _rows_as_matrix(xbuf, 0, tb)
            gu = _mm(xr, wgu_bf[...]) + bgu_ref[...]
            gt = jnp.minimum(gu[:, :D_FF], SWIGLU_LIMIT)
            up = jnp.clip(gu[:, D_FF:], -SWIGLU_LIMIT, SWIGLU_LIMIT)
            act = gt * jax.nn.sigmoid(SWIGLU_ALPHA * gt) * (up + 1.0)
            _store_rows(y_ref, _mm(act, wd_bf[...]) + bd_ref[...])

        for slot in range(2):
            pl.when(b % 2 == slot)(functools.partial(compute, bufs[slot], slot))

    @pl.when(b >= n_used)
    def _():
        y_ref[...] = jnp.zeros(y_ref.shape, y_ref.dtype)


def _experts(blk_exp, n_used, row_tok, x1_rows, w_gu, b_gu, w_down, b_down, tb):
    n_blocks = blk_exp.shape[0]
    blk = jnp.arange(n_blocks, dtype=jnp.int32)
    first = jnp.concatenate([jnp.ones((1,), jnp.int32), (blk_exp[1:] != blk_exp[:-1]).astype(jnp.int32)])
    wslot = (jnp.cumsum(first) - 1) % 2
    at_or_after = lax.cummin(jnp.where(first == 1, blk, n_blocks), reverse=True)
    next_first = jnp.concatenate([at_or_after[1:], jnp.full((1,), n_blocks, jnp.int32)])
    nxt = jnp.where(next_first < n_used[0], blk_exp[jnp.minimum(next_first, n_blocks - 1)], -1)
    bspec = lambda c: pl.BlockSpec((None, 1, c), lambda b, be, *_: (be[b], 0, 0))
    tok = row_tok.reshape(n_blocks, 1, tb)
    return pl.pallas_call(
        functools.partial(_expert_kernel, tb=tb),
        grid_spec=pltpu.PrefetchScalarGridSpec(
            num_scalar_prefetch=5,
            grid=(n_blocks,),
            in_specs=[pl.BlockSpec((1, 1, tb), lambda b, *_: (b, 0, 0), memory_space=pltpu.SMEM),
                      pl.BlockSpec((1, 1, tb), lambda b, *_: (jnp.minimum(b + 1, n_blocks - 1), 0, 0),
                                   memory_space=pltpu.SMEM),
                      pl.BlockSpec(memory_space=pl.ANY),
                      pl.BlockSpec(memory_space=pl.ANY), bspec(2 * D_FF),
                      pl.BlockSpec(memory_space=pl.ANY), bspec(D_MODEL)],
            out_specs=pl.BlockSpec((tb * ROW_TILES, LANES), lambda b, *_: (b, 0)),
            scratch_shapes=[pltpu.VMEM((tb * ROW_TILES, LANES), jnp.float32),
                            pltpu.VMEM((tb * ROW_TILES, LANES), jnp.float32),
                            pltpu.VMEM((D_MODEL, 2 * D_FF), jnp.float32),
                            pltpu.VMEM((D_MODEL, 2 * D_FF), jnp.float32),
                            pltpu.VMEM((D_FF, D_MODEL), jnp.float32),
                            pltpu.VMEM((D_FF, D_MODEL), jnp.float32),
                            pltpu.VMEM((D_MODEL, 2 * D_FF), MXU_DTYPE),
                            pltpu.VMEM((D_FF, D_MODEL), MXU_DTYPE),
                            pltpu.SemaphoreType.DMA((2,)),
                            pltpu.SemaphoreType.DMA((2, 2))]),
        out_shape=jax.ShapeDtypeStruct((n_blocks * tb * ROW_TILES, LANES), jnp.float32),
        compiler_params=_cparams(1),
        name="moe_experts",
    )(blk_exp, n_used, first, nxt.astype(jnp.int32), wslot.astype(jnp.int32), tok, tok, x1_rows,
      w_gu, b_gu, w_down, b_down)


def _combine_kernel(dest_ref, dest_next_ref, y_hbm, gate_ref, x1_ref, p_ref, wpg_ref, wpp_ref, g_ref, b_ref,
                    o_ref, ybuf0, ybuf1, sem, *, tm):
    i = pl.program_id(0)
    bufs = (ybuf0, ybuf1)

    @pl.when(i == 0)
    def _():
        _gather_rows(dest_ref, TOP_K * tm, y_hbm, ybuf0, sem.at[0], both_queues=True)

    for slot in range(2):
        @pl.when((i + 1 < pl.num_programs(0)) & ((i + 1) % 2 == slot))
        def _():
            _gather_rows(dest_next_ref, TOP_K * tm, y_hbm, bufs[slot], sem.at[slot], both_queues=True)

    x1 = _rows_as_matrix(x1_ref, 0, tm)
    ple = jax.nn.sigmoid(_mm(x1, wpg_ref[...])) * _mm(p_ref[...], wpp_ref[...])
    h = DN_ALPHA * x1 + ple
    gate = gate_ref[...]

    def finish(ybuf, slot):
        _wait_rows(y_hbm, ybuf, sem.at[slot])
        ffn = h
        for kk in range(TOP_K):
            ffn = ffn + _rows_as_matrix(ybuf, kk * tm, tm) * gate[:, kk:kk + 1]
        o_ref[...] = _layer_norm(ffn, g_ref[...], b_ref[...])

    for slot in range(2):
        pl.when(i % 2 == slot)(functools.partial(finish, bufs[slot], slot))


def _combine(dest, yr, gates, x1_rows, p_all, layer, wpg, wpp, g, b, tm=256):
    n = gates.shape[0]
    nt = n // tm
    full = lambda shape: pl.BlockSpec(shape, lambda i: (0,) * len(shape))
    row = lambda w: pl.BlockSpec((tm, w), lambda i: (i, 0))
    dest_t = dest.reshape(nt, tm, TOP_K).transpose(0, 2, 1).reshape(nt, 1, TOP_K * tm)
    return pl.pallas_call(
        functools.partial(_combine_kernel, tm=tm),
        grid=(nt,),
        in_specs=[pl.BlockSpec((1, 1, TOP_K * tm), lambda i: (i, 0, 0), memory_space=pltpu.SMEM),
                  pl.BlockSpec((1, 1, TOP_K * tm), lambda i: (jnp.minimum(i + 1, nt - 1), 0, 0),
                               memory_space=pltpu.SMEM),
                  pl.BlockSpec(memory_space=pl.ANY),
                  row(LANES), pl.BlockSpec((tm * ROW_TILES, LANES), lambda i: (i, 0)),
                  pl.BlockSpec((tm, PLE_DIM), lambda i: (layer * nt + i, 0)),
                  full((D_MODEL, D_MODEL)), full((PLE_DIM, D_MODEL)),
                  full((1, D_MODEL)), full((1, D_MODEL))],
        out_specs=row(D_MODEL),
        out_shape=jax.ShapeDtypeStruct((n, D_MODEL), jnp.float32),
        scratch_shapes=[pltpu.VMEM((TOP_K * tm * ROW_TILES, LANES), jnp.float32),
                        pltpu.VMEM((TOP_K * tm * ROW_TILES, LANES), jnp.float32),
                        pltpu.SemaphoreType.DMA((2,))],
        compiler_params=_cparams(1),
        name="combine_ple_ln2",
    )(dest_t, dest_t, yr, gates, x1_rows, p_all, wpg, wpp, g, b)


def _route(top_idx, n_tokens, tb):
    a = n_tokens * TOP_K
    n_blocks = a // tb + N_EXPERTS
    onehot = (top_idx[:, :, None] == jnp.arange(N_EXPERTS, dtype=jnp.int32)[None, None, :]).astype(jnp.int32)
    member = onehot.sum(axis=1)
    rank = jnp.cumsum(member, axis=0) - member
    counts = member.sum(axis=0)
    padded = (counts + tb - 1) // tb * tb
    pend = jnp.cumsum(padded)
    pstart = pend - padded
    dest = jnp.take_along_axis(rank + pstart[None, :], top_idx, axis=1).astype(jnp.int32)
    blk_start = jnp.arange(n_blocks, dtype=jnp.int32) * tb
    blk_exp = jnp.minimum((blk_start[:, None] >= pend[None, :]).astype(jnp.int32).sum(axis=1), N_EXPERTS - 1)
    n_used = (pend[-1] // tb).astype(jnp.int32).reshape(1)
    tok = jnp.arange(n_tokens, dtype=jnp.int32)[:, None]
    pairs = jnp.sort((top_idx * n_tokens + tok).reshape(-1))
    first = jnp.cumsum(counts) - counts
    e = blk_exp[:, None]
    j = blk_start[:, None] + jnp.arange(tb, dtype=jnp.int32)[None, :] - pstart[e]
    src = jnp.clip(first[e] + j, 0, a - 1)
    row_tok = jnp.where(j < counts[e], pairs[src] % n_tokens, 0).astype(jnp.int32)
    return dest, row_tok, blk_exp, n_used


def kernel(x, p, ln0_g, ln0_b, w_in, pool_w, pool_scale, w_up_attn, w_up_pool, w_out, ln1_g, ln1_b,
           router_w, router_b, exp_w_gu, exp_b_gu, exp_w_down, exp_b_down, ple_w_gate, ple_w_proj,
           ln2_g, ln2_b):
    batch, seq, d = x.shape
    assert d == D_MODEL
    n = batch * seq
    tb = 256
    kc = 512
    tables = _rope_tables(seq)
    bf = lambda a: a.astype(MXU_DTYPE)
    vec = lambda a: a.reshape(1, -1)
    n_le = DEPTH * N_EXPERTS
    w_gu_all = exp_w_gu.reshape(n_le, D_MODEL, 2 * D_FF)
    b_gu_all = exp_b_gu.reshape(n_le, 1, 2 * D_FF)
    w_down_all = exp_w_down.reshape(n_le, D_FF, D_MODEL)
    b_down_all = exp_b_down.reshape(n_le, 1, D_MODEL)
    p_all = p.reshape(DEPTH * n, PLE_DIM)

    h = x.reshape(n, d)
    for i in range(DEPTH):
        w = w_in[i]
        w1 = bf(jnp.concatenate([w[:, :1352], jnp.zeros((d, OFF_U - 1352), w.dtype), w[:, 1352:1864]], axis=1))
        wg = bf(w[:, 1864:])
        if i == 0:
            q, k, v_t, qi, ki, wi_t, u, h = _inproj(h, w1, tables, seq, kc, entry_ln=(ln0_g, ln0_b))
        else:
            q, k, v_t, qi, ki, wi_t, u = _inproj(h, w1, tables, seq, kc)
        y_attn = _attention(qi, wi_t, q, ki, k, v_t, batch, seq, 512, kc)
        rw = bf(jnp.pad(router_w[i], ((0, 0), (0, LANES - N_EXPERTS))))
        rb = jnp.pad(router_b[i], (0, LANES - N_EXPERTS)).reshape(1, -1)
        x1_rows, idx_l, gate_l = _merge(h, y_attn, u, wg, bf(w_up_attn[i]), bf(pool_w[i]), vec(pool_scale[i]),
                                        bf(w_up_pool[i]), bf(w_out[i]), vec(ln1_g[i]), vec(ln1_b[i]), rw, rb, seq)
        dest, row_tok, blk_exp, n_used = _route(idx_l[:, :TOP_K], n, tb)
        yr = _experts(blk_exp + i * N_EXPERTS, n_used, row_tok, x1_rows,
                      w_gu_all, b_gu_all, w_down_all, b_down_all, tb)
        h = _combine(dest, yr, gate_l, x1_rows, p_all, i, bf(ple_w_gate[i]), bf(ple_w_proj[i]),
                     vec(ln2_g[i]), vec(ln2_b[i]))
    return h.reshape(batch, seq, d)
```

```python
import functools

import jax
import jax.numpy as jnp
import numpy as np
from jax import lax
from jax.experimental import pallas as pl
from jax.experimental.pallas import tpu as pltpu

MXU_DTYPE = jnp.bfloat16

D_MODEL = 1024
HEAD_DIM = 64
N_HEADS = 8
N_KV_HEADS = 2
ROT_DIM = 16
ROPE_THETA = 500000.0
IDX_HEADS = 8
IDX_DIM = 64
MAX_TOPK = 256
POOL_CH = 512
POOL_WINDOWS = (2, 4, 8, 16)
POOL_GROUP = 128
POOL_HALO = 16
N_EXPERTS = 32
TOP_K = 4
D_FF = 1024
SWIGLU_LIMIT = 7.0
SWIGLU_ALPHA = 1.702
PLE_DIM = 256
LN_EPS = 1e-5
DEPTH = 2
DN_ALPHA = (2 * DEPTH) ** 0.25
LOG2_E = 1.4426950408889634

LANES = 128
SUBLANES = 8
ROW_TILES = D_MODEL // LANES
assert ROW_TILES == SUBLANES

OFF_Q, OFF_K, OFF_V, OFF_QI, OFF_KIW, OFF_U, W1_COLS = 0, 512, 640, 768, 1280, 1408, 1920

VMEM_LIMIT = 56 * 1024 * 1024
NEG_INF = float("-inf")
KEY_NEG_INF = -2139095041
HALF_BITS = 16
HALF_BIAS = 1 << (HALF_BITS - 1)
SEARCH_UNROLL = 4
ACC_ROWS = 80
GATHER_UNROLL = 8
WEIGHT_DMA_PRIORITY = 1

def _cparams(n_axes, flags=None):
    return pltpu.CompilerParams(dimension_semantics=("arbitrary",) * n_axes,
                                vmem_limit_bytes=VMEM_LIMIT, flags=flags)


def _mm(a, b):
    return jnp.dot(a.astype(MXU_DTYPE), b.astype(MXU_DTYPE), preferred_element_type=jnp.float32)


def _mm_nt(a, b):
    return lax.dot_general(a.astype(MXU_DTYPE), b.astype(MXU_DTYPE), (((1,), (1,)), ((), ())),
                           preferred_element_type=jnp.float32)


def _layer_norm(h, g, b):
    mu = jnp.mean(h, axis=-1, keepdims=True)
    c = h - mu
    var = jnp.mean(c * c, axis=-1, keepdims=True)
    return c * lax.rsqrt(var + LN_EPS) * g + b


def _rope_tables(seq):
    pos = jnp.arange(seq, dtype=jnp.float32)
    inv = ROPE_THETA ** (-jnp.arange(0, ROT_DIM, 2, dtype=jnp.float32) / ROT_DIM)
    ang = pos[:, None] * inv[None, :]
    half = ROT_DIM // 2
    sel_c = np.zeros((half, 2 * LANES), np.float32)
    sel_s1 = np.zeros((half, 2 * LANES), np.float32)
    sel_s2 = np.zeros((half, 2 * LANES), np.float32)
    ones = np.ones((1, 2 * LANES), np.float32)
    for lane in range(2 * LANES - HEAD_DIM):
        d = lane % HEAD_DIM
        if d < ROT_DIM:
            sel_c[d % half, lane], ones[0, lane] = 1.0, 0.0
            (sel_s1 if d < half else sel_s2)[d % half, lane] = -1.0 if d < half else 1.0
    spread = lambda a, sel: jnp.dot(a, jnp.asarray(sel), precision=lax.Precision.HIGHEST)
    cos, sin = jnp.cos(ang), jnp.sin(ang)
    return spread(cos, sel_c) + jnp.asarray(ones), spread(sin, sel_s1), spread(sin, sel_s2)


def _rope_tile(x, c, s1, s2):
    half = ROT_DIM // 2
    return x * c + pltpu.roll(x, LANES - half, axis=1) * s1 + pltpu.roll(x, half, axis=1) * s2


def _inproj_kernel(x_ref, w_ref, c_ref, s1_ref, s2_ref, *rest, entry_norm):
    if entry_norm:
        g_ref, b_ref, q_ref, k_ref, v_ref, qi_ref, ki_ref, wi_ref, u_ref, h_ref = rest
        h = _layer_norm(x_ref[...], g_ref[...], b_ref[...])
        h_ref[...] = h
    else:
        q_ref, k_ref, v_ref, qi_ref, ki_ref, wi_ref, u_ref = rest
        h = x_ref[...]
    xb = h.astype(MXU_DTYPE)
    c, s1, s2 = c_ref[:, :LANES], s1_ref[:, :LANES], s2_ref[:, :LANES]
    ck, s1k, s2k = c_ref[:, LANES:], s1_ref[:, LANES:], s2_ref[:, LANES:]

    def seg(off, width):
        return jnp.dot(xb, w_ref[:, off:off + width], preferred_element_type=jnp.float32)

    def roped(z, scale, out_ref):
        for j in range(z.shape[1] // LANES):
            t = _rope_tile(z[:, j * LANES:(j + 1) * LANES], c, s1, s2)
            out_ref[:, j * LANES:(j + 1) * LANES] = (t * scale).astype(out_ref.dtype)

    roped(seg(OFF_Q, 512), HEAD_DIM ** -0.5 * LOG2_E, q_ref)
    roped(seg(OFF_K, 128), 1.0, k_ref)
    v_ref[0] = seg(OFF_V, 128).T.astype(v_ref.dtype)
    roped(seg(OFF_QI, 512), IDX_DIM ** -0.5, qi_ref)
    kiw = seg(OFF_KIW, 128)
    ki_ref[...] = _rope_tile(kiw, ck, s1k, s2k).astype(ki_ref.dtype)
    wi_ref[...] = (kiw * (IDX_HEADS ** -0.5)).T
    u_ref[...] = seg(OFF_U, 512)


def _inproj(x, w1, tables, seq, tm, entry_ln=None):
    n = x.shape[0]
    nseq = seq // tm
    c, s1, s2 = tables
    tab_spec = pl.BlockSpec((tm, 2 * LANES), lambda i: (i % nseq, 0))

    def out(width, dtype):
        return pl.BlockSpec((tm, width), lambda i: (i, 0)), jax.ShapeDtypeStruct((n, width), dtype)

    v_t = (pl.BlockSpec((1, LANES, tm), lambda i: (i, 0, 0)), jax.ShapeDtypeStruct((n // tm, LANES, tm), MXU_DTYPE))
    wi_t = (pl.BlockSpec((LANES, tm), lambda i: (0, i)), jax.ShapeDtypeStruct((LANES, n), jnp.float32))
    outs = [out(512, MXU_DTYPE), out(128, MXU_DTYPE), v_t, out(512, MXU_DTYPE),
            out(128, MXU_DTYPE), wi_t, out(512, jnp.float32)]
    vec_spec = pl.BlockSpec((1, D_MODEL), lambda i: (0, 0))
    ln_specs, ln_args = [], []
    if entry_ln is not None:
        outs.append(out(D_MODEL, jnp.float32))
        ln_specs, ln_args = [vec_spec, vec_spec], [a.reshape(1, -1) for a in entry_ln]
    return pl.pallas_call(
        functools.partial(_inproj_kernel, entry_norm=entry_ln is not None),
        grid=(n // tm,),
        in_specs=[pl.BlockSpec((tm, D_MODEL), lambda i: (i, 0)),
                  pl.BlockSpec((D_MODEL, W1_COLS), lambda i: (0, 0)),
                  tab_spec, tab_spec, tab_spec] + ln_specs,
        out_specs=[o[0] for o in outs],
        out_shape=[o[1] for o in outs],
        compiler_params=_cparams(1),
        name="inproj",
    )(x, w1, c, s1, s2, *ln_args)


def _sortable_key(s):
    k = lax.bitcast_convert_type(s, jnp.int32)
    return k ^ ((k >> 31) & jnp.int32(0x7FFFFFFF))


def _attn_kernel(qi_ref, wi_ref, q_ref, ki_ref, k_ref, v_ref, tri_ref, o_ref,
                 key_ref, half_ref, s_ref, acc_ref, *, tq, kc, n_sel):
    j = pl.program_id(1)
    q0 = j * tq
    nch = (q0 + tq + kc - 1) // kc
    qpos = q0 + lax.broadcasted_iota(jnp.int32, (kc, tq), 1)
    krow = lax.broadcasted_iota(jnp.int32, (kc, tq), 0)

    def score_chunk(c, carry):
        ks = pl.multiple_of(c * kc, kc)
        kic = ki_ref[pl.ds(ks, kc), :][:, :IDX_DIM]
        acc = jnp.zeros((kc, tq), jnp.float32)
        for h in range(IDX_HEADS):
            d = _mm_nt(kic, qi_ref[:, h * IDX_DIM:(h + 1) * IDX_DIM])
            acc = acc + jnp.maximum(d, 0.0) * wi_ref[IDX_DIM + h:IDX_DIM + h + 1, :]
        causal = krow + ks <= qpos
        key = _sortable_key(jnp.where(causal, acc, NEG_INF))
        key_ref[c] = key
        half_ref[c] = (key >> 16).astype(jnp.int16)
        return carry

    lax.fori_loop(0, nch, score_chunk, 0)

    half_rows = 2 * SUBLANES

    def count_half_ge(cand):
        c16 = cand.astype(jnp.int16)
        c16 = jnp.concatenate([c16, c16], axis=0)
        def body(c, cnt):
            hit = jnp.where(half_ref[c].reshape(kc // half_rows, half_rows, tq) >= c16[None],
                            jnp.int16(1), jnp.int16(0))
            part = hit[0]
            for r in range(1, kc // half_rows):
                part = part + hit[r]
            return cnt + part
        cnt = lax.fori_loop(0, nch, body, jnp.zeros((half_rows, tq), jnp.int16)).astype(jnp.int32)
        cnt = cnt[0:SUBLANES, :] + cnt[SUBLANES:, :]
        for shift in (4, 2, 1):
            cnt = cnt + pltpu.roll(cnt, shift, axis=0)
        return cnt

    def bisect(base, frozen, c_lo, c_hi):
        def one_pass(st):
            lo, hi, c_lo, c_hi = st
            cand = lo + ((hi >> 1) - (lo >> 1))
            cnt = base + count_half_ge(cand)
            up = jnp.logical_not(frozen) & (cnt >= n_sel)
            dn = jnp.logical_not(frozen) & (cnt < n_sel)
            return (jnp.where(up, cand, lo), jnp.where(dn, cand, hi),
                    jnp.where(up, cnt, c_lo), jnp.where(dn, cnt, c_hi))

        def step(_, st):
            for _ in range(SEARCH_UNROLL):
                st = one_pass(st)
            return st

        lo0 = jnp.full((SUBLANES, tq), -HALF_BIAS, jnp.int32)
        lo, _, c_lo, c_hi = lax.fori_loop(0, HALF_BITS // SEARCH_UNROLL, step, (lo0, -lo0, c_lo, c_hi))
        return lo, c_lo, c_hi

    zeros = jnp.zeros((SUBLANES, tq), jnp.int32)
    n_valid = q0 + lax.broadcasted_iota(jnp.int32, (SUBLANES, tq), 1) + 1
    all_selected = n_valid <= n_sel

    h, c_ge_h, c_gt_h = bisect(zeros, all_selected, jnp.where(all_selected, n_valid, nch * kc), zeros)
    settled = all_selected | (c_ge_h == n_sel)

    def low_chunk(c, carry):
        key = key_ref[c]
        low = jnp.where((key >> 16) == h[0:1, :], (key & 0xFFFF) - HALF_BIAS, -HALF_BIAS)
        half_ref[c] = low.astype(jnp.int16)
        return carry

    lax.fori_loop(0, nch, low_chunk, 0)
    low, c_lo, c_hi = bisect(c_gt_h, settled, c_ge_h, c_gt_h)
    thr = jnp.where(all_selected, KEY_NEG_INF, h * (2 * HALF_BIAS) + (low + HALF_BIAS))
    thr, c_lo, c_hi = thr[0:1, :], c_lo[0:1, :], c_hi[0:1, :]
    need = jnp.where(c_lo == n_sel, float(2 ** 30), (n_sel - c_hi).astype(jnp.float32))

    def bias_chunk(with_ties, c, n_eq_before):
        ks = pl.multiple_of(c * kc, kc)
        key = key_ref[c]
        causal = krow + ks <= qpos
        if with_ties:
            eq = key == thr
            rank = _mm(tri_ref[...], jnp.where(eq, 1.0, 0.0)) + n_eq_before
            sel = ((key > thr) | (eq & (rank <= need))) & causal
            n_eq_before = rank[kc - 1:kc, :]
        else:
            sel = (key >= thr) & causal
        key_ref[c] = lax.bitcast_convert_type(jnp.where(sel, 0.0, NEG_INF), jnp.int32)
        return n_eq_before

    has_ties = jnp.max(c_lo) > n_sel

    @pl.when(has_ties)
    def _():
        lax.fori_loop(0, nch, functools.partial(bias_chunk, True), jnp.zeros((1, tq), jnp.float32))

    @pl.when(jnp.logical_not(has_ties))
    def _():
        lax.fori_loop(0, nch, functools.partial(bias_chunk, False), jnp.zeros((1, tq), jnp.float32))

    acc_ref[...] = jnp.zeros(acc_ref.shape, jnp.float32)
    group = N_HEADS // N_KV_HEADS
    ones_rows = jnp.ones((ACC_ROWS - HEAD_DIM, kc), MXU_DTYPE)

    def attend_chunk(c, ms):
        ks = pl.multiple_of(c * kc, kc)
        bias = lax.bitcast_convert_type(key_ref[c], jnp.float32)
        kch = k_ref[pl.ds(ks, kc), :]
        vch = v_ref[c]
        vext = [jnp.concatenate([vch[g * HEAD_DIM:(g + 1) * HEAD_DIM, :], ones_rows], axis=0)
                for g in range(N_KV_HEADS)]
        slot0 = jnp.minimum(c, 0)
        mx = []
        for h in range(N_HEADS):
            g = h // group
            s = _mm_nt(kch[:, g * HEAD_DIM:(g + 1) * HEAD_DIM], q_ref[:, h * HEAD_DIM:(h + 1) * HEAD_DIM]) + bias
            s_ref[slot0 + h] = s
            mx.append(jnp.max(s, axis=0, keepdims=True))
        new_ms = []
        for h in range(N_HEADS):
            m_new = jnp.maximum(ms[h], mx[h])
            p = jnp.exp2(s_ref[slot0 + h] - m_new)
            acc_ref[h] = jnp.exp2(ms[h] - m_new) * acc_ref[h] + _mm(vext[h // group], p)
            new_ms.append(m_new)
        return tuple(new_ms)

    lax.fori_loop(0, nch, attend_chunk, tuple(jnp.full((1, tq), -1e30, jnp.float32) for _ in range(N_HEADS)))

    out_t = jnp.concatenate([acc_ref[h, 0:HEAD_DIM, :] / acc_ref[h, HEAD_DIM:HEAD_DIM + 1, :]
                             for h in range(N_HEADS)], axis=0)
    o_ref[...] = out_t.T.astype(o_ref.dtype)


def _attention(qi, wi_t, q, ki, k, v_t, batch, seq, tq, kc):
    n = q.shape[0]
    n_sel = min(MAX_TOPK, seq // 4)
    nq = seq // tq
    nck = seq // kc
    assert seq % tq == 0 and seq % kc == 0 and kc >= n_sel and v_t.shape == (n // kc, LANES, kc)
    tri = jnp.asarray((np.arange(kc)[None, :] <= np.arange(kc)[:, None]).astype(np.float32), MXU_DTYPE)
    qblk = lambda w: pl.BlockSpec((tq, w), lambda b, j: (b * nq + j, 0))
    seqblk = pl.BlockSpec((seq, LANES), lambda b, j: (b, 0))
    return pl.pallas_call(
        functools.partial(_attn_kernel, tq=tq, kc=kc, n_sel=n_sel),
        grid=(batch, nq),
        in_specs=[qblk(512), pl.BlockSpec((LANES, tq), lambda b, j: (0, b * nq + j)), qblk(512),
                  seqblk, seqblk, pl.BlockSpec((nck, LANES, kc), lambda b, j: (b, 0, 0)),
                  pl.BlockSpec((kc, kc), lambda b, j: (0, 0))],
        out_specs=qblk(512),
        out_shape=jax.ShapeDtypeStruct((n, N_HEADS * HEAD_DIM), MXU_DTYPE),
        scratch_shapes=[pltpu.VMEM((nck, kc, tq), jnp.int32),
                        pltpu.VMEM((nck, kc, tq), jnp.int16),
                        pltpu.VMEM((N_HEADS, kc, tq), jnp.float32),
                        pltpu.VMEM((N_HEADS, ACC_ROWS, tq), jnp.float32)],
        compiler_params=_cparams(2),
        name="dsa_attention",
    )(qi, wi_t, q, ki, k, v_t, tri)


def _merge_kernel(x_ref, ya_ref, u_ref, uh_ref, wg_ref, wua_ref, pw_ref, ps_ref, wup_ref, wo_ref,
                  g_ref, b_ref, rw_ref, rb_ref,
                  x1_ref, idx_ref, gate_ref, ext_ref, *, tm, seq):
    i = pl.program_id(0)
    x = x_ref[...]
    xb = x.astype(MXU_DTYPE)

    first = (i % (seq // tm)) == 0
    ext_ref[0:POOL_HALO, :] = jnp.where(first, 0.0, uh_ref[...])
    ext_ref[POOL_HALO:, :] = u_ref[...]
    pos1 = ((i % (seq // tm)) * tm + 1 + lax.broadcasted_iota(jnp.int32, (tm, 1), 0)).astype(jnp.float32)
    parts = []
    for g, win in enumerate(POOL_WINDOWS):
        e = ext_ref[:, g * POOL_GROUP:(g + 1) * POOL_GROUP]
        w = 1
        while w < win:
            e = e + pltpu.roll(e, w, axis=0)
            w *= 2
        tok = e[POOL_HALO:, :]
        ug = u_ref[:, g * POOL_GROUP:(g + 1) * POOL_GROUP]
        d = tok / jnp.minimum(pos1, float(win)) - ug
        parts.append(_mm(d, pw_ref[g]))
    y_pool = jnp.concatenate(parts, axis=1) * ps_ref[...]

    gates = jnp.dot(xb, wg_ref[...], preferred_element_type=jnp.float32)
    merged = (jax.nn.sigmoid(gates[:, :D_MODEL]) * _mm(ya_ref[...], wua_ref[...])
              + jax.nn.sigmoid(gates[:, D_MODEL:]) * _mm(y_pool, wup_ref[...]))
    mix = _mm(merged, wo_ref[...])
    x1 = _layer_norm(DN_ALPHA * x + mix, g_ref[...], b_ref[...])
    _store_rows(x1_ref, x1)

    logits = _mm(x1, rw_ref[...]) + rb_ref[...]
    lane = lax.broadcasted_iota(jnp.int32, logits.shape, 1)
    work = jnp.where(lane < N_EXPERTS, logits, NEG_INF)
    vals, idxs = [], []
    for _ in range(TOP_K):
        m = jnp.max(work, axis=1, keepdims=True)
        ix = jnp.min(jnp.where(work == m, lane, LANES), axis=1, keepdims=True)
        vals.append(m)
        idxs.append(ix)
        work = jnp.where(lane == ix, NEG_INF, work)
    es = [jnp.exp(vv - vals[0]) for vv in vals]
    den = es[0] + es[1] + es[2] + es[3]
    idx_out = jnp.zeros(logits.shape, jnp.int32)
    gate_out = jnp.zeros(logits.shape, jnp.float32)
    for kk in range(TOP_K):
        idx_out = jnp.where(lane == kk, idxs[kk], idx_out)
        gate_out = jnp.where(lane == kk, es[kk] / den, gate_out)
    idx_ref[...] = idx_out
    gate_ref[...] = gate_out


def _merge(x, y_attn, u, wg, wua, pw, ps, wup, wo, g, b, rw, rb, seq, tm=512):
    n = x.shape[0]
    hb = tm // POOL_HALO
    full = lambda shape: pl.BlockSpec(shape, lambda i: (0,) * len(shape))
    row = lambda w: pl.BlockSpec((tm, w), lambda i: (i, 0))
    return pl.pallas_call(
        functools.partial(_merge_kernel, tm=tm, seq=seq),
        grid=(n // tm,),
        in_specs=[row(D_MODEL), row(512), row(POOL_CH),
                  pl.BlockSpec((POOL_HALO, POOL_CH), lambda i: (jnp.maximum(i * hb - 1, 0), 0)),
                  full((D_MODEL, 2 * D_MODEL)), full((512, D_MODEL)),
                  full((4, POOL_GROUP, POOL_GROUP)), full((1, POOL_CH)), full((POOL_CH, D_MODEL)),
                  full((D_MODEL, D_MODEL)), full((1, D_MODEL)), full((1, D_MODEL)),
                  full((D_MODEL, LANES)), full((1, LANES))],
        out_specs=[pl.BlockSpec((tm * ROW_TILES, LANES), lambda i: (i, 0)), row(LANES), row(LANES)],
        out_shape=[jax.ShapeDtypeStruct((n * ROW_TILES, LANES), jnp.float32),
                   jax.ShapeDtypeStruct((n, LANES), jnp.int32),
                   jax.ShapeDtypeStruct((n, LANES), jnp.float32)],
        scratch_shapes=[pltpu.VMEM((tm + POOL_HALO, POOL_CH), jnp.float32)],
        compiler_params=_cparams(1),
        name="merge_ln1_router",
    )(x, y_attn, u, u, wg, wua, pw, ps, wup, wo, g, b, rw, rb)


def _gather_rows(idx_ref, n_rows, table_hbm, buf_ref, sem, both_queues=False):
    def issue(i, carry):
        for j in range(GATHER_UNROLL):
            r = i * GATHER_UNROLL + j
            t = idx_ref[0, 0, r]
            pltpu.make_async_copy(table_hbm.at[pl.ds(pl.multiple_of(t * ROW_TILES, ROW_TILES), ROW_TILES), :],
                                  buf_ref.at[pl.ds(pl.multiple_of(r * ROW_TILES, ROW_TILES), ROW_TILES), :],
                                  sem).start(priority=j % 2 if both_queues else 0)
        return carry
    assert n_rows % GATHER_UNROLL == 0
    lax.fori_loop(0, n_rows // GATHER_UNROLL, issue, 0)


def _wait_rows(table_hbm, buf_ref, sem):
    pltpu.make_async_copy(table_hbm.at[pl.ds(0, buf_ref.shape[0]), :], buf_ref, sem).wait()


def _rows_as_matrix(buf_ref, r0, n_rows):
    return jnp.concatenate(
        [buf_ref[pl.ds(r0 * ROW_TILES + c, n_rows, stride=ROW_TILES), :] for c in range(ROW_TILES)], axis=1)


def _store_rows(out_ref, y):
    for c in range(ROW_TILES):
        out_ref[pl.ds(c, y.shape[0], stride=ROW_TILES), :] = y[:, c * LANES:(c + 1) * LANES]


def _expert_kernel(be_ref, nu_ref, first_ref, nxt_ref, wslot_ref, tok_ref, tok_next_ref, x_hbm,
                   wgu_hbm, bgu_ref, wd_hbm, bd_ref, y_ref,
                   xbuf0, xbuf1, wgu_f0, wgu_f1, wd_f0, wd_f1, wgu_bf, wd_bf, sem, wsem, *, tb):
    b = pl.program_id(0)
    n_used = nu_ref[0]
    bufs = (xbuf0, xbuf1)
    wgu_f, wd_f = (wgu_f0, wgu_f1), (wd_f0, wd_f1)

    def weight_copies(e, s):
        return (pltpu.make_async_copy(wgu_hbm.at[e], wgu_f[s], wsem.at[s, 0]),
                pltpu.make_async_copy(wd_hbm.at[e], wd_f[s], wsem.at[s, 1]))

    @pl.when(b == 0)
    def _():
        for cp in weight_copies(be_ref[0], 0):
            cp.start(priority=WEIGHT_DMA_PRIORITY)
        _gather_rows(tok_ref, tb, x_hbm, xbuf0, sem.at[0])

    for slot in range(2):
        @pl.when((b + 1 < n_used) & ((b + 1) % 2 == slot))
        def _():
            _gather_rows(tok_next_ref, tb, x_hbm, bufs[slot], sem.at[slot])

    @pl.when(b < n_used)
    def _():
        for s in range(2):
            @pl.when((first_ref[b] == 1) & (wslot_ref[b] == s))
            def _():
                for cp in weight_copies(be_ref[b], s):
                    cp.wait()
                wgu_bf[...] = wgu_f[s][...].astype(MXU_DTYPE)
                wd_bf[...] = wd_f[s][...].astype(MXU_DTYPE)

                @pl.when(nxt_ref[b] >= 0)
                def _():
                    for cp in weight_copies(nxt_ref[b], 1 - s):
                        cp.start(priority=WEIGHT_DMA_PRIORITY)

        def compute(xbuf, slot):
            _wait_rows(x_hbm, xbuf, sem.at[slot])
            xr = _rows_as_matrix(xbuf, 0, tb)
            gu = _mm(xr, wgu_bf[...]) + bgu_ref[...]
            gt = jnp.minimum(gu[:, :D_FF], SWIGLU_LIMIT)
            up = jnp.clip(gu[:, D_FF:], -SWIGLU_LIMIT, SWIGLU_LIMIT)
            act = gt * jax.nn.sigmoid(SWIGLU_ALPHA * gt) * (up + 1.0)
            _store_rows(y_ref, _mm(act, wd_bf[...]) + bd_ref[...])

        for slot in range(2):
            pl.when(b % 2 == slot)(functools.partial(compute, bufs[slot], slot))

    @pl.when(b >= n_used)
    def _():
        y_ref[...] = jnp.zeros(y_ref.shape, y_ref.dtype)


def _experts(blk_exp, n_used, row_tok, x1_rows, w_gu, b_gu, w_down, b_down, tb):
    n_blocks = blk_exp.shape[0]
    blk = jnp.arange(n_blocks, dtype=jnp.int32)
    first = jnp.concatenate([jnp.ones((1,), jnp.int32), (blk_exp[1:] != blk_exp[:-1]).astype(jnp.int32)])
    wslot = (jnp.cumsum(first) - 1) % 2
    at_or_after = lax.cummin(jnp.where(first == 1, blk, n_blocks), reverse=True)
    next_first = jnp.concatenate([at_or_after[1:], jnp.full((1,), n_blocks, jnp.int32)])
    nxt = jnp.where(next_first < n_used[0], blk_exp[jnp.minimum(next_first, n_blocks - 1)], -1)
    bspec = lambda c: pl.BlockSpec((None, 1, c), lambda b, be, *_: (be[b], 0, 0))
    tok = row_tok.reshape(n_blocks, 1, tb)
    return pl.pallas_call(
        functools.partial(_expert_kernel, tb=tb),
        grid_spec=pltpu.PrefetchScalarGridSpec(
            num_scalar_prefetch=5,
            grid=(n_blocks,),
            in_specs=[pl.BlockSpec((1, 1, tb), lambda b, *_: (b, 0, 0), memory_space=pltpu.SMEM),
                      pl.BlockSpec((1, 1, tb), lambda b, *_: (jnp.minimum(b + 1, n_blocks - 1), 0, 0),
                                   memory_space=pltpu.SMEM),
                      pl.BlockSpec(memory_space=pl.ANY),
                      pl.BlockSpec(memory_space=pl.ANY), bspec(2 * D_FF),
                      pl.BlockSpec(memory_space=pl.ANY), bspec(D_MODEL)],
            out_specs=pl.BlockSpec((tb * ROW_TILES, LANES), lambda b, *_: (b, 0)),
            scratch_shapes=[pltpu.VMEM((tb * ROW_TILES, LANES), jnp.float32),
                            pltpu.VMEM((tb * ROW_TILES, LANES), jnp.float32),
                            pltpu.VMEM((D_MODEL, 2 * D_FF), jnp.float32),
                            pltpu.VMEM((D_MODEL, 2 * D_FF), jnp.float32),
                            pltpu.VMEM((D_FF, D_MODEL), jnp.float32),
                            pltpu.VMEM((D_FF, D_MODEL), jnp.float32),
                            pltpu.VMEM((D_MODEL, 2 * D_FF), MXU_DTYPE),
                            pltpu.VMEM((D_FF, D_MODEL), MXU_DTYPE),
                            pltpu.SemaphoreType.DMA((2,)),
                            pltpu.SemaphoreType.DMA((2, 2))]),
        out_shape=jax.ShapeDtypeStruct((n_blocks * tb * ROW_TILES, LANES), jnp.float32),
        compiler_params=_cparams(1),
        name="moe_experts",
    )(blk_exp, n_used, first, nxt.astype(jnp.int32), wslot.astype(jnp.int32), tok, tok, x1_rows,
      w_gu, b_gu, w_down, b_down)


def _combine_kernel(dest_ref, dest_next_ref, y_hbm, gate_ref, x1_ref, p_ref, wpg_ref, wpp_ref, g_ref, b_ref,
                    o_ref, ybuf0, ybuf1, sem, *, tm):
    i = pl.program_id(0)
    bufs = (ybuf0, ybuf1)

    @pl.when(i == 0)
    def _():
        _gather_rows(dest_ref, TOP_K * tm, y_hbm, ybuf0, sem.at[0], both_queues=True)

    for slot in range(2):
        @pl.when((i + 1 < pl.num_programs(0)) & ((i + 1) % 2 == slot))
        def _():
            _gather_rows(dest_next_ref, TOP_K * tm, y_hbm, bufs[slot], sem.at[slot], both_queues=True)

    x1 = _rows_as_matrix(x1_ref, 0, tm)
    ple = jax.nn.sigmoid(_mm(x1, wpg_ref[...])) * _mm(p_ref[...], wpp_ref[...])
    h = DN_ALPHA * x1 + ple
    gate = gate_ref[...]

    def finish(ybuf, slot):
        _wait_rows(y_hbm, ybuf, sem.at[slot])
        ffn = h
        for kk in range(TOP_K):
            ffn = ffn + _rows_as_matrix(ybuf, kk * tm, tm) * gate[:, kk:kk + 1]
        o_ref[...] = _layer_norm(ffn, g_ref[...], b_ref[...])

    for slot in range(2):
        pl.when(i % 2 == slot)(functools.partial(finish, bufs[slot], slot))


def _combine(dest, yr, gates, x1_rows, p_all, layer, wpg, wpp, g, b, tm=256):
    n = gates.shape[0]
    nt = n // tm
    full = lambda shape: pl.BlockSpec(shape, lambda i: (0,) * len(shape))
    row = lambda w: pl.BlockSpec((tm, w), lambda i: (i, 0))
    dest_t = dest.reshape(nt, tm, TOP_K).transpose(0, 2, 1).reshape(nt, 1, TOP_K * tm)
    return pl.pallas_call(
        functools.partial(_combine_kernel, tm=tm),
        grid=(nt,),
        in_specs=[pl.BlockSpec((1, 1, TOP_K * tm), lambda i: (i, 0, 0), memory_space=pltpu.SMEM),
                  pl.BlockSpec((1, 1, TOP_K * tm), lambda i: (jnp.minimum(i + 1, nt - 1), 0, 0),
                               memory_space=pltpu.SMEM),
                  pl.BlockSpec(memory_space=pl.ANY),
                  row(LANES), pl.BlockSpec((tm * ROW_TILES, LANES), lambda i: (i, 0)),
                  pl.BlockSpec((tm, PLE_DIM), lambda i: (layer * nt + i, 0)),
                  full((D_MODEL, D_MODEL)), full((PLE_DIM, D_MODEL)),
                  full((1, D_MODEL)), full((1, D_MODEL))],
        out_specs=row(D_MODEL),
        out_shape=jax.ShapeDtypeStruct((n, D_MODEL), jnp.float32),
        scratch_shapes=[pltpu.VMEM((TOP_K * tm * ROW_TILES, LANES), jnp.float32),
                        pltpu.VMEM((TOP_K * tm * ROW_TILES, LANES), jnp.float32),
                        pltpu.SemaphoreType.DMA((2,))],
        compiler_params=_cparams(1),
        name="combine_ple_ln2",
    )(dest_t, dest_t, yr, gates, x1_rows, p_all, wpg, wpp, g, b)


def _route(top_idx, n_tokens, tb):
    a = n_tokens * TOP_K
    n_blocks = a // tb + N_EXPERTS
    onehot = (top_idx[:, :, None] == jnp.arange(N_EXPERTS, dtype=jnp.int32)[None, None, :]).astype(jnp.int32)
    member = onehot.sum(axis=1)
    rank = jnp.cumsum(member, axis=0) - member
    counts = member.sum(axis=0)
    padded = (counts + tb - 1) // tb * tb
    pend = jnp.cumsum(padded)
    pstart = pend - padded
    dest = jnp.take_along_axis(rank + pstart[None, :], top_idx, axis=1).astype(jnp.int32)
    blk_start = jnp.arange(n_blocks, dtype=jnp.int32) * tb
    blk_exp = jnp.minimum((blk_start[:, None] >= pend[None, :]).astype(jnp.int32).sum(axis=1), N_EXPERTS - 1)
    n_used = (pend[-1] // tb).astype(jnp.int32).reshape(1)
    tok = jnp.arange(n_tokens, dtype=jnp.int32)[:, None]
    pairs = jnp.sort((top_idx * n_tokens + tok).reshape(-1))
    first = jnp.cumsum(counts) - counts
    of_blk = (blk_exp[:, None] == jnp.arange(N_EXPERTS, dtype=jnp.int32)[None, :]).astype(jnp.int32)
    per_blk = lambda v: (of_blk * v[None, :]).sum(axis=1)
    j0 = blk_start - per_blk(pstart)
    lane = jnp.arange(tb, dtype=jnp.int32)[None, :]
    src = jnp.clip((per_blk(first) + j0)[:, None] + lane, 0, a - 1)
    valid = j0[:, None] + lane < per_blk(counts)[:, None]
    row_tok = jnp.where(valid, pairs[src] % n_tokens, 0).astype(jnp.int32)
    return dest, row_tok, blk_exp, n_used


def kernel(x, p, ln0_g, ln0_b, w_in, pool_w, pool_scale, w_up_attn, w_up_pool, w_out, ln1_g, ln1_b,
           router_w, router_b, exp_w_gu, exp_b_gu, exp_w_down, exp_b_down, ple_w_gate, ple_w_proj,
           ln2_g, ln2_b):
    batch, seq, d = x.shape
    assert d == D_MODEL
    n = batch * seq
    tb = 256
    kc = 512
    tables = _rope_tables(seq)
    bf = lambda a: a.astype(MXU_DTYPE)
    vec = lambda a: a.reshape(1, -1)
    n_le = DEPTH * N_EXPERTS
    w_gu_all = exp_w_gu.reshape(n_le, D_MODEL, 2 * D_FF)
    b_gu_all = exp_b_gu.reshape(n_le, 1, 2 * D_FF)
    w_down_all = exp_w_down.reshape(n_le, D_FF, D_MODEL)
    b_down_all = exp_b_down.reshape(n_le, 1, D_MODEL)
    p_all = p.reshape(DEPTH * n, PLE_DIM)

    h = x.reshape(n, d)
    for i in range(DEPTH):
        w = w_in[i]
        w1 = bf(jnp.concatenate([w[:, :1352], jnp.zeros((d, OFF_U - 1352), w.dtype), w[:, 1352:1864]], axis=1))
        wg = bf(w[:, 1864:])
        if i == 0:
            q, k, v_t, qi, ki, wi_t, u, h = _inproj(h, w1, tables, seq, kc, entry_ln=(ln0_g, ln0_b))
        else:
            q, k, v_t, qi, ki, wi_t, u = _inproj(h, w1, tables, seq, kc)
        y_attn = _attention(qi, wi_t, q, ki, k, v_t, batch, seq, 512, kc)
        rw = bf(jnp.pad(router_w[i], ((0, 0), (0, LANES - N_EXPERTS))))
        rb = jnp.pad(router_b[i], (0, LANES - N_EXPERTS)).reshape(1, -1)
        x1_rows, idx_l, gate_l = _merge(h, y_attn, u, wg, bf(w_up_attn[i]), bf(pool_w[i]), vec(pool_scale[i]),
                                        bf(w_up_pool[i]), bf(w_out[i]), vec(ln1_g[i]), vec(ln1_b[i]), rw, rb, seq)
        dest, row_tok, blk_exp, n_used = _route(idx_l[:, :TOP_K], n, tb)
        yr = _experts(blk_exp + i * N_EXPERTS, n_used, row_tok, x1_rows,
                      w_gu_all, b_gu_all, w_down_all, b_down_all, tb)
        h = _combine(dest, yr, gate_l, x1_rows, p_all, i, bf(ple_w_gate[i]), bf(ple_w_proj[i]),
                     vec(ln2_g[i]), vec(ln2_b[i]))
    return h.reshape(batch, seq, d)
```
